```python
import math
import jax
import jax.numpy as jnp
from jax import lax
import numpy as np

D_MODEL = 1024
BATCH = 8
SEQ = 2048
DEPTH = 4

GRID_W = 64
CTX_LEN = 256
GROUP_W = D_MODEL // 4
D_MIX = 4 * GROUP_W
NA_HEADS = 4
NA_HD = GROUP_W // NA_HEADS
NA_KH = 8
NA_KW = 16
MLA_HEADS = 4
MLA_NOPE = 64
MLA_ROPE = 32
MLA_V = GROUP_W // MLA_HEADS
MLA_Q_RANK = 256
MLA_KV_RANK = 128
DIFF_HEADS = 4
DIFF_V = GROUP_W // DIFF_HEADS
DIFF_QK = DIFF_V // 2
SSD_D_INNER = GROUP_W
SSD_HD = 64
SSD_HEADS = SSD_D_INNER // SSD_HD
SSD_GROUPS = 2
SSD_STATE = 128
SSD_CONV = 5
SSD_CHUNK = 64
SSD_XBC = SSD_D_INNER + 2 * SSD_GROUPS * SSD_STATE
NA_IN = 3 * NA_HEADS * NA_HD
MLA_IN = MLA_Q_RANK + MLA_KV_RANK + MLA_ROPE
DIFF_IN = 2 * DIFF_HEADS * 2 * DIFF_QK + DIFF_HEADS * DIFF_V
SSD_IN = SSD_D_INNER + SSD_XBC + 2 * SSD_HEADS
N_IN = NA_IN + MLA_IN + DIFF_IN + SSD_IN
SPLIT_AT = [NA_IN, NA_IN + MLA_IN, NA_IN + MLA_IN + DIFF_IN]
ROPE_DIM = 32
ROPE_BASE = 10000.0
Q_BLOCK = 128
MOE_GROUPS = 4
MOE_EPG = 4
MOE_EXPERTS = MOE_GROUPS * MOE_EPG
MOE_TOP_K = 2
MOE_FF = 512
MOE_BLOCK = 128
NORM_EPS = 1e-6

kernel_name = 'hybrid_grid_flow_block'


def _rmsnorm(x, g):
    xf = x.astype(jnp.float32)
    y = xf * lax.rsqrt(jnp.mean(xf * xf, axis=-1, keepdims=True) + NORM_EPS)
    return (y * g.astype(jnp.float32)).astype(x.dtype)


def _axial_rope(n_tok, dtype):
    n_freq = ROPE_DIM // 4
    inv = jnp.power(ROPE_BASE, -jnp.arange(n_freq, dtype=jnp.float32) / n_freq)
    t = jnp.arange(n_tok, dtype=jnp.int32)
    row = (t // GRID_W).astype(jnp.float32)
    col = (t % GRID_W).astype(jnp.float32)
    ang = jnp.concatenate([row[:, None] * inv, col[:, None] * inv], axis=-1)
    return jnp.cos(ang).astype(dtype), jnp.sin(ang).astype(dtype)


def _apply_rope(x, cos, sin):
    half = x.shape[-1] // 2
    x1, x2 = x[..., :half], x[..., half:]
    return jnp.concatenate([x1 * cos - x2 * sin, x1 * sin + x2 * cos], axis=-1)


def _softmax_attend(q, k, v, scale):
    s = jnp.einsum('bqhd,bkhd->bhqk', q, k, preferred_element_type=jnp.float32) * scale
    p = jax.nn.softmax(s, axis=-1).astype(v.dtype)
    return jnp.einsum('bhqk,bkhd->bqhd', p, v)


def _diff_attend(q1, q2, k1, k2, v, lam, scale):
    s1 = jnp.einsum('bqhd,bkhd->bhqk', q1, k1, preferred_element_type=jnp.float32) * scale
    s2 = jnp.einsum('bqhd,bkhd->bhqk', q2, k2, preferred_element_type=jnp.float32) * scale
    p = (jax.nn.softmax(s1, axis=-1) - lam * jax.nn.softmax(s2, axis=-1)).astype(v.dtype)
    return jnp.einsum('bhqk,bkhd->bqhd', p, v)


def _over_query_blocks(fn, *qs):
    b, s = qs[0].shape[:2]
    nb = s // Q_BLOCK
    blocks = tuple(q.reshape((b, nb, Q_BLOCK) + q.shape[2:]).swapaxes(0, 1) for q in qs)
    out = lax.map(lambda qb: fn(*qb), blocks)
    return out.swapaxes(0, 1).reshape((b, s) + out.shape[3:])


def _na_mixer(p_ctx, p_lat, g_q, g_k, rel_bias, with_ctx_out):
    def heads(p):
        b, n = p.shape[:2]
        q, k, v = (t.reshape(b, n, NA_HEADS, NA_HD) for t in jnp.split(p, 3, axis=-1))
        return _rmsnorm(q, g_q), _rmsnorm(k, g_k), v
    qc, kc, vc = heads(p_ctx)
    ql, kl, vl = heads(p_lat)
    b, n = p_lat.shape[:2]
    n_ctx = kc.shape[1]
    rows = n // GRID_W
    kh = min(NA_KH, rows)
    scale = NA_HD ** -0.5
    r = np.arange(rows)
    row_idx = np.clip(r - kh // 2, 0, rows - kh)[:, None] + np.arange(kh)[None, :]
    cq = np.arange(GRID_W)
    col_lo = np.clip(cq - NA_KW // 2, 0, GRID_W - NA_KW)
    col_mask = (cq[None, :] >= col_lo[:, None]) & (cq[None, :] < col_lo[:, None] + NA_KW)
    row_off = row_idx - r[:, None] + (NA_KH - 1)
    col_off = np.clip(cq[None, :] - cq[:, None], 1 - NA_KW, NA_KW - 1) + (NA_KW - 1)
    bias = rel_bias[:, row_off[:, None, :, None], col_off[None, :, None, :]]
    bias = jnp.where(col_mask[None, None, :, None, :], bias, -jnp.inf)
    qg = ql.reshape(b, rows, GRID_W, NA_HEADS, NA_HD)
    kg = kl.reshape(b, rows, GRID_W, NA_HEADS, NA_HD)[:, row_idx]
    vg = vl.reshape(b, rows, GRID_W, NA_HEADS, NA_HD)[:, row_idx]
    s_nb = jnp.einsum('brqhd,brkwhd->bhrqkw', qg, kg, preferred_element_type=jnp.float32) * scale + bias[None]
    s_nb = s_nb.reshape(b, NA_HEADS, rows, GRID_W, kh * GRID_W)
    s_cx = jnp.einsum('brqhd,bchd->bhrqc', qg, kc, preferred_element_type=jnp.float32) * scale
    p = jax.nn.softmax(jnp.concatenate([s_cx, s_nb], axis=-1), axis=-1).astype(vl.dtype)
    p_nb = p[..., n_ctx:].reshape(b, NA_HEADS, rows, GRID_W, kh, GRID_W)
    o = (jnp.einsum('bhrqc,bchd->brqhd', p[..., :n_ctx], vc)
         + jnp.einsum('bhrqkw,brkwhd->brqhd', p_nb, vg))
    o_lat = o.reshape(b, n, NA_HEADS * NA_HD)
    o_ctx = _softmax_attend(qc, kc, vc, scale).reshape(b, n_ctx, -1) if with_ctx_out else None
    return o_ctx, o_lat


def _mla_project(p, g_qa, w_qb, g_kva, w_kvb, g_q, g_k, cos, sin):
    b, n = p.shape[:2]
    cq, ckv, kr = jnp.split(p, [MLA_Q_RANK, MLA_Q_RANK + MLA_KV_RANK], axis=-1)
    q = (_rmsnorm(cq, g_qa) @ w_qb).reshape(b, n, MLA_HEADS, MLA_NOPE + MLA_ROPE)
    kv = (_rmsnorm(ckv, g_kva) @ w_kvb).reshape(b, n, MLA_HEADS, MLA_NOPE + MLA_V)
    k_nope, v = jnp.split(kv, [MLA_NOPE], axis=-1)
    k = jnp.concatenate([k_nope, jnp.broadcast_to(kr[:, :, None, :], (b, n, MLA_HEADS, MLA_ROPE))], axis=-1)
    q = _rmsnorm(q, g_q)
    k = _rmsnorm(k, g_k)
    if cos is not None:
        q = jnp.concatenate([q[..., :MLA_NOPE], _apply_rope(q[..., MLA_NOPE:], cos[:, None], sin[:, None])], axis=-1)
        k = jnp.concatenate([k[..., :MLA_NOPE], _apply_rope(k[..., MLA_NOPE:], cos[:, None], sin[:, None])], axis=-1)
    return q, k, v


def _mla_mixer(p_ctx, p_lat, cos, sin, g_qa, w_qb, g_kva, w_kvb, g_q, g_k, with_ctx_out):
    qc, kc, vc = _mla_project(p_ctx, g_qa, w_qb, g_kva, w_kvb, g_q, g_k, None, None)
    ql, kl, vl = _mla_project(p_lat, g_qa, w_qb, g_kva, w_kvb, g_q, g_k, cos, sin)
    scale = (MLA_NOPE + MLA_ROPE) ** -0.5
    k_all = jnp.concatenate([kc, kl], axis=1)
    v_all = jnp.concatenate([vc, vl], axis=1)
    o_lat = _over_query_blocks(lambda qb: _softmax_attend(qb, k_all, v_all, scale), ql)
    b, n = p_lat.shape[:2]
    o_ctx = _softmax_attend(qc, kc, vc, scale).reshape(b, kc.shape[1], -1) if with_ctx_out else None
    return o_ctx, o_lat.reshape(b, n, -1)


def _diff_mixer(p_ctx, p_lat, cos, sin, g_q, g_k, lam_vecs, g_sub, lam_init, with_ctx_out):
    def heads(p, cos_t, sin_t):
        b, n = p.shape[:2]
        nq = DIFF_HEADS * 2 * DIFF_QK
        q, k, v = jnp.split(p, [nq, 2 * nq], axis=-1)
        q = _rmsnorm(q.reshape(b, n, DIFF_HEADS, 2, DIFF_QK), g_q)
        k = _rmsnorm(k.reshape(b, n, DIFF_HEADS, 2, DIFF_QK), g_k)
        if cos_t is not None:
            q = _apply_rope(q, cos_t[:, None, None], sin_t[:, None, None])
            k = _apply_rope(k, cos_t[:, None, None], sin_t[:, None, None])
        return q[..., 0, :], q[..., 1, :], k[..., 0, :], k[..., 1, :], v.reshape(b, n, DIFF_HEADS, DIFF_V)
    lv = lam_vecs.astype(jnp.float32)
    lam = jnp.exp(jnp.sum(lv[0] * lv[1])) - jnp.exp(jnp.sum(lv[2] * lv[3])) + lam_init
    scale = DIFF_QK ** -0.5
    q1c, q2c, k1c, k2c, vc = heads(p_ctx, None, None)
    q1l, q2l, k1l, k2l, vl = heads(p_lat, cos, sin)
    k1 = jnp.concatenate([k1c, k1l], axis=1)
    k2 = jnp.concatenate([k2c, k2l], axis=1)
    v_all = jnp.concatenate([vc, vl], axis=1)
    o_lat = _over_query_blocks(lambda q1b, q2b: _diff_attend(q1b, q2b, k1, k2, v_all, lam, scale), q1l, q2l)

    def finish(o):
        return (_rmsnorm(o, g_sub) * (1.0 - lam_init)).reshape(o.shape[0], o.shape[1], -1)
    o_ctx = finish(_diff_attend(q1c, q2c, k1c, k2c, vc, lam, scale)) if with_ctx_out else None
    return o_ctx, finish(o_lat)


def _dwconv_centred(u, w, bias):
    k = w.shape[0]
    out = lax.conv_general_dilated(u, w[:, None, :], window_strides=(1,), padding=[(k // 2, k // 2)],
                                   dimension_numbers=('NWC', 'WIO', 'NWC'), feature_group_count=u.shape[-1])
    return out + bias


def _ssd_chunked(x, dt, a, b_in, c_in, h0):
    bsz, seqlen, nh, hp = x.shape
    ng, ns = b_in.shape[2], b_in.shape[3]
    nc = seqlen // SSD_CHUNK
    rep = nh // ng
    xdt = (x * dt[..., None]).reshape(bsz, nc, SSD_CHUNK, nh, hp)
    bh = jnp.repeat(b_in, rep, axis=2).reshape(bsz, nc, SSD_CHUNK, nh, ns)
    ch = jnp.repeat(c_in, rep, axis=2).reshape(bsz, nc, SSD_CHUNK, nh, ns)
    la = (dt.astype(jnp.float32) * a).reshape(bsz, nc, SSD_CHUNK, nh).transpose(0, 1, 3, 2)
    la_cum = jnp.cumsum(la, axis=-1)
    tril = np.tril(np.ones((SSD_CHUNK, SSD_CHUNK), dtype=bool))
    seg = la_cum[..., :, None] - la_cum[..., None, :]
    decay = jnp.where(tril, jnp.exp(jnp.where(tril, seg, 0.0)), 0.0).astype(x.dtype)
    scores = jnp.einsum('bclhn,bcshn->bchls', ch, bh) * decay
    y_diag = jnp.einsum('bchls,bcshp->bclhp', scores, xdt)
    to_end = jnp.exp(la_cum[..., -1:] - la_cum).astype(x.dtype)
    chunk_states = jnp.einsum('bclhn,bchl,bclhp->bchpn', bh, to_end, xdt)
    chunk_decay = jnp.exp(la_cum[..., -1]).astype(x.dtype)

    def step(h, inp):
        st, dec = inp
        return h * dec[..., None, None] + st, h
    h_final, h_enter = lax.scan(step, h0, (jnp.moveaxis(chunk_states, 1, 0), jnp.moveaxis(chunk_decay, 1, 0)))
    h_enter = jnp.moveaxis(h_enter, 0, 1)
    y_off = jnp.einsum('bclhn,bchpn,bchl->bclhp', ch, h_enter, jnp.exp(la_cum).astype(x.dtype))
    return (y_diag + y_off).reshape(bsz, seqlen, nh, hp), h_final


def _ssd_mixer(p_ctx, p_lat, conv_w, conv_b, dt_bias, a_log, d_skip, g_norm, with_ctx_out):
    def prep(p):
        b, n = p.shape[:2]
        z, xbc, dt_raw = jnp.split(p, [SSD_D_INNER, SSD_D_INNER + SSD_XBC], axis=-1)
        xbc = jax.nn.silu(_dwconv_centred(xbc, conv_w, conv_b))
        xs, bs, cs = jnp.split(xbc, [SSD_D_INNER, SSD_D_INNER + SSD_GROUPS * SSD_STATE], axis=-1)
        return (z, xs.reshape(b, n, SSD_HEADS, SSD_HD), bs.reshape(b, n, SSD_GROUPS, SSD_STATE),
                cs.reshape(b, n, SSD_GROUPS, SSD_STATE), dt_raw.reshape(b, n, 2, SSD_HEADS))
    zc, xc, bc, cc, dtc = prep(p_ctx)
    zl, xl, bl, cl, dtl = prep(p_lat)
    bsz = xl.shape[0]
    ys_c, ys_l = [], []
    for d in range(2):
        flip = (lambda u: jnp.flip(u, axis=1)) if d == 1 else (lambda u: u)
        a = -jnp.exp(a_log[d].astype(jnp.float32))
        dt_c = jax.nn.softplus(dtc[:, :, d] + dt_bias[d])
        dt_l = jax.nn.softplus(dtl[:, :, d] + dt_bias[d])
        h0 = jnp.zeros((bsz, SSD_HEADS, SSD_HD, SSD_STATE), xl.dtype)
        yc, h_ctx = _ssd_chunked(flip(xc), flip(dt_c), a, flip(bc), flip(cc), h0)
        yl, _ = _ssd_chunked(flip(xl), flip(dt_l), a, flip(bl), flip(cl), h_ctx)
        ys_c.append(flip(yc) + d_skip[d][:, None] * xc)
        ys_l.append(flip(yl) + d_skip[d][:, None] * xl)

    def finish(y, z):
        b, n = z.shape[:2]
        return _rmsnorm(y.reshape(b, n, SSD_D_INNER) * jax.nn.silu(z), g_norm)
    o_ctx = finish(ys_c[0] + ys_c[1], zc) if with_ctx_out else None
    return o_ctx, finish(ys_l[0] + ys_l[1], zl)


def _moe_dispatch(h, expert_idx, expert_w, w_gate, w_up, w_down):
    n_tok, d = h.shape
    k = expert_idx.shape[1]
    n_asg = n_tok * k
    n_exp = w_gate.shape[0]
    flat_e = expert_idx.reshape(-1)
    order = jnp.argsort(flat_e)
    sorted_e = flat_e[order]
    counts = jnp.bincount(flat_e, length=n_exp)
    padded = (counts + MOE_BLOCK - 1) // MOE_BLOCK * MOE_BLOCK
    pad_end = jnp.cumsum(padded)
    pad_start = pad_end - padded
    seg_start = jnp.cumsum(counts) - counts
    dest = pad_start[sorted_e] + jnp.arange(n_asg, dtype=jnp.int32) - seg_start[sorted_e]
    n_blocks = -(-n_asg // MOE_BLOCK) + n_exp
    cap = n_blocks * MOE_BLOCK
    slot_tok = jnp.full((cap,), n_tok, jnp.int32).at[dest].set((order // k).astype(jnp.int32))
    slot_w = jnp.zeros((cap,), h.dtype).at[dest].set(expert_w.reshape(-1)[order])
    block_e = jnp.minimum(jnp.searchsorted(pad_end, jnp.arange(n_blocks) * MOE_BLOCK, side='right'), n_exp - 1)
    h_pad = jnp.concatenate([h, jnp.zeros((1, d), h.dtype)], axis=0)

    def run_block(args):
        tok, e = args
        xb = h_pad[tok]
        return (jax.nn.silu(xb @ w_gate[e]) * (xb @ w_up[e])) @ w_down[e]
    y = lax.map(run_block, (slot_tok.reshape(n_blocks, MOE_BLOCK), block_e)).reshape(cap, d)
    out = jnp.zeros((n_tok + 1, d), h.dtype).at[slot_tok].add(y * slot_w[:, None])
    return out[:n_tok]


def _hier_moe(h, w_group, b_group, w_expert, b_expert, w_gate, w_up, w_down):
    n_tok = h.shape[0]
    g_prob = jax.nn.softmax((h @ w_group + b_group).astype(jnp.float32), axis=-1)
    g_top_p, g_top = lax.top_k(g_prob, 1)
    e_logits = (h @ w_expert + b_expert).astype(jnp.float32).reshape(n_tok, MOE_GROUPS, MOE_EPG)
    e_in = jnp.take_along_axis(e_logits, g_top[:, :, None], axis=1)[:, 0]
    e_top_l, e_top = lax.top_k(e_in, MOE_TOP_K)
    e_w = jax.nn.softmax(e_top_l, axis=-1) * g_top_p
    expert_idx = g_top * MOE_EPG + e_top
    return _moe_dispatch(h, expert_idx, e_w.astype(h.dtype), w_gate, w_up, w_down)


def setup_inputs(seed: int = 0) -> dict:
    key = jax.random.key(seed)
    keys = iter(jax.random.split(key, 40))
    f32 = jnp.float32
    L, D = DEPTH, D_MODEL

    def nrm(shape, std):
        return jax.random.normal(next(keys), shape, f32) * std

    def gain(shape):
        return 1.0 + nrm(shape, 0.02)
    x = nrm((BATCH, SEQ, D), 1.0)
    c = nrm((BATCH, D), 1.0)
    ctx = nrm((BATCH, CTX_LEN, D), 1.0)
    c_ctx = nrm((D,), 1.0)
    w_mod = nrm((L, D, 6 * D), 0.5 * D ** -0.5)
    b_mod = nrm((L, 6 * D), 0.02)
    g_mix = gain((L, D))
    w_in = nrm((L, D, N_IN), D ** -0.5)
    w_out = nrm((L, D_MIX, D), D_MIX ** -0.5)
    na_g_q = gain((L, NA_HD))
    na_g_k = gain((L, NA_HD))
    na_rel_bias = nrm((L, NA_HEADS, 2 * NA_KH - 1, 2 * NA_KW - 1), 0.1)
    mla_g_qa = gain((L, MLA_Q_RANK))
    mla_w_qb = nrm((L, MLA_Q_RANK, MLA_HEADS * (MLA_NOPE + MLA_ROPE)), MLA_Q_RANK ** -0.5)
    mla_g_kva = gain((L, MLA_KV_RANK))
    mla_w_kvb = nrm((L, MLA_KV_RANK, MLA_HEADS * (MLA_NOPE + MLA_V)), MLA_KV_RANK ** -0.5)
    mla_g_q = gain((L, MLA_NOPE + MLA_ROPE))
    mla_g_k = gain((L, MLA_NOPE + MLA_ROPE))
    diff_g_q = gain((L, DIFF_QK))
    diff_g_k = gain((L, DIFF_QK))
    diff_lambda = nrm((L, 4, DIFF_QK), 0.1)
    diff_g_sub = gain((L, DIFF_V))
    ssd_conv_w = nrm((L, SSD_CONV, SSD_XBC), SSD_CONV ** -0.5)
    ssd_conv_b = nrm((L, SSD_XBC), 0.02)
    dt0 = jnp.exp(jax.random.uniform(next(keys), (L, 2, SSD_HEADS), f32, math.log(1e-3), math.log(1e-1)))
    ssd_dt_bias = dt0 + jnp.log(-jnp.expm1(-dt0))
    ssd_a_log = jnp.log(jax.random.uniform(next(keys), (L, 2, SSD_HEADS), f32, 1.0, 16.0))
    ssd_d = 1.0 + nrm((L, 2, SSD_HEADS), 0.1)
    ssd_g_norm = gain((L, SSD_D_INNER))
    g_ffn = gain((L, D))
    moe_w_group = nrm((L, D, MOE_GROUPS), D ** -0.5)
    moe_b_group = nrm((L, MOE_GROUPS), 0.01)
    moe_w_expert = nrm((L, D, MOE_EXPERTS), D ** -0.5)
    moe_b_expert = nrm((L, MOE_EXPERTS), 0.01)
    moe_w_gate = nrm((L, MOE_EXPERTS, D, MOE_FF), D ** -0.5)
    moe_w_up = nrm((L, MOE_EXPERTS, D, MOE_FF), D ** -0.5)
    moe_w_down = nrm((L, MOE_EXPERTS, MOE_FF, D), MOE_FF ** -0.5)
    return {'x': x, 'c': c, 'ctx': ctx, 'c_ctx': c_ctx, 'w_mod': w_mod, 'b_mod': b_mod,
            'g_mix': g_mix, 'w_in': w_in, 'w_out': w_out,
            'na_g_q': na_g_q, 'na_g_k': na_g_k, 'na_rel_bias': na_rel_bias,
            'mla_g_qa': mla_g_qa, 'mla_w_qb': mla_w_qb, 'mla_g_kva': mla_g_kva, 'mla_w_kvb': mla_w_kvb,
            'mla_g_q': mla_g_q, 'mla_g_k': mla_g_k,
            'diff_g_q': diff_g_q, 'diff_g_k': diff_g_k, 'diff_lambda': diff_lambda, 'diff_g_sub': diff_g_sub,
            'ssd_conv_w': ssd_conv_w, 'ssd_conv_b': ssd_conv_b, 'ssd_dt_bias': ssd_dt_bias,
            'ssd_a_log': ssd_a_log, 'ssd_d': ssd_d, 'ssd_g_norm': ssd_g_norm,
            'g_ffn': g_ffn, 'moe_w_group': moe_w_group, 'moe_b_group': moe_b_group,
            'moe_w_expert': moe_w_expert, 'moe_b_expert': moe_b_expert,
            'moe_w_gate': moe_w_gate, 'moe_w_up': moe_w_up, 'moe_w_down': moe_w_down}


def reference(x, c, ctx, c_ctx, w_mod, b_mod, g_mix, w_in, w_out, na_g_q, na_g_k, na_rel_bias,
              mla_g_qa, mla_w_qb, mla_g_kva, mla_w_kvb, mla_g_q, mla_g_k,
              diff_g_q, diff_g_k, diff_lambda, diff_g_sub,
              ssd_conv_w, ssd_conv_b, ssd_dt_bias, ssd_a_log, ssd_d, ssd_g_norm,
              g_ffn, moe_w_group, moe_b_group, moe_w_expert, moe_b_expert, moe_w_gate, moe_w_up, moe_w_down):
    bsz, n_lat, d_model = x.shape
    n_ctx = ctx.shape[1]
    cos, sin = _axial_rope(n_lat, x.dtype)
    silu_c = jax.nn.silu(c)
    silu_cc = jax.nn.silu(c_ctx)
    h_lat, h_ctx = x, ctx
    for layer in range(DEPTH):
        ctx_out = layer < DEPTH - 1
        mod_l = jnp.split((silu_c @ w_mod[layer] + b_mod[layer])[:, None, :], 6, axis=-1)
        mod_c = jnp.split(silu_cc @ w_mod[layer] + b_mod[layer], 6, axis=-1)
        a_lat = _rmsnorm(h_lat, g_mix[layer]) * (1 + mod_l[1]) + mod_l[0]
        a_ctx = _rmsnorm(h_ctx, g_mix[layer]) * (1 + mod_c[1]) + mod_c[0]
        p_lat = jnp.split(a_lat @ w_in[layer], SPLIT_AT, axis=-1)
        p_ctx = jnp.split(a_ctx @ w_in[layer], SPLIT_AT, axis=-1)
        lam_init = 0.8 - 0.6 * math.exp(-0.3 * layer)
        outs = [
            _na_mixer(p_ctx[0], p_lat[0], na_g_q[layer], na_g_k[layer], na_rel_bias[layer], ctx_out),
            _mla_mixer(p_ctx[1], p_lat[1], cos, sin, mla_g_qa[layer], mla_w_qb[layer], mla_g_kva[layer],
                       mla_w_kvb[layer], mla_g_q[layer], mla_g_k[layer], ctx_out),
            _diff_mixer(p_ctx[2], p_lat[2], cos, sin, diff_g_q[layer], diff_g_k[layer], diff_lambda[layer],
                        diff_g_sub[layer], lam_init, ctx_out),
            _ssd_mixer(p_ctx[3], p_lat[3], ssd_conv_w[layer], ssd_conv_b[layer], ssd_dt_bias[layer],
                       ssd_a_log[layer], ssd_d[layer], ssd_g_norm[layer], ctx_out),
        ]
        h_lat = h_lat + mod_l[2] * (jnp.concatenate([o[1] for o in outs], axis=-1) @ w_out[layer])
        if ctx_out:
            h_ctx = h_ctx + mod_c[2] * (jnp.concatenate([o[0] for o in outs], axis=-1) @ w_out[layer])
        moe_p = (moe_w_group[layer], moe_b_group[layer], moe_w_expert[layer], moe_b_expert[layer],
                 moe_w_gate[layer], moe_w_up[layer], moe_w_down[layer])
        f_lat = (_rmsnorm(h_lat, g_ffn[layer]) * (1 + mod_l[4]) + mod_l[3]).reshape(-1, d_model)
        if ctx_out:
            f_ctx = (_rmsnorm(h_ctx, g_ffn[layer]) * (1 + mod_c[4]) + mod_c[3]).reshape(-1, d_model)
            y = _hier_moe(jnp.concatenate([f_ctx, f_lat], axis=0), *moe_p)
            n_ctx_tok = f_ctx.shape[0]
            h_ctx = h_ctx + mod_c[5] * y[:n_ctx_tok].reshape(bsz, n_ctx, d_model)
            y_lat = y[n_ctx_tok:]
        else:
            y_lat = _hier_moe(f_lat, *moe_p)
        h_lat = h_lat + mod_l[5] * y_lat.reshape(bsz, n_lat, d_model)
    return h_lat
```

```python
import functools
import math

import numpy as np
import jax
import jax.numpy as jnp
from jax import lax
from jax.experimental import pallas as pl
from jax.experimental.pallas import tpu as pltpu

F32 = jnp.float32
BF16 = jnp.bfloat16

D = 1024
GRID_W = 64
N_CTX = 256
GROUP_W = 256
HEADS = 4
NA_HD = 64
NA_KH = 8
NA_KW = 16
MLA_NOPE = 64
MLA_ROPE = 32
MLA_QK = MLA_NOPE + MLA_ROPE
MLA_QK_PAD = 128
MLA_V = 64
MLA_Q_RANK = 256
MLA_KV_RANK = 128
DIFF_QK = 32
DIFF_V = 64
SSD_INNER = 256
SSD_HD = 64
SSD_STATE = 128
SSD_GROUPS = 2
SSD_CONV = 5
SSD_XBC = 768
MOE_GROUPS = 4
MOE_EPG = 4
MOE_EXPERTS = 16
MOE_FF = 512
EPS = 1e-6

TM = 256
MOE_MB = 256
NA_R = 4
NA_W = NA_R + NA_KH - 1

P_NA = 0
P_MLA = 768
P_DIFF = 1280
P_SSD = 2048
P_W = 3200
SSD_W = 1152

VMEM_LIMIT = 56 * 1024 * 1024


def _cparams(n_axes):
    return pltpu.CompilerParams(dimension_semantics=("arbitrary",) * n_axes,
                                vmem_limit_bytes=VMEM_LIMIT)


def _dot(a, b):
    return jnp.dot(a, b, preferred_element_type=F32)


def _dot_nt(a, b):
    return lax.dot_general(a, b, (((1,), (1,)), ((), ())), preferred_element_type=F32)


def _dot_tn(a, b):
    return lax.dot_general(a, b, (((0,), (0,)), ((), ())), preferred_element_type=F32)


def _split_dot(x, m):
    hi = x.astype(BF16)
    r1 = x - hi.astype(F32)
    mid = r1.astype(BF16)
    lo = (r1 - mid.astype(F32)).astype(BF16)
    return _dot(hi, m) + _dot(mid, m) + _dot(lo, m)


def _rms(x, g):
    ms = jnp.mean(x * x, axis=-1, keepdims=True)
    return x * lax.rsqrt(ms + EPS) * g


def _seg_rms(x, bd, inv_n, g):
    ms = _split_dot(x * x, bd) * inv_n
    return x * lax.rsqrt(ms + EPS) * g


def _silu(x):
    return x * jax.nn.sigmoid(x)


def _rope(x, c, s1, s2, width):
    rot = 16
    return x * c + pltpu.roll(x, width - rot, 1) * s1 + pltpu.roll(x, rot, 1) * s2


def _mod_kernel(c_ref, w_ref, b_ref, o_ref):
    s = _silu(c_ref[...])
    o_ref[0] = _dot(s.astype(BF16), w_ref[0].astype(BF16)) + b_ref[0]


def _modulation(cvec, w_mod, b_mod):
    n_layers = w_mod.shape[0]
    tn = 1536
    return pl.pallas_call(
        _mod_kernel,
        grid=(n_layers, 6 * D // tn),
        in_specs=[pl.BlockSpec((16, D), lambda l, j: (0, 0)),
                  pl.BlockSpec((1, D, tn), lambda l, j: (l, 0, j)),
                  pl.BlockSpec((1, 1, tn), lambda l, j: (l, 0, j))],
        out_specs=pl.BlockSpec((1, 16, tn), lambda l, j: (l, 0, j)),
        out_shape=jax.ShapeDtypeStruct((n_layers, 16, 6 * D), F32),
        compiler_params=_cparams(2),
        name="modulation",
    )(cvec, w_mod, b_mod.reshape(n_layers, 1, 6 * D))


def _inproj_kernel(h_ref, mod_ref, gmix_ref, w_ref, bd64_ref, bd128_ref, bd32_ref,
                   nagq_ref, nagk_ref, gqa_ref, wqb_ref, gkva_ref, wkvb_ref, mgq_ref, mgk_ref,
                   dgq_ref, dgk_ref, mc_ref, ms1_ref, ms2_ref, dc_ref, ds1_ref, ds2_ref,
                   na_ref, mla_ref, diff_ref, ssd_ref):
    x = h_ref[0]
    shift = mod_ref[0, 0, 0:1, :]
    scale = mod_ref[0, 0, 1:2, :]
    a = _rms(x, gmix_ref[...]) * (1.0 + scale) + shift
    p = _dot(a.astype(BF16), w_ref[...])

    bd64 = bd64_ref[...]
    q = p[:, P_NA:P_NA + 256]
    k = p[:, P_NA + 256:P_NA + 512]
    na_ref[0, :, 0:256] = (_seg_rms(q, bd64, 1.0 / NA_HD, nagq_ref[...]) * (NA_HD ** -0.5)).astype(BF16)
    na_ref[0, :, 256:512] = _seg_rms(k, bd64, 1.0 / NA_HD, nagk_ref[...]).astype(BF16)
    na_ref[0, :, 512:768] = p[:, P_NA + 512:P_NA + 768].astype(BF16)

    bd128 = bd128_ref[...]
    cq = p[:, P_MLA:P_MLA + 256]
    ckv = p[:, P_MLA + 256:P_MLA + 384]
    kr = p[:, P_MLA + 384:P_MLA + 512]
    q2 = _dot(_rms(cq, gqa_ref[...]).astype(BF16), wqb_ref[...])
    kv = _dot(_rms(ckv, gkva_ref[...]).astype(BF16), wkvb_ref[...])
    k2 = kv[:, 0:512] + jnp.concatenate([kr] * HEADS, axis=-1)
    mc, ms1, ms2 = mc_ref[...], ms1_ref[...], ms2_ref[...]
    qn = _rope(_seg_rms(q2, bd128, 1.0 / MLA_QK, mgq_ref[...]), mc, ms1, ms2, 512)
    kn = _rope(_seg_rms(k2, bd128, 1.0 / MLA_QK, mgk_ref[...]), mc, ms1, ms2, 512)
    mla_ref[0, :, 0:512] = (qn * (MLA_QK ** -0.5)).astype(BF16)
    mla_ref[0, :, 512:1024] = kn.astype(BF16)
    mla_ref[0, :, 1024:1280] = kv[:, 512:768].astype(BF16)

    bd32 = bd32_ref[...]
    dc, ds1, ds2 = dc_ref[...], ds1_ref[...], ds2_ref[...]
    dq = p[:, P_DIFF:P_DIFF + 256]
    dk = p[:, P_DIFF + 256:P_DIFF + 512]
    dqn = _rope(_seg_rms(dq, bd32, 1.0 / DIFF_QK, dgq_ref[...]), dc, ds1, ds2, 256)
    dkn = _rope(_seg_rms(dk, bd32, 1.0 / DIFF_QK, dgk_ref[...]), dc, ds1, ds2, 256)
    diff_ref[0, :, 0:256] = (dqn * (DIFF_QK ** -0.5)).astype(BF16)
    diff_ref[0, :, 256:512] = dkn.astype(BF16)
    diff_ref[0, :, 512:768] = p[:, P_DIFF + 512:P_DIFF + 768].astype(BF16)

    ssd_ref[0] = p[:, P_SSD:P_SSD + SSD_W]


def _const_spec(shape):
    nd = len(shape)
    return pl.BlockSpec(shape, lambda b, i: (0,) * nd)


def _inproj(h, modsel, gmix, w, consts, lp, tabs):
    bsz, t, _ = h.shape
    nt = t // TM
    row = lambda w_: pl.BlockSpec((1, TM, w_), lambda b, i: (b, i, 0))
    tab = lambda w_: pl.BlockSpec((TM, w_), lambda b, i: (i, 0))
    in_specs = [
        row(D),
        pl.BlockSpec((1, 1, 6, D), lambda b, i: (b, jnp.minimum(i, 1), 0, 0)),
        _const_spec((1, D)), _const_spec((D, P_W)),
        _const_spec((256, 256)), _const_spec((512, 512)), _const_spec((256, 256)),
        _const_spec((1, 256)), _const_spec((1, 256)),
        _const_spec((1, 256)), _const_spec((256, 512)), _const_spec((1, 128)), _const_spec((128, 768)),
        _const_spec((1, 512)), _const_spec((1, 512)),
        _const_spec((1, 256)), _const_spec((1, 256)),
        tab(512), tab(512), tab(512), tab(256), tab(256), tab(256),
    ]
    out_shape = [jax.ShapeDtypeStruct((bsz, t, 768), BF16),
                 jax.ShapeDtypeStruct((bsz, t, 1280), BF16),
                 jax.ShapeDtypeStruct((bsz, t, 768), BF16),
                 jax.ShapeDtypeStruct((bsz, t, SSD_W), F32)]
    out_specs = [row(768), row(1280), row(768), row(SSD_W)]
    return pl.pallas_call(
        _inproj_kernel, grid=(bsz, nt), in_specs=in_specs, out_specs=out_specs, out_shape=out_shape,
        compiler_params=_cparams(2), name="inproj",
    )(h, modsel, gmix, w, consts["bd64"], consts["bd128"], consts["bd32"],
      lp["na_gq"], lp["na_gk"], lp["mla_gqa"], lp["mla_wqb"], lp["mla_gkva"], lp["mla_wkvb"],
      lp["mla_gq"], lp["mla_gk"], lp["diff_gq"], lp["diff_gk"],
      tabs["mc"], tabs["ms1"], tabs["ms2"], tabs["dc"], tabs["ds1"], tabs["ds2"])


def _na_kernel(q_ref, k_ref, v_ref, bias_ref, o_ref, *, n_win):
    i = pl.program_id(1)

    def ctx_part(h):
        sl = slice(h * NA_HD, (h + 1) * NA_HD)
        s_c = _dot_nt(q_ref[0, :, sl], k_ref[0, 0:N_CTX, sl])
        return s_c, v_ref[0, 0:N_CTX, sl]

    @pl.when(i == 0)
    def _():
        outs = []
        for h in range(HEADS):
            s_c, v_c = ctx_part(h)
            m = jnp.max(s_c, axis=-1, keepdims=True)
            e = jnp.exp(s_c - m)
            l = jnp.sum(e, axis=-1, keepdims=True)
            outs.append(_dot(e.astype(BF16), v_c) / l)
        o_ref[0] = jnp.concatenate(outs, axis=-1).astype(BF16)

    @pl.when(i > 0)
    def _():
        r0 = (i - 1) * NA_R
        s0 = jnp.clip(r0 - NA_KH // 2, 0, n_win)
        start = pl.multiple_of(N_CTX + s0 * GRID_W, GRID_W)
        win = pl.ds(start, NA_W * GRID_W)
        outs = []
        for h in range(HEADS):
            sl = slice(h * NA_HD, (h + 1) * NA_HD)
            s_c, v_c = ctx_part(h)
            s_w = _dot_nt(q_ref[0, :, sl], k_ref[0, win, sl]) + bias_ref[h, 0]
            m = jnp.maximum(jnp.max(s_c, axis=-1, keepdims=True), jnp.max(s_w, axis=-1, keepdims=True))
            e_c = jnp.exp(s_c - m)
            e_w = jnp.exp(s_w - m)
            l = jnp.sum(e_c, axis=-1, keepdims=True) + jnp.sum(e_w, axis=-1, keepdims=True)
            o = _dot(e_c.astype(BF16), v_c) + _dot(e_w.astype(BF16), v_ref[0, win, sl])
            outs.append(o / l)
        o_ref[0] = jnp.concatenate(outs, axis=-1).astype(BF16)


def _na_attention(na_qkv, bias):
    bsz, t, _ = na_qkv.shape
    nt = t // TM
    rows = (t - N_CTX) // GRID_W
    kern = functools.partial(_na_kernel, n_win=rows - NA_W)
    return pl.pallas_call(
        kern, grid=(bsz, nt),
        in_specs=[pl.BlockSpec((1, TM, 256), lambda b, i: (b, i, 0)),
                  pl.BlockSpec((1, t, 256), lambda b, i: (b, 0, 1)),
                  pl.BlockSpec((1, t, 256), lambda b, i: (b, 0, 2)),
                  pl.BlockSpec((HEADS, 1, TM, NA_W * GRID_W), lambda b, i: (0, jnp.maximum(i - 1, 0), 0, 0))],
        out_specs=pl.BlockSpec((1, TM, 256), lambda b, i: (b, i, 0)),
        out_shape=jax.ShapeDtypeStruct((bsz, t, 256), BF16),
        compiler_params=_cparams(2), name="na_attention",
    )(na_qkv, na_qkv, na_qkv, bias)


def _na_bias(rel_bias, rows):
    nb = rows // NA_R
    r0 = np.arange(nb) * NA_R
    s0 = np.clip(r0 - NA_KH // 2, 0, rows - NA_W)
    q_row = r0[:, None] + np.arange(NA_R)[None, :]
    k_row = s0[:, None] + np.arange(NA_W)[None, :]
    rs = np.clip(q_row - NA_KH // 2, 0, rows - NA_KH)
    row_ok = (k_row[:, None, :] >= rs[:, :, None]) & (k_row[:, None, :] < rs[:, :, None] + NA_KH)
    row_off = np.clip(k_row[:, None, :] - q_row[:, :, None] + (NA_KH - 1), 0, 2 * NA_KH - 2)
    cq = np.arange(GRID_W)
    col_lo = np.clip(cq - NA_KW // 2, 0, GRID_W - NA_KW)
    col_ok = (cq[None, :] >= col_lo[:, None]) & (cq[None, :] < col_lo[:, None] + NA_KW)
    col_off = np.clip(cq[None, :] - cq[:, None], 1 - NA_KW, NA_KW - 1) + (NA_KW - 1)
    b = rel_bias[:, row_off[:, :, None, :, None], col_off[None, None, :, None, :]]
    ok = row_ok[:, :, None, :, None] & col_ok[None, None, :, None, :]
    b = jnp.where(ok[None], b, -1e30)
    return b.reshape(HEADS, nb, NA_R * GRID_W, NA_W * GRID_W)


def _mla_kernel(q_ref, k_ref, v_ref, o_ref):
    i = pl.program_id(1)

    def attend(nk):
        outs = []
        for h in range(HEADS):
            s = _dot_nt(q_ref[0, :, h * MLA_QK_PAD:(h + 1) * MLA_QK_PAD],
                        k_ref[0, 0:nk, h * MLA_QK_PAD:(h + 1) * MLA_QK_PAD])
            m = jnp.max(s, axis=-1, keepdims=True)
            e = jnp.exp(s - m)
            l = jnp.sum(e, axis=-1, keepdims=True)
            outs.append(_dot(e.astype(BF16), v_ref[0, 0:nk, h * MLA_V:(h + 1) * MLA_V]) / l)
        o_ref[0] = jnp.concatenate(outs, axis=-1).astype(BF16)

    @pl.when(i == 0)
    def _():
        attend(N_CTX)

    @pl.when(i > 0)
    def _():
        attend(k_ref.shape[1])


def _mla_attention(mla_qkv):
    bsz, t, _ = mla_qkv.shape
    return pl.pallas_call(
        _mla_kernel, grid=(bsz, t // TM),
        in_specs=[pl.BlockSpec((1, TM, 512), lambda b, i: (b, i, 0)),
                  pl.BlockSpec((1, t, 512), lambda b, i: (b, 0, 1)),
                  pl.BlockSpec((1, t, 256), lambda b, i: (b, 0, 4))],
        out_specs=pl.BlockSpec((1, TM, 256), lambda b, i: (b, i, 0)),
        out_shape=jax.ShapeDtypeStruct((bsz, t, 256), BF16),
        compiler_params=_cparams(2), name="mla_attention",
    )(mla_qkv, mla_qkv, mla_qkv)


def _diff_kernel(q_ref, k_ref, v_ref, lam_ref, gsub_ref, o_ref, *, lam_init):
    i = pl.program_id(1)
    lv = lam_ref[...]
    lam = (jnp.exp(jnp.sum(lv[0:1] * lv[1:2], axis=-1, keepdims=True))
           - jnp.exp(jnp.sum(lv[2:3] * lv[3:4], axis=-1, keepdims=True)) + lam_init)

    def attend(nk):
        first = lax.broadcasted_iota(jnp.int32, (TM, DIFF_V), 1) < DIFF_QK
        outs = []
        for h in range(HEADS):
            sl = slice(h * DIFF_V, (h + 1) * DIFF_V)
            qh = q_ref[0, :, sl]
            kh = k_ref[0, 0:nk, sl]
            zero = jnp.zeros_like(qh)
            s1 = _dot_nt(jnp.where(first, qh, zero), kh)
            s2 = _dot_nt(jnp.where(first, zero, qh), kh)
            e1 = jnp.exp(s1 - jnp.max(s1, axis=-1, keepdims=True))
            e2 = jnp.exp(s2 - jnp.max(s2, axis=-1, keepdims=True))
            p = (e1 / jnp.sum(e1, axis=-1, keepdims=True)
                 - lam * (e2 / jnp.sum(e2, axis=-1, keepdims=True)))
            o = _dot(p.astype(BF16), v_ref[0, 0:nk, sl])
            outs.append(_rms(o, gsub_ref[...]) * (1.0 - lam_init))
        o_ref[0] = jnp.concatenate(outs, axis=-1).astype(BF16)

    @pl.when(i == 0)
    def _():
        attend(N_CTX)

    @pl.when(i > 0)
    def _():
        attend(k_ref.shape[1])


def _diff_attention(diff_qkv, lam_vecs, g_sub, lam_init):
    bsz, t, _ = diff_qkv.shape
    return pl.pallas_call(
        functools.partial(_diff_kernel, lam_init=lam_init), grid=(bsz, t // TM),
        in_specs=[pl.BlockSpec((1, TM, 256), lambda b, i: (b, i, 0)),
                  pl.BlockSpec((1, t, 256), lambda b, i: (b, 0, 1)),
                  pl.BlockSpec((1, t, 256), lambda b, i: (b, 0, 2)),
                  _const_spec((4, DIFF_QK)), _const_spec((1, DIFF_V))],
        out_specs=pl.BlockSpec((1, TM, 256), lambda b, i: (b, i, 0)),
        out_shape=jax.ShapeDtypeStruct((bsz, t, 256), BF16),
        compiler_params=_cparams(2), name="diff_attention",
    )(diff_qkv, diff_qkv, diff_qkv, lam_vecs, g_sub)


def _softplus(x):
    return jnp.maximum(x, 0.0) + jnp.log1p(jnp.exp(-jnp.abs(x)))


def _ssd_kernel(raw_ref, convw_ref, convb_ref, dtb_ref, alog_ref, dskip_ref, gnorm_ref, tril_ref, triu_ref,
                o_ref, xact_ref, yacc_ref, state_ref):
    t = raw_ref.shape[1]
    nt = t // TM
    xbc0 = SSD_INNER
    dt0 = SSD_INNER + SSD_XBC

    cw = convw_ref[...]
    cb = convb_ref[...]
    for j in range(nt):
        lo = j * TM
        cur = raw_ref[0, lo:lo + TM, xbc0:xbc0 + SSD_XBC]
        zeros8 = jnp.zeros((8, SSD_XBC), F32)
        prev = raw_ref[0, lo - 8:lo, xbc0:xbc0 + SSD_XBC] if j >= 2 else zeros8
        nxt = raw_ref[0, lo + TM:lo + TM + 8, xbc0:xbc0 + SSD_XBC] if 1 <= j < nt - 1 else zeros8
        u = jnp.concatenate([prev, cur, nxt], axis=0)
        acc = cb
        for kk in range(SSD_CONV):
            off = 8 - SSD_CONV // 2 + kk
            acc = acc + cw[kk:kk + 1, :] * u[off:off + TM, :]
        xact_ref[lo:lo + TM, :] = _silu(acc)

    a_pad = -jnp.exp(alog_ref[...])
    for d in range(2):
        tri = tril_ref[...] if d == 0 else triu_ref[...]
        rr = lax.broadcasted_iota(jnp.int32, (TM, TM), 0)
        cc = lax.broadcasted_iota(jnp.int32, (TM, TM), 1)
        mask = (rr >= cc) if d == 0 else (cc >= rr)
        state_ref[...] = jnp.zeros_like(state_ref)

        def chunk(c, carry, d=d, tri=tri, mask=mask):
            if d == 0:
                blk = c
            else:
                blk = jnp.where(c == 0, 0, nt - c)
            off = pl.multiple_of(blk * TM, TM)
            rows = pl.ds(off, TM)
            dt = _softplus(raw_ref[0, rows, dt0:dt0 + 128] + dtb_ref[...])
            la = dt * a_pad
            cum = _split_dot_left(tri, la)
            cum_t = cum.T
            total = cum[TM - 1:TM, :] if d == 0 else cum[0:1, :]
            e_in = jnp.exp(cum)
            e_out = jnp.exp(total - cum)
            e_tot = jnp.exp(total)
            xs = xact_ref[rows, 0:SSD_INNER]
            ys = []
            for g in range(SSD_GROUPS):
                bm = xact_ref[rows, SSD_INNER + g * SSD_STATE:SSD_INNER + (g + 1) * SSD_STATE]
                cm = xact_ref[rows, SSD_INNER + (SSD_GROUPS + g) * SSD_STATE:
                              SSD_INNER + (SSD_GROUPS + g + 1) * SSD_STATE]
                gm = _dot_nt(cm.astype(BF16), bm.astype(BF16))
                for hh in range(HEADS // SSD_GROUPS):
                    h = g * (HEADS // SSD_GROUPS) + hh
                    j = d * HEADS + h
                    x_h = xs[:, h * SSD_HD:(h + 1) * SSD_HD]
                    xdt = (x_h * dt[:, j:j + 1]).astype(BF16)
                    seg = cum[:, j:j + 1] - cum_t[j:j + 1, :]
                    dec = jnp.where(mask, jnp.exp(jnp.where(mask, seg, 0.0)), 0.0)
                    y_d = _dot((gm * dec).astype(BF16), xdt)
                    st = state_ref[h]
                    y_o = _dot((cm * e_in[:, j:j + 1]).astype(BF16), st.astype(BF16))
                    new = _dot_tn((bm * e_out[:, j:j + 1]).astype(BF16), xdt)
                    state_ref[h] = st * e_tot[:, j:j + 1] + new
                    ys.append(y_d + y_o + dskip_ref[d, h] * x_h)
            y = jnp.concatenate(ys, axis=-1)
            if d == 0:
                yacc_ref[rows, :] = y
            else:
                yacc_ref[rows, :] += y
            return carry

        lax.fori_loop(0, nt, chunk, 0)

    for j in range(nt):
        lo = j * TM
        y = yacc_ref[lo:lo + TM, :] * _silu(raw_ref[0, lo:lo + TM, 0:SSD_INNER])
        o_ref[0, lo:lo + TM, :] = _rms(y, gnorm_ref[...]).astype(BF16)


def _split_dot_left(m, x):
    hi = x.astype(BF16)
    r1 = x - hi.astype(F32)
    mid = r1.astype(BF16)
    lo = (r1 - mid.astype(F32)).astype(BF16)
    return _dot(m, hi) + _dot(m, mid) + _dot(m, lo)


def _ssd(ssd_raw, lp, consts):
    bsz, t, _ = ssd_raw.shape
    c1 = lambda shape: pl.BlockSpec(shape, lambda b: (0,) * len(shape))
    return pl.pallas_call(
        _ssd_kernel, grid=(bsz,),
        in_specs=[pl.BlockSpec((1, t, SSD_W), lambda b: (b, 0, 0)),
                  c1((SSD_CONV, SSD_XBC)), c1((1, SSD_XBC)), c1((1, 128)), c1((1, 128)),
                  pl.BlockSpec(memory_space=pltpu.SMEM),
                  c1((1, SSD_INNER)), c1((TM, TM)), c1((TM, TM))],
        out_specs=pl.BlockSpec((1, t, SSD_INNER), lambda b: (b, 0, 0)),
        out_shape=jax.ShapeDtypeStruct((bsz, t, SSD_INNER), BF16),
        scratch_shapes=[pltpu.VMEM((t, SSD_XBC), F32), pltpu.VMEM((t, SSD_INNER), F32),
                        pltpu.VMEM((HEADS, SSD_STATE, SSD_HD), F32)],
        compiler_params=_cparams(1), name="ssd",
    )(ssd_raw, lp["ssd_convw"], lp["ssd_convb"], lp["ssd_dtb"], lp["ssd_alog"], lp["ssd_dskip"],
      lp["ssd_gnorm"], consts["tril"], consts["triu"])


def _outproj_kernel(na_ref, mla_ref, diff_ref, ssd_ref, h_ref, mod_ref, wout_ref, gffn_ref, wr_ref, br_ref,
                    lstrict_ref, hout_ref, f_ref, route_ref, cnt_ref, run_ref):
    first = (pl.program_id(0) == 0) & (pl.program_id(1) == 0)

    @pl.when(first)
    def _():
        run_ref[...] = jnp.zeros_like(run_ref)

    o = (_dot(na_ref[0], wout_ref[0:256, :]) + _dot(mla_ref[0], wout_ref[256:512, :])
         + _dot(diff_ref[0], wout_ref[512:768, :]) + _dot(ssd_ref[0], wout_ref[768:1024, :]))
    gate = mod_ref[0, 0, 2:3, :]
    hn = h_ref[0] + gate * o
    hout_ref[0] = hn
    f = _rms(hn, gffn_ref[...]) * (1.0 + mod_ref[0, 0, 4:5, :]) + mod_ref[0, 0, 3:4, :]
    f_ref[0] = f

    logits = jnp.dot(f, wr_ref[...], preferred_element_type=F32, precision=lax.Precision.HIGHEST) + br_ref[...]
    lane = lax.broadcasted_iota(jnp.int32, logits.shape, 1)
    lane_f = lane.astype(F32)
    neg = jnp.float32(-jnp.inf)
    big = jnp.float32(1e9)
    gl = jnp.where(lane < MOE_GROUPS, logits, neg)
    gmax = jnp.max(gl, axis=-1, keepdims=True)
    g_top_p = 1.0 / jnp.sum(jnp.exp(gl - gmax), axis=-1, keepdims=True)
    g_top = jnp.min(jnp.where(gl == gmax, lane_f, big), axis=-1, keepdims=True).astype(jnp.int32)
    in_group = (lane >= MOE_GROUPS) & (lane < MOE_GROUPS + MOE_EXPERTS) & (((lane - MOE_GROUPS) // MOE_EPG) == g_top)
    el = jnp.where(in_group, logits, neg)
    m1 = jnp.max(el, axis=-1, keepdims=True)
    i1 = jnp.min(jnp.where(el == m1, lane_f, big), axis=-1, keepdims=True)
    el2 = jnp.where(lane_f == i1, neg, el)
    m2 = jnp.max(el2, axis=-1, keepdims=True)
    i2 = jnp.min(jnp.where(el2 == m2, lane_f, big), axis=-1, keepdims=True)
    x2 = jnp.exp(m2 - m1)
    w1 = g_top_p / (1.0 + x2)
    w2 = g_top_p * x2 / (1.0 + x2)
    e1 = i1 - MOE_GROUPS
    e2 = i2 - MOE_GROUPS

    onehot = ((lane_f == e1) | (lane_f == e2)).astype(F32)
    before = _dot(lstrict_ref[...], onehot.astype(BF16)) + run_ref[...]
    r1 = jnp.sum(jnp.where(lane_f == e1, before, 0.0), axis=-1, keepdims=True)
    r2 = jnp.sum(jnp.where(lane_f == e2, before, 0.0), axis=-1, keepdims=True)
    run_new = run_ref[...] + jnp.sum(onehot, axis=0, keepdims=True)
    run_ref[...] = run_new
    cnt_ref[...] = run_new
    route = jnp.zeros(logits.shape, F32)
    for idx, val in enumerate((e1, e2, r1, r2, w1, w2)):
        route = jnp.where(lane == idx, val, route)
    route_ref[0] = route


def _outproj(mix, h, modsel, wout, gffn, wr, br, consts):
    bsz, t, _ = h.shape
    row = lambda w_: pl.BlockSpec((1, TM, w_), lambda b, i: (b, i, 0))
    return pl.pallas_call(
        _outproj_kernel, grid=(bsz, t // TM),
        in_specs=[row(256), row(256), row(256), row(256), row(D),
                  pl.BlockSpec((1, 1, 6, D), lambda b, i: (b, jnp.minimum(i, 1), 0, 0)),
                  _const_spec((D, D)), _const_spec((1, D)), _const_spec((D, 128)), _const_spec((1, 128)),
                  _const_spec((TM, TM))],
        out_specs=[row(D), row(D), row(128), _const_spec((1, 128))],
        out_shape=[jax.ShapeDtypeStruct((bsz, t, D), F32), jax.ShapeDtypeStruct((bsz, t, D), F32),
                   jax.ShapeDtypeStruct((bsz, t, 128), F32), jax.ShapeDtypeStruct((1, 128), F32)],
        scratch_shapes=[pltpu.VMEM((1, 128), F32)],
        compiler_params=_cparams(2), name="outproj_router",
    )(*mix, h, modsel, wout, gffn, wr, br, consts["lstrict"])


def _dispatch_kernel(dest_ref, f_ref, xs_in_ref, xs_ref, sem):
    del xs_in_ref
    base = pl.program_id(0) * TM

    def row_copy(r, k):
        d = dest_ref[(base + r) * 2 + k]
        return pltpu.make_async_copy(f_ref.at[pl.ds(r, 1)], xs_ref.at[pl.ds(d, 1)], sem)

    def issue(r, c):
        row_copy(r, 0).start()
        row_copy(r, 1).start()
        return c

    def drain(r, c):
        row_copy(r, 0).wait()
        row_copy(r, 1).wait()
        return c

    lax.fori_loop(0, TM, issue, 0)
    lax.fori_loop(0, TM, drain, 0)


def _dispatch(dest, f2d, cap):
    n_tok = f2d.shape[0]
    zeros = jnp.zeros((cap, D), F32)
    return pl.pallas_call(
        _dispatch_kernel,
        grid_spec=pltpu.PrefetchScalarGridSpec(
            num_scalar_prefetch=1, grid=(n_tok // TM,),
            in_specs=[pl.BlockSpec((TM, D), lambda i, dest: (i, 0)),
                      pl.BlockSpec(memory_space=pl.ANY)],
            out_specs=pl.BlockSpec(memory_space=pl.ANY),
            scratch_shapes=[pltpu.SemaphoreType.DMA(())]),
        out_shape=jax.ShapeDtypeStruct((cap, D), F32),
        input_output_aliases={2: 0},
        compiler_params=_cparams(1), name="moe_dispatch",
    )(dest, f2d, zeros)


def _experts_kernel(be_ref, nb_ref, x_ref, wgu_ref, wd_ref, y_ref):
    @pl.when(pl.program_id(0) < nb_ref[0])
    def _():
        x = x_ref[...].astype(BF16)
        gu = _dot(x, wgu_ref[0])
        a = _silu(gu[:, 0:MOE_FF]) * gu[:, MOE_FF:2 * MOE_FF]
        y_ref[...] = _dot(a.astype(BF16), wd_ref[0])

    @pl.when(pl.program_id(0) >= nb_ref[0])
    def _():
        y_ref[...] = jnp.zeros_like(y_ref)


def _experts(block_e, n_used, xs, wgu, wd):
    cap = xs.shape[0]
    return pl.pallas_call(
        _experts_kernel,
        grid_spec=pltpu.PrefetchScalarGridSpec(
            num_scalar_prefetch=2, grid=(cap // MOE_MB,),
            in_specs=[pl.BlockSpec((MOE_MB, D), lambda i, be, nb: (i, 0)),
                      pl.BlockSpec((1, D, 2 * MOE_FF), lambda i, be, nb: (be[i], 0, 0)),
                      pl.BlockSpec((1, MOE_FF, D), lambda i, be, nb: (be[i], 0, 0))],
            out_specs=pl.BlockSpec((MOE_MB, D), lambda i, be, nb: (i, 0))),
        out_shape=jax.ShapeDtypeStruct((cap, D), F32),
        compiler_params=_cparams(1), name="moe_experts",
    )(block_e, n_used, xs, wgu, wd)


def _combine_kernel(dest_ref, h_ref, mod_ref, route_ref, y_ref, o_ref, buf_ref, sem):
    base = (pl.program_id(0) * pl.num_programs(1) + pl.program_id(1)) * TM

    def row_copy(r, k):
        d = dest_ref[(base + r) * 2 + k]
        return pltpu.make_async_copy(y_ref.at[pl.ds(d, 1)], buf_ref.at[k, pl.ds(r, 1)], sem)

    def issue(r, c):
        row_copy(r, 0).start()
        row_copy(r, 1).start()
        return c

    def drain(r, c):
        row_copy(r, 0).wait()
        row_copy(r, 1).wait()
        return c

    lax.fori_loop(0, TM, issue, 0)
    lax.fori_loop(0, TM, drain, 0)
    w1 = route_ref[0, :, 4:5]
    w2 = route_ref[0, :, 5:6]
    o_ref[0] = h_ref[0] + mod_ref[0, 0, 5:6, :] * (w1 * buf_ref[0] + w2 * buf_ref[1])


def _combine(dest, h, modsel, route, y):
    bsz, t, _ = h.shape
    row = lambda w_: pl.BlockSpec((1, TM, w_), lambda b, i, dest: (b, i, 0))
    return pl.pallas_call(
        _combine_kernel,
        grid_spec=pltpu.PrefetchScalarGridSpec(
            num_scalar_prefetch=1, grid=(bsz, t // TM),
            in_specs=[row(D),
                      pl.BlockSpec((1, 1, 6, D), lambda b, i, dest: (b, jnp.minimum(i, 1), 0, 0)),
                      row(128),
                      pl.BlockSpec(memory_space=pl.ANY)],
            out_specs=row(D),
            scratch_shapes=[pltpu.VMEM((2, TM, D), F32), pltpu.SemaphoreType.DMA(())]),
        out_shape=jax.ShapeDtypeStruct((bsz, t, D), F32),
        compiler_params=_cparams(2), name="moe_combine",
    )(dest, h, modsel, route, y)


def _moe_plan(route, counts, n_blocks):
    cnt = counts[0, :MOE_EXPERTS].astype(jnp.int32)
    padded = (cnt + MOE_MB - 1) // MOE_MB * MOE_MB
    pad_end = jnp.cumsum(padded)
    pad_start = pad_end - padded
    e = route[..., 0:2].astype(jnp.int32)
    r = route[..., 2:4].astype(jnp.int32)
    onehot = e[..., None] == jnp.arange(MOE_EXPERTS, dtype=jnp.int32)
    dest = jnp.sum(jnp.where(onehot, pad_start, 0), axis=-1) + r
    blk0 = jnp.arange(n_blocks, dtype=jnp.int32) * MOE_MB
    block_e = jnp.minimum(jnp.sum(blk0[:, None] >= pad_end[None, :], axis=-1), MOE_EXPERTS - 1)
    n_used = (pad_end[-1:] // MOE_MB).astype(jnp.int32)
    return dest.reshape(-1), block_e.astype(jnp.int32), n_used


def _block_diag(n, seg):
    idx = np.arange(n) // seg
    return jnp.asarray(idx[:, None] == idx[None, :], BF16)


def _rope_tables(n_lat, t):
    n_freq = 8
    inv = jnp.power(10000.0, -jnp.arange(n_freq, dtype=F32) / n_freq)
    tok = jnp.arange(n_lat, dtype=jnp.int32)
    row = (tok // GRID_W).astype(F32)
    col = (tok % GRID_W).astype(F32)
    ang = jnp.concatenate([row[:, None] * inv, col[:, None] * inv], axis=-1)
    n_c = t - n_lat
    cos = jnp.concatenate([jnp.ones((n_c, 16), F32), jnp.cos(ang)], axis=0)
    sin = jnp.concatenate([jnp.zeros((n_c, 16), F32), jnp.sin(ang)], axis=0)
    z16 = jnp.zeros((t, 16), F32)
    one = lambda w_: jnp.ones((t, w_), F32)
    zero = lambda w_: jnp.zeros((t, w_), F32)
    mc = jnp.concatenate([one(64), cos, cos, one(32)], axis=-1)
    ms1 = jnp.concatenate([zero(64), -sin, z16, zero(32)], axis=-1)
    ms2 = jnp.concatenate([zero(64), z16, sin, zero(32)], axis=-1)
    dc = jnp.concatenate([cos, cos], axis=-1)
    ds1 = jnp.concatenate([-sin, z16], axis=-1)
    ds2 = jnp.concatenate([z16, sin], axis=-1)
    tile = lambda a, n: jnp.tile(a, (1, n))
    return {"mc": tile(mc, 4), "ms1": tile(ms1, 4), "ms2": tile(ms2, 4),
            "dc": tile(dc, 8), "ds1": tile(ds1, 8), "ds2": tile(ds2, 8)}


def _pack_w_in(w_in):
    n_layers = w_in.shape[0]
    z = lambda n: jnp.zeros((n_layers, D, n), w_in.dtype)
    na = w_in[:, :, 0:768]
    mla = w_in[:, :, 768:1184]
    diff = w_in[:, :, 1184:1952]
    ssd = w_in[:, :, 1952:2984]
    mla_p = jnp.concatenate([mla[:, :, 0:384], z(64), mla[:, :, 384:416], z(32)], axis=-1)
    ssd_p = jnp.concatenate([ssd, z(SSD_W - 1032)], axis=-1)
    return jnp.concatenate([na, mla_p, diff, ssd_p], axis=-1).astype(BF16)


def _layer_params(l, p):
    row = lambda a: a.reshape(1, -1)
    t4 = lambda a: jnp.tile(a.reshape(1, -1), (1, HEADS))
    wqb = p["mla_w_qb"][l].reshape(MLA_Q_RANK, HEADS, MLA_QK)
    wqb = jnp.pad(wqb, ((0, 0), (0, 0), (0, MLA_QK_PAD - MLA_QK))).reshape(MLA_Q_RANK, HEADS * MLA_QK_PAD)
    wkvb = p["mla_w_kvb"][l].reshape(MLA_KV_RANK, HEADS, MLA_NOPE + MLA_V)
    wk = jnp.pad(wkvb[:, :, :MLA_NOPE], ((0, 0), (0, 0), (0, MLA_QK_PAD - MLA_NOPE)))
    wkvb = jnp.concatenate([wk.reshape(MLA_KV_RANK, -1), wkvb[:, :, MLA_NOPE:].reshape(MLA_KV_RANK, -1)], axis=-1)
    gpad = lambda g: jnp.tile(jnp.pad(g, (0, MLA_QK_PAD - MLA_QK)).reshape(1, -1), (1, HEADS))
    lane8 = lambda a: jnp.pad(a.reshape(1, -1), ((0, 0), (0, 128 - 2 * HEADS)))
    wr = jnp.concatenate([p["moe_w_group"][l], p["moe_w_expert"][l],
                          jnp.zeros((D, 128 - MOE_GROUPS - MOE_EXPERTS), F32)], axis=-1)
    br = jnp.concatenate([p["moe_b_group"][l], p["moe_b_expert"][l],
                          jnp.zeros((128 - MOE_GROUPS - MOE_EXPERTS,), F32)]).reshape(1, 128)
    return {
        "g_mix": row(p["g_mix"][l]), "g_ffn": row(p["g_ffn"][l]),
        "na_gq": t4(p["na_g_q"][l]), "na_gk": t4(p["na_g_k"][l]),
        "mla_gqa": row(p["mla_g_qa"][l]), "mla_wqb": wqb.astype(BF16),
        "mla_gkva": row(p["mla_g_kva"][l]), "mla_wkvb": wkvb.astype(BF16),
        "mla_gq": gpad(p["mla_g_q"][l]), "mla_gk": gpad(p["mla_g_k"][l]),
        "diff_gq": jnp.tile(p["diff_g_q"][l].reshape(1, -1), (1, 8)),
        "diff_gk": jnp.tile(p["diff_g_k"][l].reshape(1, -1), (1, 8)),
        "diff_lam": p["diff_lambda"][l], "diff_gsub": row(p["diff_g_sub"][l]),
        "ssd_convw": p["ssd_conv_w"][l], "ssd_convb": row(p["ssd_conv_b"][l]),
        "ssd_dtb": lane8(p["ssd_dt_bias"][l]), "ssd_alog": lane8(p["ssd_a_log"][l]),
        "ssd_dskip": p["ssd_d"][l], "ssd_gnorm": row(p["ssd_g_norm"][l]),
        "wr": wr, "br": br,
    }


def _mixers(h, modsel, w_in_l, lp, consts, tabs, bias, lam_init):
    na_qkv, mla_qkv, diff_qkv, ssd_raw = _inproj(h, modsel, lp["g_mix"], w_in_l, consts, lp, tabs)
    return (_na_attention(na_qkv, bias),
            _mla_attention(mla_qkv),
            _diff_attention(diff_qkv, lp["diff_lam"], lp["diff_gsub"], lam_init),
            _ssd(ssd_raw, lp, consts))


def _moe(hn, f, route, counts, modsel, wgu_l, wd_l):
    bsz, t, _ = hn.shape
    n_asg = bsz * t * 2
    n_blocks = -(-n_asg // MOE_MB) + MOE_EXPERTS
    dest, block_e, n_used = _moe_plan(route, counts, n_blocks)
    xs = _dispatch(dest, f.reshape(bsz * t, D), n_blocks * MOE_MB)
    y = _experts(block_e, n_used, xs, wgu_l, wd_l)
    return _combine(dest, hn, modsel, route, y)


def kernel(x, c, ctx, c_ctx, w_mod, b_mod, g_mix, w_in, w_out, na_g_q, na_g_k, na_rel_bias,
           mla_g_qa, mla_w_qb, mla_g_kva, mla_w_kvb, mla_g_q, mla_g_k,
           diff_g_q, diff_g_k, diff_lambda, diff_g_sub,
           ssd_conv_w, ssd_conv_b, ssd_dt_bias, ssd_a_log, ssd_d, ssd_g_norm,
           g_ffn, moe_w_group, moe_b_group, moe_w_expert, moe_b_expert, moe_w_gate, moe_w_up, moe_w_down):
    p = dict(g_mix=g_mix, g_ffn=g_ffn, na_g_q=na_g_q, na_g_k=na_g_k,
             mla_g_qa=mla_g_qa, mla_w_qb=mla_w_qb, mla_g_kva=mla_g_kva, mla_w_kvb=mla_w_kvb,
             mla_g_q=mla_g_q, mla_g_k=mla_g_k, diff_g_q=diff_g_q, diff_g_k=diff_g_k,
             diff_lambda=diff_lambda, diff_g_sub=diff_g_sub,
             ssd_conv_w=ssd_conv_w, ssd_conv_b=ssd_conv_b, ssd_dt_bias=ssd_dt_bias, ssd_a_log=ssd_a_log,
             ssd_d=ssd_d, ssd_g_norm=ssd_g_norm, moe_w_group=moe_w_group, moe_b_group=moe_b_group,
             moe_w_expert=moe_w_expert, moe_b_expert=moe_b_expert)
    bsz, n_lat, _ = x.shape
    n_ctx = ctx.shape[1]
    assert n_ctx == N_CTX == TM and n_lat % TM == 0 and bsz < 16
    t = n_ctx + n_lat
    n_layers = w_mod.shape[0]
    rows = n_lat // GRID_W

    consts = {"bd64": _block_diag(256, 64), "bd128": _block_diag(512, 128), "bd32": _block_diag(256, 32),
              "tril": jnp.asarray(np.tril(np.ones((TM, TM))), BF16),
              "triu": jnp.asarray(np.triu(np.ones((TM, TM))), BF16),
              "lstrict": jnp.asarray(np.tril(np.ones((TM, TM)), -1), BF16)}
    tabs = _rope_tables(n_lat, t)
    w_in_p = _pack_w_in(w_in)
    w_out_b = w_out.astype(BF16)
    wgu = jnp.concatenate([moe_w_gate, moe_w_up], axis=-1).astype(BF16)
    wd = moe_w_down.astype(BF16)

    cvec = jnp.concatenate([c, c_ctx[None, :], jnp.zeros((16 - bsz - 1, D), F32)], axis=0)
    mod = _modulation(cvec, w_mod, b_mod).reshape(n_layers, 16, 6, D)

    h = jnp.concatenate([ctx, x], axis=1)
    for l in range(n_layers):
        lp = _layer_params(l, p)
        modsel = jnp.stack([jnp.broadcast_to(mod[l, bsz][None], (bsz, 6, D)), mod[l, :bsz]], axis=1)
        lam_init = 0.8 - 0.6 * math.exp(-0.3 * l)
        bias = _na_bias(na_rel_bias[l], rows)
        mix = _mixers(h, modsel, w_in_p[l], lp, consts, tabs, bias, lam_init)
        hn, f, route, counts = _outproj(mix, h, modsel, w_out_b[l], lp["g_ffn"], lp["wr"], lp["br"], consts)
        h = _moe(hn, f, route, counts, modsel, wgu[l], wd[l])
    return h[:, n_ctx:, :]
```

```python
import functools
import math

import numpy as np
import jax
import jax.numpy as jnp
from jax import lax
from jax.experimental import pallas as pl
from jax.experimental.pallas import tpu as pltpu

F32 = jnp.float32
BF16 = jnp.bfloat16

D = 1024
GRID_W = 64
N_CTX = 256
HEADS = 4
NA_HD = 64
NA_KH = 8
NA_KW = 16
MLA_NOPE = 64
MLA_ROPE = 32
MLA_QK = MLA_NOPE + MLA_ROPE
MLA_QK_PAD = 128
MLA_V = 64
MLA_Q_RANK = 256
MLA_KV_RANK = 128
DIFF_QK = 32
DIFF_V = 64
SSD_INNER = 256
SSD_HD = 64
SSD_STATE = 128
SSD_GROUPS = 2
SSD_CONV = 5
SSD_XBC = 768
MOE_GROUPS = 4
MOE_EPG = 4
MOE_EXPERTS = 16
MOE_FF = 512
EPS = 1e-6

TM = 256
MOE_MB = 256
NA_R = 4
NA_W = 12
V_HD = 64
V_AUG = 128
CK = 256
LOG2E = math.log2(math.e)
DMA_UNROLL = 8

P_NA = 0
P_MLA = 1024
P_DIFF = 1536
P_SSD = 2560
P_W = 3712
SSD_W = 1152

VMEM_LIMIT = 56 * 1024 * 1024


def _cparams(n_axes):
    return pltpu.CompilerParams(dimension_semantics=("arbitrary",) * n_axes,
                                vmem_limit_bytes=VMEM_LIMIT)


def _dot(a, b):
    return jnp.dot(a, b, preferred_element_type=F32)


def _dot_nt(a, b):
    return lax.dot_general(a, b, (((1,), (1,)), ((), ())), preferred_element_type=F32)


def _dot_tn(a, b):
    return lax.dot_general(a, b, (((0,), (0,)), ((), ())), preferred_element_type=F32)


def _split3(x):
    hi = x.astype(BF16)
    r1 = x - hi.astype(F32)
    mid = r1.astype(BF16)
    lo = (r1 - mid.astype(F32)).astype(BF16)
    return hi, mid, lo


def _split_dot(x, m):
    hi, mid, lo = _split3(x)
    return _dot(hi, m) + _dot(mid, m) + _dot(lo, m)


def _split_dot_left(m, x):
    hi, mid, lo = _split3(x)
    return _dot(m, hi) + _dot(m, mid) + _dot(m, lo)


def _rms(x, g):
    ms = jnp.mean(x * x, axis=-1, keepdims=True)
    return x * lax.rsqrt(ms + EPS) * g


def _seg_rms(x, bd, inv_n, g):
    ms = _split_dot(x * x, bd) * inv_n
    return x * lax.rsqrt(ms + EPS) * g


def _silu(x):
    return x * jax.nn.sigmoid(x)


def _rope(x, c, s1, s2, width):
    rot = 16
    return x * c + pltpu.roll(x, width - rot, 1) * s1 + pltpu.roll(x, rot, 1) * s2


def _with_ones(v):
    lane = lax.broadcasted_iota(jnp.int32, v.shape, 1)
    return jnp.where(lane % V_AUG == V_HD, 1.0, v)


def _mod_kernel(c_ref, w_ref, b_ref, o_ref):
    s = _silu(c_ref[...])
    o_ref[0] = _dot(s.astype(BF16), w_ref[0].astype(BF16)) + b_ref[0]


def _modulation(cvec, w_mod, b_mod):
    n_layers = w_mod.shape[0]
    tn = 1536
    return pl.pallas_call(
        _mod_kernel,
        grid=(n_layers, 6 * D // tn),
        in_specs=[pl.BlockSpec((16, D), lambda l, j: (0, 0)),
                  pl.BlockSpec((1, D, tn), lambda l, j: (l, 0, j)),
                  pl.BlockSpec((1, 1, tn), lambda l, j: (l, 0, j))],
        out_specs=pl.BlockSpec((1, 16, tn), lambda l, j: (l, 0, j)),
        out_shape=jax.ShapeDtypeStruct((n_layers, 16, 6 * D), F32),
        compiler_params=_cparams(2),
        name="modulation",
    )(cvec, w_mod, b_mod.reshape(n_layers, 1, 6 * D))


def _inproj_kernel(h_ref, mod_ref, gmix_ref, w_ref, bd64_ref, bd128_ref, bd32_ref,
                   nagq_ref, nagk_ref, gqa_ref, wqb_ref, gkva_ref, wkvb_ref, mgq_ref, mgk_ref,
                   dgq_ref, dgk_ref, mc_ref, ms1_ref, ms2_ref, dc_ref, ds1_ref, ds2_ref,
                   na_ref, mla_ref, diff_ref, ssd_ref):
    x = h_ref[0]
    shift = mod_ref[0, 0, 0:1, :]
    scale = mod_ref[0, 0, 1:2, :]
    a = _rms(x, gmix_ref[...]) * (1.0 + scale) + shift
    p = _dot(a.astype(BF16), w_ref[...])


    bd64 = bd64_ref[...]
    q = p[:, P_NA:P_NA + 256]
    k = p[:, P_NA + 256:P_NA + 512]
    na_ref[0, :, 0:256] = (_seg_rms(q, bd64, 1.0 / NA_HD, nagq_ref[...]) * (NA_HD ** -0.5 * LOG2E)).astype(BF16)
    na_ref[0, :, 256:512] = _seg_rms(k, bd64, 1.0 / NA_HD, nagk_ref[...]).astype(BF16)
    na_ref[0, :, 512:1024] = _with_ones(p[:, P_NA + 512:P_NA + 1024]).astype(BF16)

    bd128 = bd128_ref[...]
    cq = p[:, P_MLA:P_MLA + 256]
    ckv = p[:, P_MLA + 256:P_MLA + 384]
    kr = p[:, P_MLA + 384:P_MLA + 512]
    q2 = _dot(_rms(cq, gqa_ref[...]).astype(BF16), wqb_ref[...])
    kv = _dot(_rms(ckv, gkva_ref[...]).astype(BF16), wkvb_ref[...])
    k2 = kv[:, 0:512] + jnp.concatenate([kr] * HEADS, axis=-1)
    mc, ms1, ms2 = mc_ref[...], ms1_ref[...], ms2_ref[...]
    qn = _rope(_seg_rms(q2, bd128, 1.0 / MLA_QK, mgq_ref[...]), mc, ms1, ms2, 512)
    kn = _rope(_seg_rms(k2, bd128, 1.0 / MLA_QK, mgk_ref[...]), mc, ms1, ms2, 512)
    mla_ref[0, :, 0:512] = (qn * (MLA_QK ** -0.5 * LOG2E)).astype(BF16)
    mla_ref[0, :, 512:1024] = kn.astype(BF16)
    mla_ref[0, :, 1024:1536] = _with_ones(kv[:, 512:1024]).astype(BF16)

    bd32 = bd32_ref[...]
    dc, ds1, ds2 = dc_ref[...], ds1_ref[...], ds2_ref[...]
    dq = p[:, P_DIFF:P_DIFF + 256]
    dk = p[:, P_DIFF + 256:P_DIFF + 512]
    dqn = _rope(_seg_rms(dq, bd32, 1.0 / DIFF_QK, dgq_ref[...]), dc, ds1, ds2, 256)
    dkn = _rope(_seg_rms(dk, bd32, 1.0 / DIFF_QK, dgk_ref[...]), dc, ds1, ds2, 256)
    diff_ref[0, :, 0:256] = (dqn * (DIFF_QK ** -0.5 * LOG2E)).astype(BF16)
    diff_ref[0, :, 256:512] = dkn.astype(BF16)
    diff_ref[0, :, 512:1024] = _with_ones(p[:, P_DIFF + 512:P_DIFF + 1024]).astype(BF16)

    ssd_ref[0] = p[:, P_SSD:P_SSD + SSD_W]


def _const_spec(shape):
    nd = len(shape)
    return pl.BlockSpec(shape, lambda b, i: (0,) * nd)


def _inproj(h, modsel, gmix, w, consts, lp, tabs):
    bsz, t, _ = h.shape
    nt = t // TM
    row = lambda w_: pl.BlockSpec((1, TM, w_), lambda b, i: (b, i, 0))
    tab = lambda w_: pl.BlockSpec((TM, w_), lambda b, i: (i, 0))
    in_specs = [
        row(D),
        pl.BlockSpec((1, 1, 6, D), lambda b, i: (b, jnp.minimum(i, 1), 0, 0)),
        _const_spec((1, D)), _const_spec((D, P_W)),
        _const_spec((256, 256)), _const_spec((512, 512)), _const_spec((256, 256)),
        _const_spec((1, 256)), _const_spec((1, 256)),
        _const_spec((1, 256)), _const_spec((256, 512)), _const_spec((1, 128)), _const_spec((128, 1024)),
        _const_spec((1, 512)), _const_spec((1, 512)),
        _const_spec((1, 256)), _const_spec((1, 256)),
        tab(512), tab(512), tab(512), tab(256), tab(256), tab(256),
    ]
    out_shape = [jax.ShapeDtypeStruct((bsz, t, 1024), BF16),
                 jax.ShapeDtypeStruct((bsz, t, 1536), BF16),
                 jax.ShapeDtypeStruct((bsz, t, 1024), BF16),
                 jax.ShapeDtypeStruct((bsz, t, SSD_W), F32)]
    out_specs = [row(1024), row(1536), row(1024), row(SSD_W)]
    return pl.pallas_call(
        _inproj_kernel, grid=(bsz, nt), in_specs=in_specs, out_specs=out_specs, out_shape=out_shape,
        compiler_params=_cparams(2), name="inproj",
    )(h, modsel, gmix, w, consts["bd64"], consts["bd128"], consts["bd32"],
      lp["na_gq"], lp["na_gk"], lp["mla_gqa"], lp["mla_wqb"], lp["mla_gkva"], lp["mla_wkvb"],
      lp["mla_gq"], lp["mla_gk"], lp["diff_gq"], lp["diff_gk"],
      tabs["mc"], tabs["ms1"], tabs["ms2"], tabs["dc"], tabs["ds1"], tabs["ds2"])


def _attend(qh, chunks, s_ref):
    m_run = None
    for c, (k_fn, _, b_fn) in enumerate(chunks):
        s = _dot_nt(qh, k_fn())
        if b_fn is not None:
            s = s + b_fn()
        s_ref[:, c * CK:(c + 1) * CK] = s
        for j in range(CK // 128):
            blk = s[:, j * 128:(j + 1) * 128]
            m_run = blk if m_run is None else jnp.maximum(m_run, blk)
    m = jnp.broadcast_to(jnp.max(m_run, axis=-1, keepdims=True), (TM, 128))
    acc = None
    for c, (_, v_fn, _) in enumerate(chunks):
        e = jnp.concatenate(
            [jnp.exp2(s_ref[:, c * CK + j * 128:c * CK + (j + 1) * 128] - m) for j in range(CK // 128)], axis=-1)
        pv = _dot(e.astype(BF16), v_fn())
        acc = pv if acc is None else acc + pv
    return acc[:, 0:V_HD] / acc[:, V_HD:V_HD + 1]


def _kv_chunks(k_ref, v_ref, k_sl, v_sl, n_chunks, first=0, start=None, bias_fn=None):
    out = []
    for c in range(n_chunks):
        if start is None:
            rows = slice((first + c) * CK, (first + c + 1) * CK)
        else:
            rows = pl.ds(start + c * CK, CK)
        out.append((lambda rows=rows: k_ref[0, rows, k_sl],
                    lambda rows=rows: v_ref[0, rows, v_sl],
                    None if bias_fn is None else functools.partial(bias_fn, c)))
    return out


def _na_kernel(q_ref, k_ref, v_ref, bias_ref, o_ref, s_ref, *, n_win):
    i = pl.program_id(1)

    def run(window_start):
        outs = []
        for h in range(HEADS):
            k_sl = slice(h * NA_HD, (h + 1) * NA_HD)
            v_sl = slice(h * V_AUG, (h + 1) * V_AUG)
            chunks = _kv_chunks(k_ref, v_ref, k_sl, v_sl, 1)
            if window_start is not None:
                bias_fn = lambda c, h=h: bias_ref[h, 0, :, c * CK:(c + 1) * CK]
                chunks += _kv_chunks(k_ref, v_ref, k_sl, v_sl, NA_W * GRID_W // CK, start=window_start,
                                     bias_fn=bias_fn)
            outs.append(_attend(q_ref[0, :, k_sl], chunks, s_ref.at[h % 2]))
        o_ref[0] = jnp.concatenate(outs, axis=-1).astype(BF16)

    @pl.when(i == 0)
    def _():
        run(None)

    @pl.when(i > 0)
    def _():
        s0 = jnp.clip((i - 1) * NA_R - NA_KH // 2, 0, n_win)
        run(pl.multiple_of(N_CTX + s0 * GRID_W, GRID_W))


def _na_attention(na_qkv, bias):
    bsz, t, _ = na_qkv.shape
    rows = (t - N_CTX) // GRID_W
    kern = functools.partial(_na_kernel, n_win=rows - NA_W)
    return pl.pallas_call(
        kern, grid=(bsz, t // TM),
        in_specs=[pl.BlockSpec((1, TM, 256), lambda b, i: (b, i, 0)),
                  pl.BlockSpec((1, t, 256), lambda b, i: (b, 0, 1)),
                  pl.BlockSpec((1, t, 512), lambda b, i: (b, 0, 1)),
                  pl.BlockSpec((HEADS, 1, TM, NA_W * GRID_W), lambda b, i: (0, jnp.maximum(i - 1, 0), 0, 0))],
        out_specs=pl.BlockSpec((1, TM, 256), lambda b, i: (b, i, 0)),
        out_shape=jax.ShapeDtypeStruct((bsz, t, 256), BF16),
        scratch_shapes=[pltpu.VMEM((2, TM, CK + NA_W * GRID_W), F32)],
        compiler_params=_cparams(2), name="na_attention",
    )(na_qkv, na_qkv, na_qkv, bias)


def _na_bias(rel_bias, rows):
    nb = rows // NA_R
    r0 = np.arange(nb) * NA_R
    s0 = np.clip(r0 - NA_KH // 2, 0, rows - NA_W)
    q_row = r0[:, None] + np.arange(NA_R)[None, :]
    k_row = s0[:, None] + np.arange(NA_W)[None, :]
    rs = np.clip(q_row - NA_KH // 2, 0, rows - NA_KH)
    row_ok = (k_row[:, None, :] >= rs[:, :, None]) & (k_row[:, None, :] < rs[:, :, None] + NA_KH)
    row_off = np.clip(k_row[:, None, :] - q_row[:, :, None] + (NA_KH - 1), 0, 2 * NA_KH - 2)
    cq = np.arange(GRID_W)
    col_lo = np.clip(cq - NA_KW // 2, 0, GRID_W - NA_KW)
    col_ok = (cq[None, :] >= col_lo[:, None]) & (cq[None, :] < col_lo[:, None] + NA_KW)
    col_off = np.clip(cq[None, :] - cq[:, None], 1 - NA_KW, NA_KW - 1) + (NA_KW - 1)
    row_sel = jnp.asarray(row_off[..., None] == np.arange(2 * NA_KH - 1), F32)
    col_sel = jnp.asarray(col_off[..., None] == np.arange(2 * NA_KW - 1), F32)
    t1 = jnp.einsum("hab,qkb->haqk", rel_bias * LOG2E, col_sel, precision=lax.Precision.HIGHEST)
    b = jnp.einsum("nrwa,haqk->hnrqwk", row_sel, t1, precision=lax.Precision.HIGHEST)
    ok = row_ok[:, :, None, :, None] & col_ok[None, None, :, None, :]
    b = jnp.where(ok[None], b, -1e30)
    return b.reshape(HEADS, nb, NA_R * GRID_W, NA_W * GRID_W)


def _mla_kernel(q_ref, k_ref, v_ref, o_ref, s_ref):
    i = pl.program_id(1)

    def run(n_chunks):
        outs = []
        for h in range(HEADS):
            sl = slice(h * MLA_QK_PAD, (h + 1) * MLA_QK_PAD)
            chunks = _kv_chunks(k_ref, v_ref, sl, slice(h * V_AUG, (h + 1) * V_AUG), n_chunks)
            outs.append(_attend(q_ref[0, :, sl], chunks, s_ref.at[h % 2]))
        o_ref[0] = jnp.concatenate(outs, axis=-1).astype(BF16)

    @pl.when(i == 0)
    def _():
        run(1)

    @pl.when(i > 0)
    def _():
        run(k_ref.shape[1] // CK)


def _mla_attention(mla_qkv):
    bsz, t, _ = mla_qkv.shape
    return pl.pallas_call(
        _mla_kernel, grid=(bsz, t // TM),
        in_specs=[pl.BlockSpec((1, TM, 512), lambda b, i: (b, i, 0)),
                  pl.BlockSpec((1, t, 512), lambda b, i: (b, 0, 1)),
                  pl.BlockSpec((1, t, 512), lambda b, i: (b, 0, 2))],
        out_specs=pl.BlockSpec((1, TM, 256), lambda b, i: (b, i, 0)),
        out_shape=jax.ShapeDtypeStruct((bsz, t, 256), BF16),
        scratch_shapes=[pltpu.VMEM((2, TM, t), F32)],
        compiler_params=_cparams(2), name="mla_attention",
    )(mla_qkv, mla_qkv, mla_qkv)


def _diff_kernel(q_ref, k_ref, v_ref, lam_ref, gsub_ref, o_ref, s_ref, *, lam_init):
    i = pl.program_id(1)
    lv = lam_ref[...]
    lam = (jnp.exp(jnp.sum(lv[0:1] * lv[1:2], axis=-1, keepdims=True))
           - jnp.exp(jnp.sum(lv[2:3] * lv[3:4], axis=-1, keepdims=True)) + lam_init)

    def run(n_chunks):
        first = lax.broadcasted_iota(jnp.int32, (TM, 2 * DIFF_QK), 1) < DIFF_QK
        outs = []
        for h in range(HEADS):
            sl = slice(h * 2 * DIFF_QK, (h + 1) * 2 * DIFF_QK)
            chunks = _kv_chunks(k_ref, v_ref, sl, slice(h * V_AUG, (h + 1) * V_AUG), n_chunks)
            qh = q_ref[0, :, sl]
            zero = jnp.zeros_like(qh)
            o1 = _attend(jnp.where(first, qh, zero), chunks, s_ref.at[0])
            o2 = _attend(jnp.where(first, zero, qh), chunks, s_ref.at[1])
            outs.append(_rms(o1 - lam * o2, gsub_ref[...]) * (1.0 - lam_init))
        o_ref[0] = jnp.concatenate(outs, axis=-1).astype(BF16)

    @pl.when(i == 0)
    def _():
        run(1)

    @pl.when(i > 0)
    def _():
        run(k_ref.shape[1] // CK)


def _diff_attention(diff_qkv, lam_vecs, g_sub, lam_init):
    bsz, t, _ = diff_qkv.shape
    return pl.pallas_call(
        functools.partial(_diff_kernel, lam_init=lam_init), grid=(bsz, t // TM),
        in_specs=[pl.BlockSpec((1, TM, 256), lambda b, i: (b, i, 0)),
                  pl.BlockSpec((1, t, 256), lambda b, i: (b, 0, 1)),
                  pl.BlockSpec((1, t, 512), lambda b, i: (b, 0, 1)),
                  _const_spec((4, DIFF_QK)), _const_spec((1, DIFF_V))],
        out_specs=pl.BlockSpec((1, TM, 256), lambda b, i: (b, i, 0)),
        out_shape=jax.ShapeDtypeStruct((bsz, t, 256), BF16),
        scratch_shapes=[pltpu.VMEM((2, TM, t), F32)],
        compiler_params=_cparams(2), name="diff_attention",
    )(diff_qkv, diff_qkv, diff_qkv, lam_vecs, g_sub)


def _softplus(x):
    return jnp.maximum(x, 0.0) + jnp.log1p(jnp.exp(-jnp.abs(x)))


def _ssd_kernel(raw_ref, convw_ref, convb_ref, dtb_ref, alog_ref, dskip_ref, gnorm_ref,
                tril_ref, triu_ref, mlow_ref, mupp_ref, o_ref, xact_ref, yacc_ref, state_ref):
    t = raw_ref.shape[1]
    nt = t // TM
    xbc0 = SSD_INNER
    dt0 = SSD_INNER + SSD_XBC

    cw = convw_ref[...]
    cb = convb_ref[...]
    for j in range(nt):
        lo = j * TM
        cur = raw_ref[0, lo:lo + TM, xbc0:xbc0 + SSD_XBC]
        zeros8 = jnp.zeros((8, SSD_XBC), F32)
        prev = raw_ref[0, lo - 8:lo, xbc0:xbc0 + SSD_XBC] if j >= 2 else zeros8
        nxt = raw_ref[0, lo + TM:lo + TM + 8, xbc0:xbc0 + SSD_XBC] if 1 <= j < nt - 1 else zeros8
        u = jnp.concatenate([prev, cur, nxt], axis=0)
        acc = cb
        for kk in range(SSD_CONV):
            off = 8 - SSD_CONV // 2 + kk
            acc = acc + cw[kk:kk + 1, :] * u[off:off + TM, :]
        xact_ref[lo:lo + TM, :] = _silu(acc)

    a_pad = -jnp.exp(alog_ref[...])
    for d in range(2):
        tri_ref = tril_ref if d == 0 else triu_ref
        off_ref = mlow_ref if d == 0 else mupp_ref
        state_ref[...] = jnp.zeros_like(state_ref)

        def chunk(c, carry, d=d, tri_ref=tri_ref, off_ref=off_ref):
            if d == 0:
                blk = c
            else:
                blk = jnp.where(c == 0, 0, nt - c)
            off = pl.multiple_of(blk * TM, TM)
            rows = pl.ds(off, TM)
            dt = _softplus(raw_ref[0, rows, dt0:dt0 + 128] + dtb_ref[...])
            la = dt * a_pad
            cum = _split_dot_left(tri_ref[...], la)
            cum_t = cum.T
            total = cum[TM - 1:TM, :] if d == 0 else cum[0:1, :]
            e_in = jnp.exp(cum)
            e_out = jnp.exp(total - cum)
            e_tot = jnp.exp(total)
            xs = xact_ref[rows, 0:SSD_INNER]
            ys = []
            for g in range(SSD_GROUPS):
                bm = xact_ref[rows, SSD_INNER + g * SSD_STATE:SSD_INNER + (g + 1) * SSD_STATE]
                cm = xact_ref[rows, SSD_INNER + (SSD_GROUPS + g) * SSD_STATE:
                              SSD_INNER + (SSD_GROUPS + g + 1) * SSD_STATE]
                gm = _dot_nt(cm.astype(BF16), bm.astype(BF16))
                for hh in range(HEADS // SSD_GROUPS):
                    h = g * (HEADS // SSD_GROUPS) + hh
                    j = d * HEADS + h
                    x_h = xs[:, h * SSD_HD:(h + 1) * SSD_HD]
                    xdt = (x_h * dt[:, j:j + 1]).astype(BF16)
                    dec = jnp.exp(cum[:, j:j + 1] - cum_t[j:j + 1, :] + off_ref[...])
                    y_d = _dot((gm * dec).astype(BF16), xdt)
                    st = state_ref[h]
                    y_o = _dot((cm * e_in[:, j:j + 1]).astype(BF16), st.astype(BF16))
                    new = _dot_tn((bm * e_out[:, j:j + 1]).astype(BF16), xdt)
                    state_ref[h] = st * e_tot[:, j:j + 1] + new
                    ys.append(y_d + y_o + dskip_ref[d, h] * x_h)
            y = jnp.concatenate(ys, axis=-1)
            if d == 0:
                yacc_ref[rows, :] = y
            else:
                yacc_ref[rows, :] += y
            return carry

        lax.fori_loop(0, nt, chunk, 0)

    for j in range(nt):
        lo = j * TM
        y = yacc_ref[lo:lo + TM, :] * _silu(raw_ref[0, lo:lo + TM, 0:SSD_INNER])
        o_ref[0, lo:lo + TM, :] = _rms(y, gnorm_ref[...]).astype(BF16)


def _ssd(ssd_raw, lp, consts):
    bsz, t, _ = ssd_raw.shape
    c1 = lambda shape: pl.BlockSpec(shape, lambda b: (0,) * len(shape))
    return pl.pallas_call(
        _ssd_kernel, grid=(bsz,),
        in_specs=[pl.BlockSpec((1, t, SSD_W), lambda b: (b, 0, 0)),
                  c1((SSD_CONV, SSD_XBC)), c1((1, SSD_XBC)), c1((1, 128)), c1((1, 128)),
                  pl.BlockSpec(memory_space=pltpu.SMEM),
                  c1((1, SSD_INNER)), c1((TM, TM)), c1((TM, TM)), c1((TM, TM)), c1((TM, TM))],
        out_specs=pl.BlockSpec((1, t, SSD_INNER), lambda b: (b, 0, 0)),
        out_shape=jax.ShapeDtypeStruct((bsz, t, SSD_INNER), BF16),
        scratch_shapes=[pltpu.VMEM((t, SSD_XBC), F32), pltpu.VMEM((t, SSD_INNER), F32),
                        pltpu.VMEM((HEADS, SSD_STATE, SSD_HD), F32)],
        compiler_params=_cparams(1), name="ssd",
    )(ssd_raw, lp["ssd_convw"], lp["ssd_convb"], lp["ssd_dtb"], lp["ssd_alog"], lp["ssd_dskip"],
      lp["ssd_gnorm"], consts["tril"], consts["triu"], consts["mlow"], consts["mupp"])


def _outproj_kernel(na_ref, mla_ref, diff_ref, ssd_ref, h_ref, mod_ref, wout_ref, gffn_ref,
                    wrh_ref, wrl_ref, br_ref, lstrict_ref, hout_ref, f_ref, route_ref, cnt_ref, run_ref):
    first = (pl.program_id(0) == 0) & (pl.program_id(1) == 0)

    @pl.when(first)
    def _():
        run_ref[...] = jnp.zeros_like(run_ref)

    o = (_dot(na_ref[0], wout_ref[0:256, :]) + _dot(mla_ref[0], wout_ref[256:512, :])
         + _dot(diff_ref[0], wout_ref[512:768, :]) + _dot(ssd_ref[0], wout_ref[768:1024, :]))
    gate = mod_ref[0, 0, 2:3, :]
    hn = h_ref[0] + gate * o
    hout_ref[0] = hn
    f = _rms(hn, gffn_ref[...]) * (1.0 + mod_ref[0, 0, 4:5, :]) + mod_ref[0, 0, 3:4, :]
    f_ref[0] = f

    f_hi = f.astype(BF16)
    f_lo = (f - f_hi.astype(F32)).astype(BF16)
    logits = _dot(f_hi, wrh_ref[...]) + _dot(f_lo, wrh_ref[...]) + _dot(f_hi, wrl_ref[...]) + br_ref[...]
    lane = lax.broadcasted_iota(jnp.int32, logits.shape, 1)
    lane_f = lane.astype(F32)
    neg = jnp.float32(-jnp.inf)
    big = jnp.float32(1e9)
    gl = jnp.where(lane < MOE_GROUPS, logits, neg)
    gmax = jnp.max(gl, axis=-1, keepdims=True)
    g_top_p = 1.0 / jnp.sum(jnp.exp(gl - gmax), axis=-1, keepdims=True)
    g_top = jnp.min(jnp.where(gl == gmax, lane_f, big), axis=-1, keepdims=True).astype(jnp.int32)
    in_group = (lane >= MOE_GROUPS) & (lane < MOE_GROUPS + MOE_EXPERTS) & (((lane - MOE_GROUPS) // MOE_EPG) == g_top)
    el = jnp.where(in_group, logits, neg)
    m1 = jnp.max(el, axis=-1, keepdims=True)
    i1 = jnp.min(jnp.where(el == m1, lane_f, big), axis=-1, keepdims=True)
    el2 = jnp.where(lane_f == i1, neg, el)
    m2 = jnp.max(el2, axis=-1, keepdims=True)
    i2 = jnp.min(jnp.where(el2 == m2, lane_f, big), axis=-1, keepdims=True)
    x2 = jnp.exp(m2 - m1)
    w1 = g_top_p / (1.0 + x2)
    w2 = g_top_p * x2 / (1.0 + x2)
    e1 = i1 - MOE_GROUPS
    e2 = i2 - MOE_GROUPS

    onehot = ((lane_f == e1) | (lane_f == e2)).astype(F32)
    before = _dot(lstrict_ref[...], onehot.astype(BF16)) + run_ref[...]
    r1 = jnp.sum(jnp.where(lane_f == e1, before, 0.0), axis=-1, keepdims=True)
    r2 = jnp.sum(jnp.where(lane_f == e2, before, 0.0), axis=-1, keepdims=True)
    run_new = run_ref[...] + jnp.sum(onehot, axis=0, keepdims=True)
    run_ref[...] = run_new
    cnt_ref[...] = run_new
    route = jnp.zeros(logits.shape, F32)
    for idx, val in enumerate((e1, e2, r1, r2, w1, w2)):
        route = jnp.where(lane == idx, val, route)
    route_ref[0] = route


def _outproj(mix, h, modsel, wout, gffn, lp, consts):
    bsz, t, _ = h.shape
    row = lambda w_: pl.BlockSpec((1, TM, w_), lambda b, i: (b, i, 0))
    return pl.pallas_call(
        _outproj_kernel, grid=(bsz, t // TM),
        in_specs=[row(256), row(256), row(256), row(256), row(D),
                  pl.BlockSpec((1, 1, 6, D), lambda b, i: (b, jnp.minimum(i, 1), 0, 0)),
                  _const_spec((D, D)), _const_spec((1, D)), _const_spec((D, 128)), _const_spec((D, 128)),
                  _const_spec((1, 128)), _const_spec((TM, TM))],
        out_specs=[row(D), row(D), row(128), _const_spec((1, 128))],
        out_shape=[jax.ShapeDtypeStruct((bsz, t, D), F32), jax.ShapeDtypeStruct((bsz, t, D), F32),
                   jax.ShapeDtypeStruct((bsz, t, 128), F32), jax.ShapeDtypeStruct((1, 128), F32)],
        scratch_shapes=[pltpu.VMEM((1, 128), F32)],
        compiler_params=_cparams(2), name="outproj_router",
    )(*mix, h, modsel, wout, gffn, lp["wr_hi"], lp["wr_lo"], lp["br"], consts["lstrict"])


def _row_dmas(make_copy, whole_copies):
    def issue(r8, c):
        for u in range(DMA_UNROLL):
            for k in range(2):
                make_copy(r8 * DMA_UNROLL + u, k).start(priority=k)
        return c

    lax.fori_loop(0, TM // DMA_UNROLL, issue, 0)
    for w in whole_copies:
        w.wait()


def _dispatch_kernel(dest_ref, f_ref, xs_in_ref, xs_ref, sem):
    del xs_in_ref
    base = pl.program_id(0) * TM

    def make_copy(r, k):
        d = dest_ref[(base + r) * 2 + k]
        return pltpu.make_async_copy(f_ref.at[pl.ds(r, 1)], xs_ref.at[pl.ds(d, 1)], sem)

    whole = pltpu.make_async_copy(f_ref, xs_ref.at[pl.ds(0, TM)], sem)
    _row_dmas(make_copy, [whole, whole])


def _dispatch(dest, f2d, cap):
    n_tok = f2d.shape[0]
    zeros = jnp.zeros((cap, D), F32)
    return pl.pallas_call(
        _dispatch_kernel,
        grid_spec=pltpu.PrefetchScalarGridSpec(
            num_scalar_prefetch=1, grid=(n_tok // TM,),
            in_specs=[pl.BlockSpec((TM, D), lambda i, dest: (i, 0)),
                      pl.BlockSpec(memory_space=pl.ANY)],
            out_specs=pl.BlockSpec(memory_space=pl.ANY),
            scratch_shapes=[pltpu.SemaphoreType.DMA(())]),
        out_shape=jax.ShapeDtypeStruct((cap, D), F32),
        input_output_aliases={2: 0},
        compiler_params=_cparams(1), name="moe_dispatch",
    )(dest, f2d, zeros)


def _experts_kernel(be_ref, nb_ref, x_ref, wgu_ref, wd_ref, y_ref):
    @pl.when(pl.program_id(0) < nb_ref[0])
    def _():
        x = x_ref[...].astype(BF16)
        gu = _dot(x, wgu_ref[0])
        a = _silu(gu[:, 0:MOE_FF]) * gu[:, MOE_FF:2 * MOE_FF]
        y_ref[...] = _dot(a.astype(BF16), wd_ref[0])

    @pl.when(pl.program_id(0) >= nb_ref[0])
    def _():
        y_ref[...] = jnp.zeros_like(y_ref)


def _experts(block_e, n_used, xs, wgu, wd):
    cap = xs.shape[0]
    return pl.pallas_call(
        _experts_kernel,
        grid_spec=pltpu.PrefetchScalarGridSpec(
            num_scalar_prefetch=2, grid=(cap // MOE_MB,),
            in_specs=[pl.BlockSpec((MOE_MB, D), lambda i, be, nb: (i, 0)),
                      pl.BlockSpec((1, D, 2 * MOE_FF), lambda i, be, nb: (be[i], 0, 0)),
                      pl.BlockSpec((1, MOE_FF, D), lambda i, be, nb: (be[i], 0, 0))],
            out_specs=pl.BlockSpec((MOE_MB, D), lambda i, be, nb: (i, 0))),
        out_shape=jax.ShapeDtypeStruct((cap, D), F32),
        compiler_params=_cparams(1), name="moe_experts",
    )(block_e, n_used, xs, wgu, wd)


def _combine_kernel(dest_ref, h_ref, mod_ref, route_ref, y_ref, o_ref, buf_ref, sem):
    base = (pl.program_id(0) * pl.num_programs(1) + pl.program_id(1)) * TM

    def make_copy(r, k):
        d = dest_ref[(base + r) * 2 + k]
        return pltpu.make_async_copy(y_ref.at[pl.ds(d, 1)], buf_ref.at[k, pl.ds(r, 1)], sem)

    whole = [pltpu.make_async_copy(y_ref.at[pl.ds(0, TM)], buf_ref.at[k], sem) for k in range(2)]
    _row_dmas(make_copy, whole)
    w1 = route_ref[0, :, 4:5]
    w2 = route_ref[0, :, 5:6]
    o_ref[0] = h_ref[0] + mod_ref[0, 0, 5:6, :] * (w1 * buf_ref[0] + w2 * buf_ref[1])


def _combine(dest, h, modsel, route, y):
    bsz, t, _ = h.shape
    row = lambda w_: pl.BlockSpec((1, TM, w_), lambda b, i, dest: (b, i, 0))
    return pl.pallas_call(
        _combine_kernel,
        grid_spec=pltpu.PrefetchScalarGridSpec(
            num_scalar_prefetch=1, grid=(bsz, t // TM),
            in_specs=[row(D),
                      pl.BlockSpec((1, 1, 6, D), lambda b, i, dest: (b, jnp.minimum(i, 1), 0, 0)),
                      row(128),
                      pl.BlockSpec(memory_space=pl.ANY)],
            out_specs=row(D),
            scratch_shapes=[pltpu.VMEM((2, TM, D), F32), pltpu.SemaphoreType.DMA(())]),
        out_shape=jax.ShapeDtypeStruct((bsz, t, D), F32),
        compiler_params=_cparams(2), name="moe_combine",
    )(dest, h, modsel, route, y)


def _moe_plan(route, counts, n_blocks):
    cnt = counts[0, :MOE_EXPERTS].astype(jnp.int32)
    padded = (cnt + MOE_MB - 1) // MOE_MB * MOE_MB
    pad_end = jnp.cumsum(padded)
    pad_start = pad_end - padded
    e = route[..., 0:2].astype(jnp.int32)
    r = route[..., 2:4].astype(jnp.int32)
    onehot = e[..., None] == jnp.arange(MOE_EXPERTS, dtype=jnp.int32)
    dest = jnp.sum(jnp.where(onehot, pad_start, 0), axis=-1) + r
    blk0 = jnp.arange(n_blocks, dtype=jnp.int32) * MOE_MB
    block_e = jnp.minimum(jnp.sum(blk0[:, None] >= pad_end[None, :], axis=-1), MOE_EXPERTS - 1)
    n_used = (pad_end[-1:] // MOE_MB).astype(jnp.int32)
    return dest.reshape(-1), block_e.astype(jnp.int32), n_used


def _block_diag(n, seg):
    idx = np.arange(n) // seg
    return jnp.asarray(idx[:, None] == idx[None, :], BF16)


def _constants():
    lower = np.tril(np.ones((TM, TM), np.float32))
    upper = np.triu(np.ones((TM, TM), np.float32))
    return {"bd64": _block_diag(256, 64), "bd128": _block_diag(512, 128), "bd32": _block_diag(256, 32),
            "tril": jnp.asarray(lower, BF16), "triu": jnp.asarray(upper, BF16),
            "mlow": jnp.asarray((lower - 1.0) * 1e30, F32), "mupp": jnp.asarray((upper - 1.0) * 1e30, F32),
            "lstrict": jnp.asarray(np.tril(np.ones((TM, TM)), -1), BF16)}


def _rope_tables(n_lat, t):
    n_freq = 8
    inv = jnp.power(10000.0, -jnp.arange(n_freq, dtype=F32) / n_freq)
    tok = jnp.arange(n_lat, dtype=jnp.int32)
    row = (tok // GRID_W).astype(F32)
    col = (tok % GRID_W).astype(F32)
    ang = jnp.concatenate([row[:, None] * inv, col[:, None] * inv], axis=-1)
    n_c = t - n_lat
    cos = jnp.concatenate([jnp.ones((n_c, 16), F32), jnp.cos(ang)], axis=0)
    sin = jnp.concatenate([jnp.zeros((n_c, 16), F32), jnp.sin(ang)], axis=0)
    z16 = jnp.zeros((t, 16), F32)
    one = lambda w_: jnp.ones((t, w_), F32)
    zero = lambda w_: jnp.zeros((t, w_), F32)
    mc = jnp.concatenate([one(64), cos, cos, one(32)], axis=-1)
    ms1 = jnp.concatenate([zero(64), -sin, z16, zero(32)], axis=-1)
    ms2 = jnp.concatenate([zero(64), z16, sin, zero(32)], axis=-1)
    dc = jnp.concatenate([cos, cos], axis=-1)
    ds1 = jnp.concatenate([-sin, z16], axis=-1)
    ds2 = jnp.concatenate([z16, sin], axis=-1)
    tile = lambda a, n: jnp.tile(a, (1, n))
    return {"mc": tile(mc, 4), "ms1": tile(ms1, 4), "ms2": tile(ms2, 4),
            "dc": tile(dc, 8), "ds1": tile(ds1, 8), "ds2": tile(ds2, 8)}


def _pad_heads(w, width, padded):
    lead = w.shape[:-1]
    w = w.reshape(lead + (HEADS, width))
    w = jnp.pad(w, [(0, 0)] * len(lead) + [(0, 0), (0, padded - width)])
    return w.reshape(lead + (HEADS * padded,))


def _pack_w_in(w_in):
    n_layers = w_in.shape[0]
    z = lambda n: jnp.zeros((n_layers, D, n), w_in.dtype)
    na = w_in[:, :, 0:768]
    mla = w_in[:, :, 768:1184]
    diff = w_in[:, :, 1184:1952]
    ssd = w_in[:, :, 1952:2984]
    na_p = jnp.concatenate([na[:, :, 0:512], _pad_heads(na[:, :, 512:768], V_HD, V_AUG)], axis=-1)
    mla_p = jnp.concatenate([mla[:, :, 0:384], z(64), mla[:, :, 384:416], z(32)], axis=-1)
    diff_p = jnp.concatenate([diff[:, :, 0:512], _pad_heads(diff[:, :, 512:768], V_HD, V_AUG)], axis=-1)
    ssd_p = jnp.concatenate([ssd, z(SSD_W - 1032)], axis=-1)
    return jnp.concatenate([na_p, mla_p, diff_p, ssd_p], axis=-1).astype(BF16)


def _layer_params(l, p):
    row = lambda a: a.reshape(1, -1)
    t4 = lambda a: jnp.tile(a.reshape(1, -1), (1, HEADS))
    wqb = _pad_heads(p["mla_w_qb"][l], MLA_QK, MLA_QK_PAD)
    wkvb = p["mla_w_kvb"][l].reshape(MLA_KV_RANK, HEADS, MLA_NOPE + MLA_V)
    wkvb = jnp.concatenate([_pad_heads(wkvb[:, :, :MLA_NOPE].reshape(MLA_KV_RANK, -1), MLA_NOPE, MLA_QK_PAD),
                            _pad_heads(wkvb[:, :, MLA_NOPE:].reshape(MLA_KV_RANK, -1), MLA_V, V_AUG)], axis=-1)
    gpad = lambda g: jnp.tile(jnp.pad(g, (0, MLA_QK_PAD - MLA_QK)).reshape(1, -1), (1, HEADS))
    lane8 = lambda a: jnp.pad(a.reshape(1, -1), ((0, 0), (0, 128 - 2 * HEADS)))
    wr = jnp.concatenate([p["moe_w_group"][l], p["moe_w_expert"][l],
                          jnp.zeros((D, 128 - MOE_GROUPS - MOE_EXPERTS), F32)], axis=-1)
    wr_hi = wr.astype(BF16)
    br = jnp.concatenate([p["moe_b_group"][l], p["moe_b_expert"][l],
                          jnp.zeros((128 - MOE_GROUPS - MOE_EXPERTS,), F32)]).reshape(1, 128)
    return {
        "g_mix": row(p["g_mix"][l]), "g_ffn": row(p["g_ffn"][l]),
        "na_gq": t4(p["na_g_q"][l]), "na_gk": t4(p["na_g_k"][l]),
        "mla_gqa": row(p["mla_g_qa"][l]), "mla_wqb": wqb.astype(BF16),
        "mla_gkva": row(p["mla_g_kva"][l]), "mla_wkvb": wkvb.astype(BF16),
        "mla_gq": gpad(p["mla_g_q"][l]), "mla_gk": gpad(p["mla_g_k"][l]),
        "diff_gq": jnp.tile(p["diff_g_q"][l].reshape(1, -1), (1, 8)),
        "diff_gk": jnp.tile(p["diff_g_k"][l].reshape(1, -1), (1, 8)),
        "diff_lam": p["diff_lambda"][l], "diff_gsub": row(p["diff_g_sub"][l]),
        "ssd_convw": p["ssd_conv_w"][l], "ssd_convb": row(p["ssd_conv_b"][l]),
        "ssd_dtb": lane8(p["ssd_dt_bias"][l]), "ssd_alog": lane8(p["ssd_a_log"][l]),
        "ssd_dskip": p["ssd_d"][l], "ssd_gnorm": row(p["ssd_g_norm"][l]),
        "wr_hi": wr_hi, "wr_lo": (wr - wr_hi.astype(F32)).astype(BF16), "br": br,
    }


def _mixers(h, modsel, w_in_l, lp, consts, tabs, bias, lam_init):
    na_qkv, mla_qkv, diff_qkv, ssd_raw = _inproj(h, modsel, lp["g_mix"], w_in_l, consts, lp, tabs)
    return (_na_attention(na_qkv, bias),
            _mla_attention(mla_qkv),
            _diff_attention(diff_qkv, lp["diff_lam"], lp["diff_gsub"], lam_init),
            _ssd(ssd_raw, lp, consts))


def _moe(hn, f, route, counts, modsel, wgu_l, wd_l):
    bsz, t, _ = hn.shape
    n_asg = bsz * t * 2
    n_blocks = -(-n_asg // MOE_MB) + MOE_EXPERTS
    dest, block_e, n_used = _moe_plan(route, counts, n_blocks)
    xs = _dispatch(dest, f.reshape(bsz * t, D), n_blocks * MOE_MB)
    y = _experts(block_e, n_used, xs, wgu_l, wd_l)
    return _combine(dest, hn, modsel, route, y)


def kernel(x, c, ctx, c_ctx, w_mod, b_mod, g_mix, w_in, w_out, na_g_q, na_g_k, na_rel_bias,
           mla_g_qa, mla_w_qb, mla_g_kva, mla_w_kvb, mla_g_q, mla_g_k,
           diff_g_q, diff_g_k, diff_lambda, diff_g_sub,
           ssd_conv_w, ssd_conv_b, ssd_dt_bias, ssd_a_log, ssd_d, ssd_g_norm,
           g_ffn, moe_w_group, moe_b_group, moe_w_expert, moe_b_expert, moe_w_gate, moe_w_up, moe_w_down):
    p = dict(g_mix=g_mix, g_ffn=g_ffn, na_g_q=na_g_q, na_g_k=na_g_k,
             mla_g_qa=mla_g_qa, mla_w_qb=mla_w_qb, mla_g_kva=mla_g_kva, mla_w_kvb=mla_w_kvb,
             mla_g_q=mla_g_q, mla_g_k=mla_g_k, diff_g_q=diff_g_q, diff_g_k=diff_g_k,
             diff_lambda=diff_lambda, diff_g_sub=diff_g_sub,
             ssd_conv_w=ssd_conv_w, ssd_conv_b=ssd_conv_b, ssd_dt_bias=ssd_dt_bias, ssd_a_log=ssd_a_log,
             ssd_d=ssd_d, ssd_g_norm=ssd_g_norm, moe_w_group=moe_w_group, moe_b_group=moe_b_group,
             moe_w_expert=moe_w_expert, moe_b_expert=moe_b_expert)
    bsz, n_lat, _ = x.shape
    n_ctx = ctx.shape[1]
    assert n_ctx == N_CTX == TM == CK and n_lat % TM == 0 and bsz < 16
    t = n_ctx + n_lat
    n_layers = w_mod.shape[0]
    rows = n_lat // GRID_W
    assert rows % NA_R == 0 and rows >= NA_W and (NA_W * GRID_W) % CK == 0

    consts = _constants()
    tabs = _rope_tables(n_lat, t)
    w_in_p = _pack_w_in(w_in)
    w_out_b = w_out.astype(BF16)
    wgu = jnp.concatenate([moe_w_gate, moe_w_up], axis=-1).astype(BF16)
    wd = moe_w_down.astype(BF16)

    cvec = jnp.concatenate([c, c_ctx[None, :], jnp.zeros((16 - bsz - 1, D), F32)], axis=0)
    mod = _modulation(cvec, w_mod, b_mod).reshape(n_layers, 16, 6, D)

    h = jnp.concatenate([ctx, x], axis=1)
    for l in range(n_layers):
        lp = _layer_params(l, p)
        modsel = jnp.stack([jnp.broadcast_to(mod[l, bsz][None], (bsz, 6, D)), mod[l, :bsz]], axis=1)
        lam_init = 0.8 - 0.6 * math.exp(-0.3 * l)
        bias = _na_bias(na_rel_bias[l], rows)
        mix = _mixers(h, modsel, w_in_p[l], lp, consts, tabs, bias, lam_init)
        hn, f, route, counts = _outproj(mix, h, modsel, w_out_b[l], lp["g_ffn"], lp, consts)
        h = _moe(hn, f, route, counts, modsel, wgu[l], wd[l])
    return h[:, n_ctx:, :]
```

```python
import functools
import math

import numpy as np
import jax
import jax.numpy as jnp
from jax import lax
from jax.experimental import pallas as pl
from jax.experimental.pallas import tpu as pltpu

F32 = jnp.float32
BF16 = jnp.bfloat16

D = 1024
GRID_W = 64
N_CTX = 256
HEADS = 4
NA_HD = 64
NA_KH = 8
NA_KW = 16
MLA_NOPE = 64
MLA_ROPE = 32
MLA_QK = MLA_NOPE + MLA_ROPE
MLA_QK_PAD = 128
MLA_V = 64
MLA_Q_RANK = 256
MLA_KV_RANK = 128
DIFF_QK = 32
DIFF_V = 64
SSD_INNER = 256
SSD_HD = 64
SSD_STATE = 128
SSD_GROUPS = 2
SSD_CONV = 5
SSD_XBC = 768
MOE_GROUPS = 4
MOE_EPG = 4
MOE_EXPERTS = 16
MOE_FF = 512
EPS = 1e-6

TM = 256
MOE_MB = 256
NA_R = 4
NA_W = 12
V_HD = 64
V_AUG = 128
CK = 256
LOG2E = math.log2(math.e)
DMA_UNROLL = 8

P_NA = 0
P_MLA = 1024
P_DIFF = 1536
P_SSD = 2560
P_W = 3712
SSD_W = 1152

VMEM_LIMIT = 56 * 1024 * 1024


def _cparams(n_axes):
    return pltpu.CompilerParams(dimension_semantics=("arbitrary",) * n_axes,
                                vmem_limit_bytes=VMEM_LIMIT)


def _dot(a, b):
    return jnp.dot(a, b, preferred_element_type=F32)


def _dot_nt(a, b):
    return lax.dot_general(a, b, (((1,), (1,)), ((), ())), preferred_element_type=F32)


def _dot_tn(a, b):
    return lax.dot_general(a, b, (((0,), (0,)), ((), ())), preferred_element_type=F32)


def _split3(x):
    hi = x.astype(BF16)
    r1 = x - hi.astype(F32)
    mid = r1.astype(BF16)
    lo = (r1 - mid.astype(F32)).astype(BF16)
    return hi, mid, lo


def _split_dot(x, m):
    hi, mid, lo = _split3(x)
    return _dot(hi, m) + _dot(mid, m) + _dot(lo, m)


def _split_dot_left(m, x):
    hi, mid, lo = _split3(x)
    return _dot(m, hi) + _dot(m, mid) + _dot(m, lo)


def _rms(x, g):
    ms = jnp.mean(x * x, axis=-1, keepdims=True)
    return x * lax.rsqrt(ms + EPS) * g


def _seg_rms(x, bd, inv_n, g):
    x2 = x * x
    hi = x2.astype(BF16)
    lo = (x2 - hi.astype(F32)).astype(BF16)
    ms = (_dot(hi, bd) + _dot(lo, bd)) * inv_n
    return x * lax.rsqrt(ms + EPS) * g


def _silu(x):
    return x * jax.nn.sigmoid(x)


def _rope(x, c, s1, s2, width):
    rot = 16
    return x * c + pltpu.roll(x, width - rot, 1) * s1 + pltpu.roll(x, rot, 1) * s2


def _with_ones(v):
    lane = lax.broadcasted_iota(jnp.int32, v.shape, 1)
    return jnp.where(lane % V_AUG == V_HD, 1.0, v)


def _mod_kernel(c_ref, w_ref, b_ref, o_ref):
    s = _silu(c_ref[...])
    o_ref[0] = _dot(s.astype(BF16), w_ref[0].astype(BF16)) + b_ref[0]


def _modulation(cvec, w_mod, b_mod):
    n_layers = w_mod.shape[0]
    tn = 1536
    return pl.pallas_call(
        _mod_kernel,
        grid=(n_layers, 6 * D // tn),
        in_specs=[pl.BlockSpec((16, D), lambda l, j: (0, 0)),
                  pl.BlockSpec((1, D, tn), lambda l, j: (l, 0, j)),
                  pl.BlockSpec((1, 1, tn), lambda l, j: (l, 0, j))],
        out_specs=pl.BlockSpec((1, 16, tn), lambda l, j: (l, 0, j)),
        out_shape=jax.ShapeDtypeStruct((n_layers, 16, 6 * D), F32),
        compiler_params=_cparams(2),
        name="modulation",
    )(cvec, w_mod, b_mod.reshape(n_layers, 1, 6 * D))


def _inproj_kernel(h_ref, mod_ref, gmix_ref, w_ref, bd64_ref, bd128_ref, bd32_ref,
                   nagq_ref, nagk_ref, gqa_ref, wqb_ref, gkva_ref, wkvb_ref, mgq_ref, mgk_ref,
                   dgq_ref, dgk_ref, mc_ref, ms1_ref, ms2_ref, dc_ref, ds1_ref, ds2_ref,
                   na_ref, mla_ref, mlavt_ref, diff_ref, diffvt_ref, ssd_ref):
    x = h_ref[0]
    shift = mod_ref[0, 0, 0:1, :]
    scale = mod_ref[0, 0, 1:2, :]
    a = _rms(x, gmix_ref[...]) * (1.0 + scale) + shift
    p = _dot(a.astype(BF16), w_ref[...])


    bd64 = bd64_ref[...]
    q = p[:, P_NA:P_NA + 256]
    k = p[:, P_NA + 256:P_NA + 512]
    na_ref[0, :, 0:256] = (_seg_rms(q, bd64, 1.0 / NA_HD, nagq_ref[...]) * (NA_HD ** -0.5 * LOG2E)).astype(BF16)
    na_ref[0, :, 256:512] = _seg_rms(k, bd64, 1.0 / NA_HD, nagk_ref[...]).astype(BF16)
    na_ref[0, :, 512:1024] = _with_ones(p[:, P_NA + 512:P_NA + 1024]).astype(BF16)

    bd128 = bd128_ref[...]
    cq = p[:, P_MLA:P_MLA + 256]
    ckv = p[:, P_MLA + 256:P_MLA + 384]
    kr = p[:, P_MLA + 384:P_MLA + 512]
    q2 = _dot(_rms(cq, gqa_ref[...]).astype(BF16), wqb_ref[...])
    kv = _dot(_rms(ckv, gkva_ref[...]).astype(BF16), wkvb_ref[...])
    k2 = kv[:, 0:512] + jnp.concatenate([kr] * HEADS, axis=-1)
    mc, ms1, ms2 = mc_ref[...], ms1_ref[...], ms2_ref[...]
    qn = _rope(_seg_rms(q2, bd128, 1.0 / MLA_QK, mgq_ref[...]), mc, ms1, ms2, 512)
    kn = _rope(_seg_rms(k2, bd128, 1.0 / MLA_QK, mgk_ref[...]), mc, ms1, ms2, 512)
    mla_ref[0, :, 0:512] = (qn * (MLA_QK ** -0.5 * LOG2E)).astype(BF16)
    mla_ref[0, :, 512:1024] = kn.astype(BF16)
    mlavt_ref[0] = _with_ones(kv[:, 512:1024]).T.astype(BF16)

    bd32 = bd32_ref[...]
    dc, ds1, ds2 = dc_ref[...], ds1_ref[...], ds2_ref[...]
    dq = p[:, P_DIFF:P_DIFF + 256]
    dk = p[:, P_DIFF + 256:P_DIFF + 512]
    dqn = _rope(_seg_rms(dq, bd32, 1.0 / DIFF_QK, dgq_ref[...]), dc, ds1, ds2, 256)
    dkn = _rope(_seg_rms(dk, bd32, 1.0 / DIFF_QK, dgk_ref[...]), dc, ds1, ds2, 256)
    diff_ref[0, :, 0:256] = (dqn * (DIFF_QK ** -0.5 * LOG2E)).astype(BF16)
    diff_ref[0, :, 256:512] = dkn.astype(BF16)
    diffvt_ref[0] = _with_ones(p[:, P_DIFF + 512:P_DIFF + 1024]).T.astype(BF16)

    ssd_ref[0] = p[:, P_SSD:P_SSD + SSD_W]


def _const_spec(shape):
    nd = len(shape)
    return pl.BlockSpec(shape, lambda b, i: (0,) * nd)


def _inproj(h, modsel, gmix, w, consts, lp, tabs):
    bsz, t, _ = h.shape
    nt = t // TM
    row = lambda w_: pl.BlockSpec((1, TM, w_), lambda b, i: (b, i, 0))
    tab = lambda w_: pl.BlockSpec((TM, w_), lambda b, i: (i, 0))
    in_specs = [
        row(D),
        pl.BlockSpec((1, 1, 6, D), lambda b, i: (b, jnp.minimum(i, 1), 0, 0)),
        _const_spec((1, D)), _const_spec((D, P_W)),
        _const_spec((256, 256)), _const_spec((512, 512)), _const_spec((256, 256)),
        _const_spec((1, 256)), _const_spec((1, 256)),
        _const_spec((1, 256)), _const_spec((256, 512)), _const_spec((1, 128)), _const_spec((128, 1024)),
        _const_spec((1, 512)), _const_spec((1, 512)),
        _const_spec((1, 256)), _const_spec((1, 256)),
        tab(512), tab(512), tab(512), tab(256), tab(256), tab(256),
    ]
    col = lambda w_: pl.BlockSpec((1, w_, TM), lambda b, i: (b, 0, i))
    out_shape = [jax.ShapeDtypeStruct((bsz, t, 1024), BF16),
                 jax.ShapeDtypeStruct((bsz, t, 1024), BF16),
                 jax.ShapeDtypeStruct((bsz, HEADS * V_AUG, t), BF16),
                 jax.ShapeDtypeStruct((bsz, t, 512), BF16),
                 jax.ShapeDtypeStruct((bsz, HEADS * V_AUG, t), BF16),
                 jax.ShapeDtypeStruct((bsz, t, SSD_W), F32)]
    out_specs = [row(1024), row(1024), col(HEADS * V_AUG), row(512), col(HEADS * V_AUG), row(SSD_W)]
    return pl.pallas_call(
        _inproj_kernel, grid=(bsz, nt), in_specs=in_specs, out_specs=out_specs, out_shape=out_shape,
        compiler_params=_cparams(2), name="inproj",
    )(h, modsel, gmix, w, consts["bd64"], consts["bd128"], consts["bd32"],
      lp["na_gq"], lp["na_gk"], lp["mla_gqa"], lp["mla_wqb"], lp["mla_gkva"], lp["mla_wkvb"],
      lp["mla_gq"], lp["mla_gk"], lp["diff_gq"], lp["diff_gk"],
      tabs["mc"], tabs["ms1"], tabs["ms2"], tabs["dc"], tabs["ds1"], tabs["ds2"])


def _attend(jobs, s_ref):
    n = len(jobs[0][1])
    total = len(jobs) * n
    ahead = total if n == 1 else n + min(2, n - 1)
    assert n > 1 or total * CK <= s_ref.shape[2]
    m_run = [None] * len(jobs)

    def cols(j, c):
        return (0, j * CK) if n == 1 else (j % 2, c * CK)

    def score(t):
        j, c = divmod(t, n)
        qh, chunks = jobs[j]
        s = _dot_nt(qh, chunks[c][0]())
        if chunks[c][2] is not None:
            s = s + chunks[c][2]()
        slot, c0 = cols(j, c)
        s_ref[slot, :, c0:c0 + CK] = s
        for b in range(CK // 128):
            blk = s[:, b * 128:(b + 1) * 128]
            m_run[j] = blk if m_run[j] is None else jnp.maximum(m_run[j], blk)

    for t in range(min(ahead, total)):
        score(t)
    outs = []
    for j in range(len(jobs)):
        m = jnp.broadcast_to(jnp.max(m_run[j], axis=-1, keepdims=True), (TM, 128))
        acc = None
        for c in range(n):
            slot, c0 = cols(j, c)
            e = jnp.concatenate(
                [jnp.exp2(s_ref[slot, :, c0 + b * 128:c0 + (b + 1) * 128] - m) for b in range(CK // 128)], axis=-1)
            pv = _dot(e.astype(BF16), jobs[j][1][c][1]())
            acc = pv if acc is None else acc + pv
            if j * n + c + ahead < total:
                score(j * n + c + ahead)
        outs.append(acc[:, 0:V_HD] / acc[:, V_HD:V_HD + 1])
    return outs


def _attend_t(jobs, s_ref):
    n = len(jobs[0][1])
    total = len(jobs) * n
    ahead = total if n == 1 else n + min(2, n - 1)
    assert n > 1 or total * CK <= s_ref.shape[1]
    m_run = [None] * len(jobs)

    def rows(j, c):
        return (0, slice(j * CK, (j + 1) * CK)) if n == 1 else (j % 2, slice(c * CK, (c + 1) * CK))

    def score(t):
        j, c = divmod(t, n)
        qh, chunks = jobs[j]
        st = _dot_nt(chunks[c][0](), qh)
        slot, r = rows(j, c)
        s_ref[slot, r, :] = st
        m8 = jnp.max(st.reshape(CK // 8, 8, TM), axis=0)
        m_run[j] = m8 if m_run[j] is None else jnp.maximum(m_run[j], m8)

    for t in range(min(ahead, total)):
        score(t)
    outs = []
    for j in range(len(jobs)):
        m = jnp.max(m_run[j], axis=0, keepdims=True)
        acc = None
        for c in range(n):
            slot, r = rows(j, c)
            et = jnp.exp2(s_ref[slot, r, :] - m).astype(BF16)
            pv = _dot(jobs[j][1][c][1](), et)
            acc = pv if acc is None else acc + pv
            if j * n + c + ahead < total:
                score(j * n + c + ahead)
        outs.append(acc[0:V_HD, :] / acc[V_HD:V_HD + 1, :])
    return outs


def _kvt_chunks(k_ref, vt_ref, k_sl, h, n_chunks):
    return [(lambda c=c: k_ref[0, c * CK:(c + 1) * CK, k_sl],
             lambda c=c: vt_ref[0, h * V_AUG:(h + 1) * V_AUG, c * CK:(c + 1) * CK]) for c in range(n_chunks)]


def _kv_chunks(k_ref, v_ref, k_sl, v_sl, n_chunks, first=0, start=None, bias_fn=None):
    out = []
    for c in range(n_chunks):
        if start is None:
            rows = slice((first + c) * CK, (first + c + 1) * CK)
        else:
            rows = pl.ds(start + c * CK, CK)
        out.append((lambda rows=rows: k_ref[0, rows, k_sl],
                    lambda rows=rows: v_ref[0, rows, v_sl],
                    None if bias_fn is None else functools.partial(bias_fn, c)))
    return out


def _na_kernel(q_ref, k_ref, v_ref, tab_ref, o_ref, s_ref, *, rows):
    i = pl.program_id(1)
    r0 = (i - 1) * NA_R
    s0 = jnp.clip(r0 - NA_KH // 2, 0, rows - NA_W)

    def bias_chunk(h, c):
        lane = lax.broadcasted_iota(jnp.int32, (1, 2 * GRID_W), 1)
        row_blocks = []
        for qr in range(NA_R):
            q_row = r0 + qr
            lo = jnp.clip(q_row - NA_KH // 2, 0, rows - NA_KH)
            pieces = []
            for u in range(CK // (2 * GRID_W)):
                k_row = s0 + c * (CK // GRID_W) + 2 * u
                pen = [jnp.where((k_row + d >= lo) & (k_row + d < lo + NA_KH), 0.0, -1e30) for d in range(2)]
                idx = jnp.clip(k_row - q_row + (NA_KH - 1), -1, 2 * NA_KH - 1) + 1
                pieces.append(tab_ref[h, idx] + jnp.where(lane < GRID_W, pen[0], pen[1]))
            row_blocks.append(jnp.concatenate(pieces, axis=-1))
        return jnp.concatenate(row_blocks, axis=0)

    def run(window_start):
        jobs = []
        for h in range(HEADS):
            k_sl = slice(h * NA_HD, (h + 1) * NA_HD)
            v_sl = slice(h * V_AUG, (h + 1) * V_AUG)
            chunks = _kv_chunks(k_ref, v_ref, k_sl, v_sl, 1)
            if window_start is not None:
                chunks += _kv_chunks(k_ref, v_ref, k_sl, v_sl, NA_W * GRID_W // CK, start=window_start,
                                     bias_fn=functools.partial(bias_chunk, h))
            jobs.append((q_ref[0, :, k_sl], chunks))
        o_ref[0] = jnp.concatenate(_attend(jobs, s_ref), axis=-1).astype(BF16)

    @pl.when(i == 0)
    def _():
        run(None)

    @pl.when(i > 0)
    def _():
        run(pl.multiple_of(N_CTX + s0 * GRID_W, GRID_W))


def _na_attention(na_qkv, table):
    bsz, t, _ = na_qkv.shape
    rows = (t - N_CTX) // GRID_W
    return pl.pallas_call(
        functools.partial(_na_kernel, rows=rows), grid=(bsz, t // TM),
        in_specs=[pl.BlockSpec((1, TM, 256), lambda b, i: (b, i, 0)),
                  pl.BlockSpec((1, t, 256), lambda b, i: (b, 0, 1)),
                  pl.BlockSpec((1, t, 512), lambda b, i: (b, 0, 1)),
                  _const_spec(table.shape)],
        out_specs=pl.BlockSpec((1, TM, 256), lambda b, i: (b, i, 0)),
        out_shape=jax.ShapeDtypeStruct((bsz, t, 256), BF16),
        scratch_shapes=[pltpu.VMEM((2, TM, CK + NA_W * GRID_W), F32)],
        compiler_params=_cparams(2), name="na_attention",
    )(na_qkv, na_qkv, na_qkv, table)


def _na_bias_table(rel_bias):
    cq = np.arange(GRID_W)
    col_lo = np.clip(cq - NA_KW // 2, 0, GRID_W - NA_KW)
    col_ok = (cq[None, :] >= col_lo[:, None]) & (cq[None, :] < col_lo[:, None] + NA_KW)
    col_off = np.clip(cq[None, :] - cq[:, None], 1 - NA_KW, NA_KW - 1) + (NA_KW - 1)
    col_sel = jnp.asarray(col_off[..., None] == np.arange(2 * NA_KW - 1), F32)
    t1 = jnp.einsum("hab,qkb->haqk", rel_bias * LOG2E, col_sel, precision=lax.Precision.HIGHEST)
    t1 = jnp.where(col_ok[None, None], t1, -1e30)
    fill = jnp.full((HEADS, 1, GRID_W, GRID_W), -1e30, F32)
    ext = jnp.concatenate([fill, t1, fill, fill], axis=1)
    return jnp.concatenate([ext[:, :-1], ext[:, 1:]], axis=-1)


def _mla_kernel(q_ref, k_ref, v_ref, o_ref, s_ref):
    i = pl.program_id(1)

    def run(n_chunks):
        jobs = []
        for h in range(HEADS):
            sl = slice(h * MLA_QK_PAD, (h + 1) * MLA_QK_PAD)
            jobs.append((q_ref[0, :, sl], _kvt_chunks(k_ref, v_ref, sl, h, n_chunks)))
        o_ref[0] = jnp.concatenate(_attend_t(jobs, s_ref), axis=0).T.astype(BF16)

    @pl.when(i == 0)
    def _():
        run(1)

    @pl.when(i > 0)
    def _():
        run(k_ref.shape[1] // CK)


def _mla_attention(mla_qk, mla_vt):
    bsz, t, _ = mla_qk.shape
    return pl.pallas_call(
        _mla_kernel, grid=(bsz, t // TM),
        in_specs=[pl.BlockSpec((1, TM, 512), lambda b, i: (b, i, 0)),
                  pl.BlockSpec((1, t, 512), lambda b, i: (b, 0, 1)),
                  pl.BlockSpec((1, HEADS * V_AUG, t), lambda b, i: (b, 0, 0))],
        out_specs=pl.BlockSpec((1, TM, 256), lambda b, i: (b, i, 0)),
        out_shape=jax.ShapeDtypeStruct((bsz, t, 256), BF16),
        scratch_shapes=[pltpu.VMEM((2, t, TM), F32)],
        compiler_params=_cparams(2), name="mla_attention",
    )(mla_qk, mla_qk, mla_vt)


def _diff_kernel(q_ref, k_ref, v_ref, lam_ref, gsub_ref, o_ref, s_ref, *, lam_init):
    i = pl.program_id(1)
    lv = lam_ref[...]
    lam = (jnp.exp(jnp.sum(lv[0:1] * lv[1:2], axis=-1, keepdims=True))
           - jnp.exp(jnp.sum(lv[2:3] * lv[3:4], axis=-1, keepdims=True)) + lam_init)

    def run(n_chunks):
        first = lax.broadcasted_iota(jnp.int32, (TM, 2 * DIFF_QK), 1) < DIFF_QK
        jobs = []
        for h in range(HEADS):
            sl = slice(h * 2 * DIFF_QK, (h + 1) * 2 * DIFF_QK)
            chunks = _kvt_chunks(k_ref, v_ref, sl, h, n_chunks)
            qh = q_ref[0, :, sl]
            zero = jnp.zeros_like(qh)
            jobs += [(jnp.where(first, qh, zero), chunks), (jnp.where(first, zero, qh), chunks)]
        ot = _attend_t(jobs, s_ref)
        outs = []
        for h in range(HEADS):
            d = ot[2 * h] - lam * ot[2 * h + 1]
            ms = jnp.mean(d * d, axis=0, keepdims=True)
            outs.append(d * lax.rsqrt(ms + EPS) * gsub_ref[...] * (1.0 - lam_init))
        o_ref[0] = jnp.concatenate(outs, axis=0).T.astype(BF16)

    @pl.when(i == 0)
    def _():
        run(1)

    @pl.when(i > 0)
    def _():
        run(k_ref.shape[1] // CK)


def _diff_attention(diff_qk, diff_vt, lam_vecs, g_sub, lam_init):
    bsz, t, _ = diff_qk.shape
    g_sub_t = jnp.broadcast_to(g_sub.reshape(DIFF_V, 1), (DIFF_V, TM))
    return pl.pallas_call(
        functools.partial(_diff_kernel, lam_init=lam_init), grid=(bsz, t // TM),
        in_specs=[pl.BlockSpec((1, TM, 256), lambda b, i: (b, i, 0)),
                  pl.BlockSpec((1, t, 256), lambda b, i: (b, 0, 1)),
                  pl.BlockSpec((1, HEADS * V_AUG, t), lambda b, i: (b, 0, 0)),
                  _const_spec((4, DIFF_QK)), _const_spec((DIFF_V, TM))],
        out_specs=pl.BlockSpec((1, TM, 256), lambda b, i: (b, i, 0)),
        out_shape=jax.ShapeDtypeStruct((bsz, t, 256), BF16),
        scratch_shapes=[pltpu.VMEM((2, t, TM), F32)],
        compiler_params=_cparams(2), name="diff_attention",
    )(diff_qk, diff_qk, diff_vt, lam_vecs, g_sub_t)


def _softplus(x):
    return jnp.maximum(x, 0.0) + jnp.log1p(jnp.exp(-jnp.abs(x)))


def _ssd_kernel(raw_ref, convw_ref, convb_ref, dtb_ref, alog_ref, dskip_ref, gnorm_ref,
                tril_ref, triu_ref, mlow_ref, mupp_ref, o_ref, xact_ref, yacc_ref, state_ref):
    t = raw_ref.shape[1]
    nt = t // TM
    xbc0 = SSD_INNER
    dt0 = SSD_INNER + SSD_XBC

    cw = convw_ref[...]
    cb = convb_ref[...]
    for j in range(nt):
        lo = j * TM
        cur = raw_ref[0, lo:lo + TM, xbc0:xbc0 + SSD_XBC]
        zeros8 = jnp.zeros((8, SSD_XBC), F32)
        prev = raw_ref[0, lo - 8:lo, xbc0:xbc0 + SSD_XBC] if j >= 2 else zeros8
        nxt = raw_ref[0, lo + TM:lo + TM + 8, xbc0:xbc0 + SSD_XBC] if 1 <= j < nt - 1 else zeros8
        u = jnp.concatenate([prev, cur, nxt], axis=0)
        acc = cb
        for kk in range(SSD_CONV):
            off = 8 - SSD_CONV // 2 + kk
            acc = acc + cw[kk:kk + 1, :] * u[off:off + TM, :]
        xact_ref[lo:lo + TM, :] = _silu(acc)

    a_pad = -jnp.exp(alog_ref[...])
    for d in range(2):
        tri_ref = tril_ref if d == 0 else triu_ref
        off_ref = mlow_ref if d == 0 else mupp_ref
        state_ref[...] = jnp.zeros_like(state_ref)

        def chunk(c, carry, d=d, tri_ref=tri_ref, off_ref=off_ref):
            if d == 0:
                blk = c
            else:
                blk = jnp.where(c == 0, 0, nt - c)
            off = pl.multiple_of(blk * TM, TM)
            rows = pl.ds(off, TM)
            dt = _softplus(raw_ref[0, rows, dt0:dt0 + 128] + dtb_ref[...])
            la = dt * a_pad
            cum = _split_dot_left(tri_ref[...], la)
            cum_t = cum.T
            dt_t = dt.T
            total = cum[TM - 1:TM, :] if d == 0 else cum[0:1, :]
            e_in = jnp.exp(cum)
            w_t = (jnp.exp(total - cum) * dt).T
            e_tot = jnp.exp(total)
            xs = xact_ref[rows, 0:SSD_INNER]
            xs_b = xs.astype(BF16)
            ys = []
            for g in range(SSD_GROUPS):
                bm = xact_ref[rows, SSD_INNER + g * SSD_STATE:SSD_INNER + (g + 1) * SSD_STATE]
                cm_b = xact_ref[rows, SSD_INNER + (SSD_GROUPS + g) * SSD_STATE:
                                SSD_INNER + (SSD_GROUPS + g + 1) * SSD_STATE].astype(BF16)
                gm = _dot_nt(cm_b, bm.astype(BF16))
                bm_t = bm.T
                for hh in range(HEADS // SSD_GROUPS):
                    h = g * (HEADS // SSD_GROUPS) + hh
                    j = d * HEADS + h
                    x_b = xs_b[:, h * SSD_HD:(h + 1) * SSD_HD]
                    dec = jnp.exp(cum[:, j:j + 1] - cum_t[j:j + 1, :] + off_ref[...])
                    y_d = _dot((gm * dec * dt_t[j:j + 1, :]).astype(BF16), x_b)
                    st = state_ref[h]
                    y_o = e_in[:, j:j + 1] * _dot(cm_b, st.astype(BF16))
                    new = _dot((bm_t * w_t[j:j + 1, :]).astype(BF16), x_b)
                    state_ref[h] = st * e_tot[:, j:j + 1] + new
                    ys.append(y_d + y_o + dskip_ref[d, h] * xs[:, h * SSD_HD:(h + 1) * SSD_HD])
            y = jnp.concatenate(ys, axis=-1)
            if d == 0:
                yacc_ref[rows, :] = y
            else:
                yacc_ref[rows, :] += y
            return carry

        lax.fori_loop(0, nt, chunk, 0)

    for j in range(nt):
        lo = j * TM
        y = yacc_ref[lo:lo + TM, :] * _silu(raw_ref[0, lo:lo + TM, 0:SSD_INNER])
        o_ref[0, lo:lo + TM, :] = _rms(y, gnorm_ref[...]).astype(BF16)


def _ssd(ssd_raw, lp, consts):
    bsz, t, _ = ssd_raw.shape
    c1 = lambda shape: pl.BlockSpec(shape, lambda b: (0,) * len(shape))
    return pl.pallas_call(
        _ssd_kernel, grid=(bsz,),
        in_specs=[pl.BlockSpec((1, t, SSD_W), lambda b: (b, 0, 0)),
                  c1((SSD_CONV, SSD_XBC)), c1((1, SSD_XBC)), c1((1, 128)), c1((1, 128)),
                  pl.BlockSpec(memory_space=pltpu.SMEM),
                  c1((1, SSD_INNER)), c1((TM, TM)), c1((TM, TM)), c1((TM, TM)), c1((TM, TM))],
        out_specs=pl.BlockSpec((1, t, SSD_INNER), lambda b: (b, 0, 0)),
        out_shape=jax.ShapeDtypeStruct((bsz, t, SSD_INNER), BF16),
        scratch_shapes=[pltpu.VMEM((t, SSD_XBC), F32), pltpu.VMEM((t, SSD_INNER), F32),
                        pltpu.VMEM((HEADS, SSD_STATE, SSD_HD), F32)],
        compiler_params=_cparams(1), name="ssd",
    )(ssd_raw, lp["ssd_convw"], lp["ssd_convb"], lp["ssd_dtb"], lp["ssd_alog"], lp["ssd_dskip"],
      lp["ssd_gnorm"], consts["tril"], consts["triu"], consts["mlow"], consts["mupp"])


def _outproj_kernel(na_ref, mla_ref, diff_ref, ssd_ref, h_ref, mod_ref, wout_ref, gffn_ref,
                    wrh_ref, wrl_ref, br_ref, lstrict_ref, hout_ref, f_ref, route_ref, cnt_ref, run_ref):
    first = (pl.program_id(0) == 0) & (pl.program_id(1) == 0)

    @pl.when(first)
    def _():
        run_ref[...] = jnp.zeros_like(run_ref)

    o = (_dot(na_ref[0], wout_ref[0:256, :]) + _dot(mla_ref[0], wout_ref[256:512, :])
         + _dot(diff_ref[0], wout_ref[512:768, :]) + _dot(ssd_ref[0], wout_ref[768:1024, :]))
    gate = mod_ref[0, 0, 2:3, :]
    hn = h_ref[0] + gate * o
    hout_ref[0] = hn
    f = _rms(hn, gffn_ref[...]) * (1.0 + mod_ref[0, 0, 4:5, :]) + mod_ref[0, 0, 3:4, :]
    f_ref[0] = f

    f_hi = f.astype(BF16)
    f_lo = (f - f_hi.astype(F32)).astype(BF16)
    logits = _dot(f_hi, wrh_ref[...]) + _dot(f_lo, wrh_ref[...]) + _dot(f_hi, wrl_ref[...]) + br_ref[...]
    lane = lax.broadcasted_iota(jnp.int32, logits.shape, 1)
    lane_f = lane.astype(F32)
    neg = jnp.float32(-jnp.inf)
    big = jnp.float32(1e9)
    gl = jnp.where(lane < MOE_GROUPS, logits, neg)
    gmax = jnp.max(gl, axis=-1, keepdims=True)
    g_top_p = 1.0 / jnp.sum(jnp.exp(gl - gmax), axis=-1, keepdims=True)
    g_top = jnp.min(jnp.where(gl == gmax, lane_f, big), axis=-1, keepdims=True).astype(jnp.int32)
    in_group = (lane >= MOE_GROUPS) & (lane < MOE_GROUPS + MOE_EXPERTS) & (((lane - MOE_GROUPS) // MOE_EPG) == g_top)
    el = jnp.where(in_group, logits, neg)
    m1 = jnp.max(el, axis=-1, keepdims=True)
    i1 = jnp.min(jnp.where(el == m1, lane_f, big), axis=-1, keepdims=True)
    el2 = jnp.where(lane_f == i1, neg, el)
    m2 = jnp.max(el2, axis=-1, keepdims=True)
    i2 = jnp.min(jnp.where(el2 == m2, lane_f, big), axis=-1, keepdims=True)
    x2 = jnp.exp(m2 - m1)
    w1 = g_top_p / (1.0 + x2)
    w2 = g_top_p * x2 / (1.0 + x2)
    e1 = i1 - MOE_GROUPS
    e2 = i2 - MOE_GROUPS

    onehot = ((lane_f == e1) | (lane_f == e2)).astype(F32)
    before = _dot(lstrict_ref[...], onehot.astype(BF16)) + run_ref[...]
    r1 = jnp.sum(jnp.where(lane_f == e1, before, 0.0), axis=-1, keepdims=True)
    r2 = jnp.sum(jnp.where(lane_f == e2, before, 0.0), axis=-1, keepdims=True)
    run_new = run_ref[...] + jnp.sum(onehot, axis=0, keepdims=True)
    run_ref[...] = run_new
    cnt_ref[...] = run_new
    route = jnp.zeros(logits.shape, F32)
    for idx, val in enumerate((e1, e2, r1, r2, w1, w2)):
        route = jnp.where(lane == idx, val, route)
    route_ref[0] = route


def _outproj(mix, h, modsel, wout, gffn, lp, consts):
    bsz, t, _ = h.shape
    row = lambda w_: pl.BlockSpec((1, TM, w_), lambda b, i: (b, i, 0))
    return pl.pallas_call(
        _outproj_kernel, grid=(bsz, t // TM),
        in_specs=[row(256), row(256), row(256), row(256), row(D),
                  pl.BlockSpec((1, 1, 6, D), lambda b, i: (b, jnp.minimum(i, 1), 0, 0)),
                  _const_spec((D, D)), _const_spec((1, D)), _const_spec((D, 128)), _const_spec((D, 128)),
                  _const_spec((1, 128)), _const_spec((TM, TM))],
        out_specs=[row(D), row(D), row(128), _const_spec((1, 128))],
        out_shape=[jax.ShapeDtypeStruct((bsz, t, D), F32), jax.ShapeDtypeStruct((bsz, t, D), F32),
                   jax.ShapeDtypeStruct((bsz, t, 128), F32), jax.ShapeDtypeStruct((1, 128), F32)],
        scratch_shapes=[pltpu.VMEM((1, 128), F32)],
        compiler_params=_cparams(2), name="outproj_router",
    )(*mix, h, modsel, wout, gffn, lp["wr_hi"], lp["wr_lo"], lp["br"], consts["lstrict"])


def _row_dmas(make_copy, whole_copies):
    def issue(r8, c):
        for u in range(DMA_UNROLL):
            for k in range(2):
                make_copy(r8 * DMA_UNROLL + u, k).start(priority=k)
        return c

    lax.fori_loop(0, TM // DMA_UNROLL, issue, 0)
    for w in whole_copies:
        w.wait()


def _dispatch_kernel(dest_ref, f_ref, xs_in_ref, xs_ref, sem):
    del xs_in_ref
    base = pl.program_id(0) * TM

    def make_copy(r, k):
        d = dest_ref[(base + r) * 2 + k]
        return pltpu.make_async_copy(f_ref.at[pl.ds(r, 1)], xs_ref.at[pl.ds(d, 1)], sem)

    whole = pltpu.make_async_copy(f_ref, xs_ref.at[pl.ds(0, TM)], sem)
    _row_dmas(make_copy, [whole, whole])


def _dispatch(dest, f2d, cap):
    n_tok = f2d.shape[0]
    zeros = jnp.zeros((cap, D), F32)
    return pl.pallas_call(
        _dispatch_kernel,
        grid_spec=pltpu.PrefetchScalarGridSpec(
            num_scalar_prefetch=1, grid=(n_tok // TM,),
            in_specs=[pl.BlockSpec((TM, D), lambda i, dest: (i, 0)),
                      pl.BlockSpec(memory_space=pl.ANY)],
            out_specs=pl.BlockSpec(memory_space=pl.ANY),
            scratch_shapes=[pltpu.SemaphoreType.DMA(())]),
        out_shape=jax.ShapeDtypeStruct((cap, D), F32),
        input_output_aliases={2: 0},
        compiler_params=_cparams(1), name="moe_dispatch",
    )(dest, f2d, zeros)


def _experts_kernel(be_ref, nb_ref, x_ref, wgu_ref, wd_ref, y_ref):
    @pl.when(pl.program_id(0) < nb_ref[0])
    def _():
        x = x_ref[...].astype(BF16)
        gu = _dot(x, wgu_ref[0])
        a = _silu(gu[:, 0:MOE_FF]) * gu[:, MOE_FF:2 * MOE_FF]
        y_ref[...] = _dot(a.astype(BF16), wd_ref[0])

    @pl.when(pl.program_id(0) >= nb_ref[0])
    def _():
        y_ref[...] = jnp.zeros_like(y_ref)


def _experts(block_e, n_used, xs, wgu, wd):
    cap = xs.shape[0]
    return pl.pallas_call(
        _experts_kernel,
        grid_spec=pltpu.PrefetchScalarGridSpec(
            num_scalar_prefetch=2, grid=(cap // MOE_MB,),
            in_specs=[pl.BlockSpec((MOE_MB, D), lambda i, be, nb: (i, 0)),
                      pl.BlockSpec((1, D, 2 * MOE_FF), lambda i, be, nb: (be[i], 0, 0)),
                      pl.BlockSpec((1, MOE_FF, D), lambda i, be, nb: (be[i], 0, 0))],
            out_specs=pl.BlockSpec((MOE_MB, D), lambda i, be, nb: (i, 0))),
        out_shape=jax.ShapeDtypeStruct((cap, D), F32),
        compiler_params=_cparams(1), name="moe_experts",
    )(block_e, n_used, xs, wgu, wd)


def _combine_kernel(dest_ref, h_ref, mod_ref, route_ref, y_ref, o_ref, buf_ref, sem):
    base = (pl.program_id(0) * pl.num_programs(1) + pl.program_id(1)) * TM

    def make_copy(r, k):
        d = dest_ref[(base + r) * 2 + k]
        return pltpu.make_async_copy(y_ref.at[pl.ds(d, 1)], buf_ref.at[k, pl.ds(r, 1)], sem)

    whole = [pltpu.make_async_copy(y_ref.at[pl.ds(0, TM)], buf_ref.at[k], sem) for k in range(2)]
    _row_dmas(make_copy, whole)
    w1 = route_ref[0, :, 4:5]
    w2 = route_ref[0, :, 5:6]
    o_ref[0] = h_ref[0] + mod_ref[0, 0, 5:6, :] * (w1 * buf_ref[0] + w2 * buf_ref[1])


def _combine(dest, h, modsel, route, y):
    bsz, t, _ = h.shape
    row = lambda w_: pl.BlockSpec((1, TM, w_), lambda b, i, dest: (b, i, 0))
    return pl.pallas_call(
        _combine_kernel,
        grid_spec=pltpu.PrefetchScalarGridSpec(
            num_scalar_prefetch=1, grid=(bsz, t // TM),
            in_specs=[row(D),
                      pl.BlockSpec((1, 1, 6, D), lambda b, i, dest: (b, jnp.minimum(i, 1), 0, 0)),
                      row(128),
                      pl.BlockSpec(memory_space=pl.ANY)],
            out_specs=row(D),
            scratch_shapes=[pltpu.VMEM((2, TM, D), F32), pltpu.SemaphoreType.DMA(())]),
        out_shape=jax.ShapeDtypeStruct((bsz, t, D), F32),
        compiler_params=_cparams(2), name="moe_combine",
    )(dest, h, modsel, route, y)


def _moe_plan(route, counts, n_blocks):
    cnt = counts[0, :MOE_EXPERTS].astype(jnp.int32)
    padded = (cnt + MOE_MB - 1) // MOE_MB * MOE_MB
    pad_end = jnp.cumsum(padded)
    pad_start = pad_end - padded
    e = route[..., 0:2].astype(jnp.int32)
    r = route[..., 2:4].astype(jnp.int32)
    onehot = e[..., None] == jnp.arange(MOE_EXPERTS, dtype=jnp.int32)
    dest = jnp.sum(jnp.where(onehot, pad_start, 0), axis=-1) + r
    blk0 = jnp.arange(n_blocks, dtype=jnp.int32) * MOE_MB
    block_e = jnp.minimum(jnp.sum(blk0[:, None] >= pad_end[None, :], axis=-1), MOE_EXPERTS - 1)
    n_used = (pad_end[-1:] // MOE_MB).astype(jnp.int32)
    return dest.reshape(-1), block_e.astype(jnp.int32), n_used


def _block_diag(n, seg):
    idx = np.arange(n) // seg
    return jnp.asarray(idx[:, None] == idx[None, :], BF16)


def _constants():
    lower = np.tril(np.ones((TM, TM), np.float32))
    upper = np.triu(np.ones((TM, TM), np.float32))
    return {"bd64": _block_diag(256, 64), "bd128": _block_diag(512, 128), "bd32": _block_diag(256, 32),
            "tril": jnp.asarray(lower, BF16), "triu": jnp.asarray(upper, BF16),
            "mlow": jnp.asarray((lower - 1.0) * 1e30, F32), "mupp": jnp.asarray((upper - 1.0) * 1e30, F32),
            "lstrict": jnp.asarray(np.tril(np.ones((TM, TM)), -1), BF16)}


def _rope_tables(n_lat, t):
    n_freq = 8
    inv = jnp.power(10000.0, -jnp.arange(n_freq, dtype=F32) / n_freq)
    tok = jnp.arange(n_lat, dtype=jnp.int32)
    row = (tok // GRID_W).astype(F32)
    col = (tok % GRID_W).astype(F32)
    ang = jnp.concatenate([row[:, None] * inv, col[:, None] * inv], axis=-1)
    n_c = t - n_lat
    cos = jnp.concatenate([jnp.ones((n_c, 16), F32), jnp.cos(ang)], axis=0)
    sin = jnp.concatenate([jnp.zeros((n_c, 16), F32), jnp.sin(ang)], axis=0)
    z16 = jnp.zeros((t, 16), F32)
    one = lambda w_: jnp.ones((t, w_), F32)
    zero = lambda w_: jnp.zeros((t, w_), F32)
    mc = jnp.concatenate([one(64), cos, cos, one(32)], axis=-1)
    ms1 = jnp.concatenate([zero(64), -sin, z16, zero(32)], axis=-1)
    ms2 = jnp.concatenate([zero(64), z16, sin, zero(32)], axis=-1)
    dc = jnp.concatenate([cos, cos], axis=-1)
    ds1 = jnp.concatenate([-sin, z16], axis=-1)
    ds2 = jnp.concatenate([z16, sin], axis=-1)
    tile = lambda a, n: jnp.tile(a, (1, n))
    return {"mc": tile(mc, 4), "ms1": tile(ms1, 4), "ms2": tile(ms2, 4),
            "dc": tile(dc, 8), "ds1": tile(ds1, 8), "ds2": tile(ds2, 8)}


def _pad_heads(w, width, padded):
    lead = w.shape[:-1]
    w = w.reshape(lead + (HEADS, width))
    w = jnp.pad(w, [(0, 0)] * len(lead) + [(0, 0), (0, padded - width)])
    return w.reshape(lead + (HEADS * padded,))


def _pack_w_in(w_in):
    n_layers = w_in.shape[0]
    z = lambda n: jnp.zeros((n_layers, D, n), w_in.dtype)
    na = w_in[:, :, 0:768]
    mla = w_in[:, :, 768:1184]
    diff = w_in[:, :, 1184:1952]
    ssd = w_in[:, :, 1952:2984]
    na_p = jnp.concatenate([na[:, :, 0:512], _pad_heads(na[:, :, 512:768], V_HD, V_AUG)], axis=-1)
    mla_p = jnp.concatenate([mla[:, :, 0:384], z(64), mla[:, :, 384:416], z(32)], axis=-1)
    diff_p = jnp.concatenate([diff[:, :, 0:512], _pad_heads(diff[:, :, 512:768], V_HD, V_AUG)], axis=-1)
    ssd_p = jnp.concatenate([ssd, z(SSD_W - 1032)], axis=-1)
    return jnp.concatenate([na_p, mla_p, diff_p, ssd_p], axis=-1).astype(BF16)


def _layer_params(l, p):
    row = lambda a: a.reshape(1, -1)
    t4 = lambda a: jnp.tile(a.reshape(1, -1), (1, HEADS))
    wqb = _pad_heads(p["mla_w_qb"][l], MLA_QK, MLA_QK_PAD)
    wkvb = p["mla_w_kvb"][l].reshape(MLA_KV_RANK, HEADS, MLA_NOPE + MLA_V)
    wkvb = jnp.concatenate([_pad_heads(wkvb[:, :, :MLA_NOPE].reshape(MLA_KV_RANK, -1), MLA_NOPE, MLA_QK_PAD),
                            _pad_heads(wkvb[:, :, MLA_NOPE:].reshape(MLA_KV_RANK, -1), MLA_V, V_AUG)], axis=-1)
    gpad = lambda g: jnp.tile(jnp.pad(g, (0, MLA_QK_PAD - MLA_QK)).reshape(1, -1), (1, HEADS))
    lane8 = lambda a: jnp.pad(a.reshape(1, -1), ((0, 0), (0, 128 - 2 * HEADS)))
    wr = jnp.concatenate([p["moe_w_group"][l], p["moe_w_expert"][l],
                          jnp.zeros((D, 128 - MOE_GROUPS - MOE_EXPERTS), F32)], axis=-1)
    wr_hi = wr.astype(BF16)
    br = jnp.concatenate([p["moe_b_group"][l], p["moe_b_expert"][l],
                          jnp.zeros((128 - MOE_GROUPS - MOE_EXPERTS,), F32)]).reshape(1, 128)
    return {
        "g_mix": row(p["g_mix"][l]), "g_ffn": row(p["g_ffn"][l]),
        "na_gq": t4(p["na_g_q"][l]), "na_gk": t4(p["na_g_k"][l]),
        "mla_gqa": row(p["mla_g_qa"][l]), "mla_wqb": wqb.astype(BF16),
        "mla_gkva": row(p["mla_g_kva"][l]), "mla_wkvb": wkvb.astype(BF16),
        "mla_gq": gpad(p["mla_g_q"][l]), "mla_gk": gpad(p["mla_g_k"][l]),
        "diff_gq": jnp.tile(p["diff_g_q"][l].reshape(1, -1), (1, 8)),
        "diff_gk": jnp.tile(p["diff_g_k"][l].reshape(1, -1), (1, 8)),
        "diff_lam": p["diff_lambda"][l], "diff_gsub": row(p["diff_g_sub"][l]),
        "ssd_convw": p["ssd_conv_w"][l], "ssd_convb": row(p["ssd_conv_b"][l]),
        "ssd_dtb": lane8(p["ssd_dt_bias"][l]), "ssd_alog": lane8(p["ssd_a_log"][l]),
        "ssd_dskip": p["ssd_d"][l], "ssd_gnorm": row(p["ssd_g_norm"][l]),
        "wr_hi": wr_hi, "wr_lo": (wr - wr_hi.astype(F32)).astype(BF16), "br": br,
    }


def _mixers(h, modsel, w_in_l, lp, consts, tabs, bias, lam_init):
    na_qkv, mla_qk, mla_vt, diff_qk, diff_vt, ssd_raw = _inproj(h, modsel, lp["g_mix"], w_in_l, consts, lp, tabs)
    return (_na_attention(na_qkv, bias),
            _mla_attention(mla_qk, mla_vt),
            _diff_attention(diff_qk, diff_vt, lp["diff_lam"], lp["diff_gsub"], lam_init),
            _ssd(ssd_raw, lp, consts))


def _moe(hn, f, route, counts, modsel, wgu_l, wd_l):
    bsz, t, _ = hn.shape
    n_asg = bsz * t * 2
    n_blocks = -(-n_asg // MOE_MB) + MOE_EXPERTS
    dest, block_e, n_used = _moe_plan(route, counts, n_blocks)
    xs = _dispatch(dest, f.reshape(bsz * t, D), n_blocks * MOE_MB)
    y = _experts(block_e, n_used, xs, wgu_l, wd_l)
    return _combine(dest, hn, modsel, route, y)


def kernel(x, c, ctx, c_ctx, w_mod, b_mod, g_mix, w_in, w_out, na_g_q, na_g_k, na_rel_bias,
           mla_g_qa, mla_w_qb, mla_g_kva, mla_w_kvb, mla_g_q, mla_g_k,
           diff_g_q, diff_g_k, diff_lambda, diff_g_sub,
           ssd_conv_w, ssd_conv_b, ssd_dt_bias, ssd_a_log, ssd_d, ssd_g_norm,
           g_ffn, moe_w_group, moe_b_group, moe_w_expert, moe_b_expert, moe_w_gate, moe_w_up, moe_w_down):
    p = dict(g_mix=g_mix, g_ffn=g_ffn, na_g_q=na_g_q, na_g_k=na_g_k,
             mla_g_qa=mla_g_qa, mla_w_qb=mla_w_qb, mla_g_kva=mla_g_kva, mla_w_kvb=mla_w_kvb,
             mla_g_q=mla_g_q, mla_g_k=mla_g_k, diff_g_q=diff_g_q, diff_g_k=diff_g_k,
             diff_lambda=diff_lambda, diff_g_sub=diff_g_sub,
             ssd_conv_w=ssd_conv_w, ssd_conv_b=ssd_conv_b, ssd_dt_bias=ssd_dt_bias, ssd_a_log=ssd_a_log,
             ssd_d=ssd_d, ssd_g_norm=ssd_g_norm, moe_w_group=moe_w_group, moe_b_group=moe_b_group,
             moe_w_expert=moe_w_expert, moe_b_expert=moe_b_expert)
    bsz, n_lat, _ = x.shape
    n_ctx = ctx.shape[1]
    assert n_ctx == N_CTX == TM == CK and n_lat % TM == 0 and bsz < 16
    t = n_ctx + n_lat
    n_layers = w_mod.shape[0]
    rows = n_lat // GRID_W
    assert rows % NA_R == 0 and rows >= NA_W and (NA_W * GRID_W) % CK == 0

    consts = _constants()
    tabs = _rope_tables(n_lat, t)
    w_in_p = _pack_w_in(w_in)
    w_out_b = w_out.astype(BF16)
    wgu = jnp.concatenate([moe_w_gate, moe_w_up], axis=-1).astype(BF16)
    wd = moe_w_down.astype(BF16)

    cvec = jnp.concatenate([c, c_ctx[None, :], jnp.zeros((16 - bsz - 1, D), F32)], axis=0)
    mod = _modulation(cvec, w_mod, b_mod).reshape(n_layers, 16, 6, D)

    h = jnp.concatenate([ctx, x], axis=1)
    for l in range(n_layers):
        lp = _layer_params(l, p)
        modsel = jnp.stack([jnp.broadcast_to(mod[l, bsz][None], (bsz, 6, D)), mod[l, :bsz]], axis=1)
        lam_init = 0.8 - 0.6 * math.exp(-0.3 * l)
        bias = _na_bias_table(na_rel_bias[l])
        mix = _mixers(h, modsel, w_in_p[l], lp, consts, tabs, bias, lam_init)
        hn, f, route, counts = _outproj(mix, h, modsel, w_out_b[l], lp["g_ffn"], lp, consts)
        h = _moe(hn, f, route, counts, modsel, wgu[l], wd[l])
    return h[:, n_ctx:, :]
```

```python
import functools
import math

import numpy as np
import jax
import jax.numpy as jnp
from jax import lax
from jax.experimental import pallas as pl
from jax.experimental.pallas import tpu as pltpu

F32 = jnp.float32
BF16 = jnp.bfloat16

D = 1024
GRID_W = 64
N_CTX = 256
HEADS = 4
NA_HD = 64
NA_KH = 8
NA_KW = 16
MLA_NOPE = 64
MLA_ROPE = 32
MLA_QK = MLA_NOPE + MLA_ROPE
MLA_QK_PAD = 128
MLA_V = 64
MLA_Q_RANK = 256
MLA_KV_RANK = 128
DIFF_QK = 32
DIFF_V = 64
SSD_INNER = 256
SSD_HD = 64
SSD_STATE = 128
SSD_GROUPS = 2
SSD_CONV = 5
SSD_XBC = 768
MOE_GROUPS = 4
MOE_EPG = 4
MOE_EXPERTS = 16
MOE_FF = 512
EPS = 1e-6

TM = 256
MOE_MB = 256
NA_R = 4
NA_W = 12
V_HD = 64
V_AUG = 128
CK = 256
LOG2E = math.log2(math.e)
RUN_ALIGN = 8
RUN_BITS = (3, 9)
SORT_ROWS = 2 * TM + 128

P_NA = 0
P_MLA = 1024
P_DIFF = 1536
P_SSD = 2560
P_W = 3712
SSD_W = 1152

VMEM_LIMIT = 56 * 1024 * 1024


def _cparams(n_axes):
    return pltpu.CompilerParams(dimension_semantics=("arbitrary",) * n_axes,
                                vmem_limit_bytes=VMEM_LIMIT)


def _dot(a, b):
    return jnp.dot(a, b, preferred_element_type=F32)


def _dot_nt(a, b):
    return lax.dot_general(a, b, (((1,), (1,)), ((), ())), preferred_element_type=F32)


def _dot_tn(a, b):
    return lax.dot_general(a, b, (((0,), (0,)), ((), ())), preferred_element_type=F32)


def _split3(x):
    hi = x.astype(BF16)
    r1 = x - hi.astype(F32)
    mid = r1.astype(BF16)
    lo = (r1 - mid.astype(F32)).astype(BF16)
    return hi, mid, lo


def _split_dot(x, m):
    hi, mid, lo = _split3(x)
    return _dot(hi, m) + _dot(mid, m) + _dot(lo, m)


def _split_dot_left(m, x):
    hi, mid, lo = _split3(x)
    return _dot(m, hi) + _dot(m, mid) + _dot(m, lo)


def _rms(x, g):
    ms = jnp.mean(x * x, axis=-1, keepdims=True)
    return x * lax.rsqrt(ms + EPS) * g


def _seg_rms(x, bd, inv_n, g):
    x2 = x * x
    hi = x2.astype(BF16)
    lo = (x2 - hi.astype(F32)).astype(BF16)
    ms = (_dot(hi, bd) + _dot(lo, bd)) * inv_n
    return x * lax.rsqrt(ms + EPS) * g


def _silu(x):
    return x * jax.nn.sigmoid(x)


def _rope(x, c, s1, s2, width):
    rot = 16
    return x * c + pltpu.roll(x, width - rot, 1) * s1 + pltpu.roll(x, rot, 1) * s2


def _with_ones(v):
    lane = lax.broadcasted_iota(jnp.int32, v.shape, 1)
    return jnp.where(lane % V_AUG == V_HD, 1.0, v)


def _mod_kernel(c_ref, w_ref, b_ref, o_ref):
    s = _silu(c_ref[...])
    o_ref[0] = _dot(s.astype(BF16), w_ref[0].astype(BF16)) + b_ref[0]


def _modulation(cvec, w_mod, b_mod):
    n_layers = w_mod.shape[0]
    tn = 1536
    return pl.pallas_call(
        _mod_kernel,
        grid=(n_layers, 6 * D // tn),
        in_specs=[pl.BlockSpec((16, D), lambda l, j: (0, 0)),
                  pl.BlockSpec((1, D, tn), lambda l, j: (l, 0, j)),
                  pl.BlockSpec((1, 1, tn), lambda l, j: (l, 0, j))],
        out_specs=pl.BlockSpec((1, 16, tn), lambda l, j: (l, 0, j)),
        out_shape=jax.ShapeDtypeStruct((n_layers, 16, 6 * D), F32),
        compiler_params=_cparams(2),
        name="modulation",
    )(cvec, w_mod, b_mod.reshape(n_layers, 1, 6 * D))


def _inproj_kernel(h_ref, mod_ref, gmix_ref, w_ref, bd64_ref, bd128_ref, bd32_ref,
                   nagq_ref, nagk_ref, gqa_ref, wqb_ref, gkva_ref, wkvb_ref, mgq_ref, mgk_ref,
                   dgq_ref, dgk_ref, mc_ref, ms1_ref, ms2_ref, dc_ref, ds1_ref, ds2_ref,
                   na_ref, mla_ref, mlavt_ref, diff_ref, diffvt_ref, ssd_ref):
    x = h_ref[0]
    shift = mod_ref[0, 0, 0:1, :]
    scale = mod_ref[0, 0, 1:2, :]
    a = _rms(x, gmix_ref[...]) * (1.0 + scale) + shift
    p = _dot(a.astype(BF16), w_ref[...])


    bd64 = bd64_ref[...]
    q = p[:, P_NA:P_NA + 256]
    k = p[:, P_NA + 256:P_NA + 512]
    na_ref[0, :, 0:256] = (_seg_rms(q, bd64, 1.0 / NA_HD, nagq_ref[...]) * (NA_HD ** -0.5 * LOG2E)).astype(BF16)
    na_ref[0, :, 256:512] = _seg_rms(k, bd64, 1.0 / NA_HD, nagk_ref[...]).astype(BF16)
    na_ref[0, :, 512:1024] = _with_ones(p[:, P_NA + 512:P_NA + 1024]).astype(BF16)

    bd128 = bd128_ref[...]
    cq = p[:, P_MLA:P_MLA + 256]
    ckv = p[:, P_MLA + 256:P_MLA + 384]
    kr = p[:, P_MLA + 384:P_MLA + 512]
    q2 = _dot(_rms(cq, gqa_ref[...]).astype(BF16), wqb_ref[...])
    kv = _dot(_rms(ckv, gkva_ref[...]).astype(BF16), wkvb_ref[...])
    k2 = kv[:, 0:512] + jnp.concatenate([kr] * HEADS, axis=-1)
    mc, ms1, ms2 = mc_ref[...], ms1_ref[...], ms2_ref[...]
    qn = _rope(_seg_rms(q2, bd128, 1.0 / MLA_QK, mgq_ref[...]), mc, ms1, ms2, 512)
    kn = _rope(_seg_rms(k2, bd128, 1.0 / MLA_QK, mgk_ref[...]), mc, ms1, ms2, 512)
    mla_ref[0, :, 0:512] = (qn * (MLA_QK ** -0.5 * LOG2E)).astype(BF16)
    mla_ref[0, :, 512:1024] = kn.astype(BF16)
    mlavt_ref[0] = _with_ones(kv[:, 512:1024]).T.astype(BF16)

    bd32 = bd32_ref[...]
    dc, ds1, ds2 = dc_ref[...], ds1_ref[...], ds2_ref[...]
    dq = p[:, P_DIFF:P_DIFF + 256]
    dk = p[:, P_DIFF + 256:P_DIFF + 512]
    dqn = _rope(_seg_rms(dq, bd32, 1.0 / DIFF_QK, dgq_ref[...]), dc, ds1, ds2, 256)
    dkn = _rope(_seg_rms(dk, bd32, 1.0 / DIFF_QK, dgk_ref[...]), dc, ds1, ds2, 256)
    diff_ref[0, :, 0:256] = (dqn * (DIFF_QK ** -0.5 * LOG2E)).astype(BF16)
    diff_ref[0, :, 256:512] = dkn.astype(BF16)
    diffvt_ref[0] = _with_ones(p[:, P_DIFF + 512:P_DIFF + 1024]).T.astype(BF16)

    ssd_ref[0] = p[:, P_SSD:P_SSD + SSD_W]


def _const_spec(shape):
    nd = len(shape)
    return pl.BlockSpec(shape, lambda b, i: (0,) * nd)


def _inproj(h, modsel, gmix, w, consts, lp, tabs):
    bsz, t, _ = h.shape
    nt = t // TM
    row = lambda w_: pl.BlockSpec((1, TM, w_), lambda b, i: (b, i, 0))
    tab = lambda w_: pl.BlockSpec((TM, w_), lambda b, i: (i, 0))
    in_specs = [
        row(D),
        pl.BlockSpec((1, 1, 6, D), lambda b, i: (b, jnp.minimum(i, 1), 0, 0)),
        _const_spec((1, D)), _const_spec((D, P_W)),
        _const_spec((256, 256)), _const_spec((512, 512)), _const_spec((256, 256)),
        _const_spec((1, 256)), _const_spec((1, 256)),
        _const_spec((1, 256)), _const_spec((256, 512)), _const_spec((1, 128)), _const_spec((128, 1024)),
        _const_spec((1, 512)), _const_spec((1, 512)),
        _const_spec((1, 256)), _const_spec((1, 256)),
        tab(512), tab(512), tab(512), tab(256), tab(256), tab(256),
    ]
    col = lambda w_: pl.BlockSpec((1, w_, TM), lambda b, i: (b, 0, i))
    out_shape = [jax.ShapeDtypeStruct((bsz, t, 1024), BF16),
                 jax.ShapeDtypeStruct((bsz, t, 1024), BF16),
                 jax.ShapeDtypeStruct((bsz, HEADS * V_AUG, t), BF16),
                 jax.ShapeDtypeStruct((bsz, t, 512), BF16),
                 jax.ShapeDtypeStruct((bsz, HEADS * V_AUG, t), BF16),
                 jax.ShapeDtypeStruct((bsz, t, SSD_W), F32)]
    out_specs = [row(1024), row(1024), col(HEADS * V_AUG), row(512), col(HEADS * V_AUG), row(SSD_W)]
    return pl.pallas_call(
        _inproj_kernel, grid=(bsz, nt), in_specs=in_specs, out_specs=out_specs, out_shape=out_shape,
        compiler_params=_cparams(2), name="inproj",
    )(h, modsel, gmix, w, consts["bd64"], consts["bd128"], consts["bd32"],
      lp["na_gq"], lp["na_gk"], lp["mla_gqa"], lp["mla_wqb"], lp["mla_gkva"], lp["mla_wkvb"],
      lp["mla_gq"], lp["mla_gk"], lp["diff_gq"], lp["diff_gk"],
      tabs["mc"], tabs["ms1"], tabs["ms2"], tabs["dc"], tabs["ds1"], tabs["ds2"])


def _attend(jobs, s_ref):
    n = len(jobs[0][1])
    total = len(jobs) * n
    ahead = total if n == 1 else n + min(2, n - 1)
    assert n > 1 or total * CK <= s_ref.shape[2]
    m_run = [None] * len(jobs)

    def cols(j, c):
        return (0, j * CK) if n == 1 else (j % 2, c * CK)

    def score(t):
        j, c = divmod(t, n)
        qh, chunks = jobs[j]
        s = _dot_nt(qh, chunks[c][0]())
        if chunks[c][2] is not None:
            s = s + chunks[c][2]()
        slot, c0 = cols(j, c)
        s_ref[slot, :, c0:c0 + CK] = s
        for b in range(CK // 128):
            blk = s[:, b * 128:(b + 1) * 128]
            m_run[j] = blk if m_run[j] is None else jnp.maximum(m_run[j], blk)

    for t in range(min(ahead, total)):
        score(t)
    outs = []
    for j in range(len(jobs)):
        m = jnp.broadcast_to(jnp.max(m_run[j], axis=-1, keepdims=True), (TM, 128))
        acc = None
        for c in range(n):
            slot, c0 = cols(j, c)
            e = jnp.concatenate(
                [jnp.exp2(s_ref[slot, :, c0 + b * 128:c0 + (b + 1) * 128] - m) for b in range(CK // 128)], axis=-1)
            pv = _dot(e.astype(BF16), jobs[j][1][c][1]())
            acc = pv if acc is None else acc + pv
            if j * n + c + ahead < total:
                score(j * n + c + ahead)
        outs.append(acc[:, 0:V_HD] / acc[:, V_HD:V_HD + 1])
    return outs


def _attend_t(jobs, s_ref):
    n = len(jobs[0][1])
    total = len(jobs) * n
    ahead = total if n == 1 else n + min(2, n - 1)
    assert n > 1 or total * CK <= s_ref.shape[1]
    m_run = [None] * len(jobs)

    def rows(j, c):
        return (0, slice(j * CK, (j + 1) * CK)) if n == 1 else (j % 2, slice(c * CK, (c + 1) * CK))

    def score(t):
        j, c = divmod(t, n)
        qh, chunks = jobs[j]
        st = _dot_nt(chunks[c][0](), qh)
        slot, r = rows(j, c)
        s_ref[slot, r, :] = st
        m8 = jnp.max(st.reshape(CK // 8, 8, TM), axis=0)
        m_run[j] = m8 if m_run[j] is None else jnp.maximum(m_run[j], m8)

    for t in range(min(ahead, total)):
        score(t)
    outs = []
    for j in range(len(jobs)):
        m = jnp.max(m_run[j], axis=0, keepdims=True)
        acc = None
        for c in range(n):
            slot, r = rows(j, c)
            et = jnp.exp2(s_ref[slot, r, :] - m).astype(BF16)
            pv = _dot(jobs[j][1][c][1](), et)
            acc = pv if acc is None else acc + pv
            if j * n + c + ahead < total:
                score(j * n + c + ahead)
        outs.append(acc[0:V_HD, :] / acc[V_HD:V_HD + 1, :])
    return outs


def _kvt_chunks(k_ref, vt_ref, k_sl, h, n_chunks):
    return [(lambda c=c: k_ref[0, c * CK:(c + 1) * CK, k_sl],
             lambda c=c: vt_ref[0, h * V_AUG:(h + 1) * V_AUG, c * CK:(c + 1) * CK]) for c in range(n_chunks)]


def _kv_chunks(k_ref, v_ref, k_sl, v_sl, n_chunks, first=0, start=None, bias_fn=None):
    out = []
    for c in range(n_chunks):
        if start is None:
            rows = slice((first + c) * CK, (first + c + 1) * CK)
        else:
            rows = pl.ds(start + c * CK, CK)
        out.append((lambda rows=rows: k_ref[0, rows, k_sl],
                    lambda rows=rows: v_ref[0, rows, v_sl],
                    None if bias_fn is None else functools.partial(bias_fn, c)))
    return out


def _na_kernel(q_ref, k_ref, v_ref, tab_ref, o_ref, s_ref, *, rows):
    i = pl.program_id(1)
    r0 = (i - 1) * NA_R
    s0 = jnp.clip(r0 - NA_KH // 2, 0, rows - NA_W)

    def bias_chunk(h, c):
        lane = lax.broadcasted_iota(jnp.int32, (1, 2 * GRID_W), 1)
        row_blocks = []
        for qr in range(NA_R):
            q_row = r0 + qr
            lo = jnp.clip(q_row - NA_KH // 2, 0, rows - NA_KH)
            pieces = []
            for u in range(CK // (2 * GRID_W)):
                k_row = s0 + c * (CK // GRID_W) + 2 * u
                pen = [jnp.where((k_row + d >= lo) & (k_row + d < lo + NA_KH), 0.0, -1e30) for d in range(2)]
                idx = jnp.clip(k_row - q_row + (NA_KH - 1), -1, 2 * NA_KH - 1) + 1
                pieces.append(tab_ref[h, idx] + jnp.where(lane < GRID_W, pen[0], pen[1]))
            row_blocks.append(jnp.concatenate(pieces, axis=-1))
        return jnp.concatenate(row_blocks, axis=0)

    def run(window_start):
        jobs = []
        for h in range(HEADS):
            k_sl = slice(h * NA_HD, (h + 1) * NA_HD)
            v_sl = slice(h * V_AUG, (h + 1) * V_AUG)
            chunks = _kv_chunks(k_ref, v_ref, k_sl, v_sl, 1)
            if window_start is not None:
                chunks += _kv_chunks(k_ref, v_ref, k_sl, v_sl, NA_W * GRID_W // CK, start=window_start,
                                     bias_fn=functools.partial(bias_chunk, h))
            jobs.append((q_ref[0, :, k_sl], chunks))
        o_ref[0] = jnp.concatenate(_attend(jobs, s_ref), axis=-1).astype(BF16)

    @pl.when(i == 0)
    def _():
        run(None)

    @pl.when(i > 0)
    def _():
        run(pl.multiple_of(N_CTX + s0 * GRID_W, GRID_W))


def _na_attention(na_qkv, table):
    bsz, t, _ = na_qkv.shape
    rows = (t - N_CTX) // GRID_W
    return pl.pallas_call(
        functools.partial(_na_kernel, rows=rows), grid=(bsz, t // TM),
        in_specs=[pl.BlockSpec((1, TM, 256), lambda b, i: (b, i, 0)),
                  pl.BlockSpec((1, t, 256), lambda b, i: (b, 0, 1)),
                  pl.BlockSpec((1, t, 512), lambda b, i: (b, 0, 1)),
                  _const_spec(table.shape)],
        out_specs=pl.BlockSpec((1, TM, 256), lambda b, i: (b, i, 0)),
        out_shape=jax.ShapeDtypeStruct((bsz, t, 256), BF16),
        scratch_shapes=[pltpu.VMEM((2, TM, CK + NA_W * GRID_W), F32)],
        compiler_params=_cparams(2), name="na_attention",
    )(na_qkv, na_qkv, na_qkv, table)


def _na_bias_table(rel_bias):
    cq = np.arange(GRID_W)
    col_lo = np.clip(cq - NA_KW // 2, 0, GRID_W - NA_KW)
    col_ok = (cq[None, :] >= col_lo[:, None]) & (cq[None, :] < col_lo[:, None] + NA_KW)
    col_off = np.clip(cq[None, :] - cq[:, None], 1 - NA_KW, NA_KW - 1) + (NA_KW - 1)
    col_sel = jnp.asarray(col_off[..., None] == np.arange(2 * NA_KW - 1), F32)
    t1 = jnp.einsum("hab,qkb->haqk", rel_bias * LOG2E, col_sel, precision=lax.Precision.HIGHEST)
    t1 = jnp.where(col_ok[None, None], t1, -1e30)
    fill = jnp.full((HEADS, 1, GRID_W, GRID_W), -1e30, F32)
    ext = jnp.concatenate([fill, t1, fill, fill], axis=1)
    return jnp.concatenate([ext[:, :-1], ext[:, 1:]], axis=-1)


def _mla_kernel(q_ref, k_ref, v_ref, o_ref, s_ref):
    i = pl.program_id(1)

    def run(n_chunks):
        jobs = []
        for h in range(HEADS):
            sl = slice(h * MLA_QK_PAD, (h + 1) * MLA_QK_PAD)
            jobs.append((q_ref[0, :, sl], _kvt_chunks(k_ref, v_ref, sl, h, n_chunks)))
        o_ref[0] = jnp.concatenate(_attend_t(jobs, s_ref), axis=0).T.astype(BF16)

    @pl.when(i == 0)
    def _():
        run(1)

    @pl.when(i > 0)
    def _():
        run(k_ref.shape[1] // CK)


def _mla_attention(mla_qk, mla_vt):
    bsz, t, _ = mla_qk.shape
    return pl.pallas_call(
        _mla_kernel, grid=(bsz, t // TM),
        in_specs=[pl.BlockSpec((1, TM, 512), lambda b, i: (b, i, 0)),
                  pl.BlockSpec((1, t, 512), lambda b, i: (b, 0, 1)),
                  pl.BlockSpec((1, HEADS * V_AUG, t), lambda b, i: (b, 0, 0))],
        out_specs=pl.BlockSpec((1, TM, 256), lambda b, i: (b, i, 0)),
        out_shape=jax.ShapeDtypeStruct((bsz, t, 256), BF16),
        scratch_shapes=[pltpu.VMEM((2, t, TM), F32)],
        compiler_params=_cparams(2), name="mla_attention",
    )(mla_qk, mla_qk, mla_vt)


def _diff_kernel(q_ref, k_ref, v_ref, lam_ref, gsub_ref, o_ref, s_ref, *, lam_init):
    i = pl.program_id(1)
    lv = lam_ref[...]
    lam = (jnp.exp(jnp.sum(lv[0:1] * lv[1:2], axis=-1, keepdims=True))
           - jnp.exp(jnp.sum(lv[2:3] * lv[3:4], axis=-1, keepdims=True)) + lam_init)

    def run(n_chunks):
        first = lax.broadcasted_iota(jnp.int32, (TM, 2 * DIFF_QK), 1) < DIFF_QK
        jobs = []
        for h in range(HEADS):
            sl = slice(h * 2 * DIFF_QK, (h + 1) * 2 * DIFF_QK)
            chunks = _kvt_chunks(k_ref, v_ref, sl, h, n_chunks)
            qh = q_ref[0, :, sl]
            zero = jnp.zeros_like(qh)
            jobs += [(jnp.where(first, qh, zero), chunks), (jnp.where(first, zero, qh), chunks)]
        ot = _attend_t(jobs, s_ref)
        outs = []
        for h in range(HEADS):
            d = ot[2 * h] - lam * ot[2 * h + 1]
            ms = jnp.mean(d * d, axis=0, keepdims=True)
            outs.append(d * lax.rsqrt(ms + EPS) * gsub_ref[...] * (1.0 - lam_init))
        o_ref[0] = jnp.concatenate(outs, axis=0).T.astype(BF16)

    @pl.when(i == 0)
    def _():
        run(1)

    @pl.when(i > 0)
    def _():
        run(k_ref.shape[1] // CK)


def _diff_attention(diff_qk, diff_vt, lam_vecs, g_sub, lam_init):
    bsz, t, _ = diff_qk.shape
    g_sub_t = jnp.broadcast_to(g_sub.reshape(DIFF_V, 1), (DIFF_V, TM))
    return pl.pallas_call(
        functools.partial(_diff_kernel, lam_init=lam_init), grid=(bsz, t // TM),
        in_specs=[pl.BlockSpec((1, TM, 256), lambda b, i: (b, i, 0)),
                  pl.BlockSpec((1, t, 256), lambda b, i: (b, 0, 1)),
                  pl.BlockSpec((1, HEADS * V_AUG, t), lambda b, i: (b, 0, 0)),
                  _const_spec((4, DIFF_QK)), _const_spec((DIFF_V, TM))],
        out_specs=pl.BlockSpec((1, TM, 256), lambda b, i: (b, i, 0)),
        out_shape=jax.ShapeDtypeStruct((bsz, t, 256), BF16),
        scratch_shapes=[pltpu.VMEM((2, t, TM), F32)],
        compiler_params=_cparams(2), name="diff_attention",
    )(diff_qk, diff_qk, diff_vt, lam_vecs, g_sub_t)


def _softplus(x):
    return jnp.maximum(x, 0.0) + jnp.log1p(jnp.exp(-jnp.abs(x)))


def _ssd_kernel(raw_ref, convw_ref, convb_ref, dtb_ref, alog_ref, dskip_ref, gnorm_ref,
                tril_ref, triu_ref, mlow_ref, mupp_ref, o_ref, xact_ref, yacc_ref, state_ref):
    t = raw_ref.shape[1]
    nt = t // TM
    xbc0 = SSD_INNER
    dt0 = SSD_INNER + SSD_XBC

    cw = convw_ref[...]
    cb = convb_ref[...]
    for j in range(nt):
        lo = j * TM
        cur = raw_ref[0, lo:lo + TM, xbc0:xbc0 + SSD_XBC]
        zeros8 = jnp.zeros((8, SSD_XBC), F32)
        prev = raw_ref[0, lo - 8:lo, xbc0:xbc0 + SSD_XBC] if j >= 2 else zeros8
        nxt = raw_ref[0, lo + TM:lo + TM + 8, xbc0:xbc0 + SSD_XBC] if 1 <= j < nt - 1 else zeros8
        u = jnp.concatenate([prev, cur, nxt], axis=0)
        acc = cb
        for kk in range(SSD_CONV):
            off = 8 - SSD_CONV // 2 + kk
            acc = acc + cw[kk:kk + 1, :] * u[off:off + TM, :]
        xact_ref[lo:lo + TM, :] = _silu(acc)

    a_pad = -jnp.exp(alog_ref[...])
    for d in range(2):
        tri_ref = tril_ref if d == 0 else triu_ref
        off_ref = mlow_ref if d == 0 else mupp_ref
        state_ref[...] = jnp.zeros_like(state_ref)

        def chunk(c, carry, d=d, tri_ref=tri_ref, off_ref=off_ref):
            if d == 0:
                blk = c
            else:
                blk = jnp.where(c == 0, 0, nt - c)
            off = pl.multiple_of(blk * TM, TM)
            rows = pl.ds(off, TM)
            dt = _softplus(raw_ref[0, rows, dt0:dt0 + 128] + dtb_ref[...])
            la = dt * a_pad
            cum = _split_dot_left(tri_ref[...], la)
            cum_t = cum.T
            dt_t = dt.T
            total = cum[TM - 1:TM, :] if d == 0 else cum[0:1, :]
            e_in = jnp.exp(cum)
            w_t = (jnp.exp(total - cum) * dt).T
            e_tot = jnp.exp(total)
            xs = xact_ref[rows, 0:SSD_INNER]
            xs_b = xs.astype(BF16)
            ys = []
            for g in range(SSD_GROUPS):
                bm = xact_ref[rows, SSD_INNER + g * SSD_STATE:SSD_INNER + (g + 1) * SSD_STATE]
                cm_b = xact_ref[rows, SSD_INNER + (SSD_GROUPS + g) * SSD_STATE:
                                SSD_INNER + (SSD_GROUPS + g + 1) * SSD_STATE].astype(BF16)
                gm = _dot_nt(cm_b, bm.astype(BF16))
                bm_t = bm.T
                for hh in range(HEADS // SSD_GROUPS):
                    h = g * (HEADS // SSD_GROUPS) + hh
                    j = d * HEADS + h
                    x_b = xs_b[:, h * SSD_HD:(h + 1) * SSD_HD]
                    dec = jnp.exp(cum[:, j:j + 1] - cum_t[j:j + 1, :] + off_ref[...])
                    y_d = _dot((gm * dec * dt_t[j:j + 1, :]).astype(BF16), x_b)
                    st = state_ref[h]
                    y_o = e_in[:, j:j + 1] * _dot(cm_b, st.astype(BF16))
                    new = _dot((bm_t * w_t[j:j + 1, :]).astype(BF16), x_b)
                    state_ref[h] = st * e_tot[:, j:j + 1] + new
                    ys.append(y_d + y_o + dskip_ref[d, h] * xs[:, h * SSD_HD:(h + 1) * SSD_HD])
            y = jnp.concatenate(ys, axis=-1)
            if d == 0:
                yacc_ref[rows, :] = y
            else:
                yacc_ref[rows, :] += y
            return carry

        lax.fori_loop(0, nt, chunk, 0)

    for j in range(nt):
        lo = j * TM
        y = yacc_ref[lo:lo + TM, :] * _silu(raw_ref[0, lo:lo + TM, 0:SSD_INNER])
        o_ref[0, lo:lo + TM, :] = _rms(y, gnorm_ref[...]).astype(BF16)


def _ssd(ssd_raw, lp, consts):
    bsz, t, _ = ssd_raw.shape
    c1 = lambda shape: pl.BlockSpec(shape, lambda b: (0,) * len(shape))
    return pl.pallas_call(
        _ssd_kernel, grid=(bsz,),
        in_specs=[pl.BlockSpec((1, t, SSD_W), lambda b: (b, 0, 0)),
                  c1((SSD_CONV, SSD_XBC)), c1((1, SSD_XBC)), c1((1, 128)), c1((1, 128)),
                  pl.BlockSpec(memory_space=pltpu.SMEM),
                  c1((1, SSD_INNER)), c1((TM, TM)), c1((TM, TM)), c1((TM, TM)), c1((TM, TM))],
        out_specs=pl.BlockSpec((1, t, SSD_INNER), lambda b: (b, 0, 0)),
        out_shape=jax.ShapeDtypeStruct((bsz, t, SSD_INNER), BF16),
        scratch_shapes=[pltpu.VMEM((t, SSD_XBC), F32), pltpu.VMEM((t, SSD_INNER), F32),
                        pltpu.VMEM((HEADS, SSD_STATE, SSD_HD), F32)],
        compiler_params=_cparams(1), name="ssd",
    )(ssd_raw, lp["ssd_convw"], lp["ssd_convb"], lp["ssd_dtb"], lp["ssd_alog"], lp["ssd_dskip"],
      lp["ssd_gnorm"], consts["tril"], consts["triu"], consts["mlow"], consts["mupp"])


def _outproj_kernel(na_ref, mla_ref, diff_ref, ssd_ref, h_ref, mod_ref, wout_ref, gffn_ref,
                    wrh_ref, wrl_ref, br_ref, lstrict_ref, ustrict_ref,
                    hout_ref, f_ref, route_ref, cnt_ref, meta_ref, run_ref):
    first = (pl.program_id(0) == 0) & (pl.program_id(1) == 0)

    @pl.when(first)
    def _():
        run_ref[...] = jnp.zeros_like(run_ref)

    o = (_dot(na_ref[0], wout_ref[0:256, :]) + _dot(mla_ref[0], wout_ref[256:512, :])
         + _dot(diff_ref[0], wout_ref[512:768, :]) + _dot(ssd_ref[0], wout_ref[768:1024, :]))
    gate = mod_ref[0, 0, 2:3, :]
    hn = h_ref[0] + gate * o
    hout_ref[0] = hn
    f = _rms(hn, gffn_ref[...]) * (1.0 + mod_ref[0, 0, 4:5, :]) + mod_ref[0, 0, 3:4, :]
    f_ref[0] = f

    f_hi = f.astype(BF16)
    f_lo = (f - f_hi.astype(F32)).astype(BF16)
    logits = _dot(f_hi, wrh_ref[...]) + _dot(f_lo, wrh_ref[...]) + _dot(f_hi, wrl_ref[...]) + br_ref[...]
    lane = lax.broadcasted_iota(jnp.int32, logits.shape, 1)
    lane_f = lane.astype(F32)
    neg = jnp.float32(-jnp.inf)
    big = jnp.float32(1e9)
    gl = jnp.where(lane < MOE_GROUPS, logits, neg)
    gmax = jnp.max(gl, axis=-1, keepdims=True)
    g_top_p = 1.0 / jnp.sum(jnp.exp(gl - gmax), axis=-1, keepdims=True)
    g_top = jnp.min(jnp.where(gl == gmax, lane_f, big), axis=-1, keepdims=True).astype(jnp.int32)
    in_group = (lane >= MOE_GROUPS) & (lane < MOE_GROUPS + MOE_EXPERTS) & (((lane - MOE_GROUPS) // MOE_EPG) == g_top)
    el = jnp.where(in_group, logits, neg)
    m1 = jnp.max(el, axis=-1, keepdims=True)
    i1 = jnp.min(jnp.where(el == m1, lane_f, big), axis=-1, keepdims=True)
    el2 = jnp.where(lane_f == i1, neg, el)
    m2 = jnp.max(el2, axis=-1, keepdims=True)
    i2 = jnp.min(jnp.where(el2 == m2, lane_f, big), axis=-1, keepdims=True)
    x2 = jnp.exp(m2 - m1)
    w1 = g_top_p / (1.0 + x2)
    w2 = g_top_p * x2 / (1.0 + x2)
    e1 = i1 - MOE_GROUPS
    e2 = i2 - MOE_GROUPS

    onehot = ((lane_f == e1) | (lane_f == e2)).astype(F32)
    tile_cnt = jnp.floor((jnp.sum(onehot, axis=0, keepdims=True) + (RUN_ALIGN - 1.0)) * (1.0 / RUN_ALIGN)) * RUN_ALIGN
    tile_start = _dot(jnp.broadcast_to(tile_cnt, (8, 128)).astype(BF16), ustrict_ref[...])[0:1]
    pos = _dot(lstrict_ref[...], onehot.astype(BF16)) + tile_start
    p1 = jnp.sum(jnp.where(lane_f == e1, pos, 0.0), axis=-1, keepdims=True)
    p2 = jnp.sum(jnp.where(lane_f == e2, pos, 0.0), axis=-1, keepdims=True)
    run_old = run_ref[...]
    run_ref[...] = run_old + tile_cnt
    cnt_ref[...] = run_old + tile_cnt
    route = jnp.zeros(logits.shape, F32)
    for idx, val in enumerate((p1, p2, w1, w2)):
        route = jnp.where(lane == idx, val, route)
    route_ref[0] = route
    sub = lax.broadcasted_iota(jnp.int32, (8, 128), 0)
    meta_ref[0] = jnp.where(sub == 0, tile_cnt, jnp.where(sub == 1, tile_start, jnp.where(sub == 2, run_old, 0.0)))


def _outproj(mix, h, modsel, wout, gffn, lp, consts):
    bsz, t, _ = h.shape
    row = lambda w_: pl.BlockSpec((1, TM, w_), lambda b, i: (b, i, 0))
    return pl.pallas_call(
        _outproj_kernel, grid=(bsz, t // TM),
        in_specs=[row(256), row(256), row(256), row(256), row(D),
                  pl.BlockSpec((1, 1, 6, D), lambda b, i: (b, jnp.minimum(i, 1), 0, 0)),
                  _const_spec((D, D)), _const_spec((1, D)), _const_spec((D, 128)), _const_spec((D, 128)),
                  _const_spec((1, 128)), _const_spec((TM, TM)), _const_spec((128, 128))],
        out_specs=[row(D), row(D), row(128), _const_spec((1, 128)),
                   pl.BlockSpec((1, 8, 128), lambda b, i: (b * (t // TM) + i, 0, 0))],
        out_shape=[jax.ShapeDtypeStruct((bsz, t, D), F32), jax.ShapeDtypeStruct((bsz, t, D), F32),
                   jax.ShapeDtypeStruct((bsz, t, 128), F32), jax.ShapeDtypeStruct((1, 128), F32),
                   jax.ShapeDtypeStruct((bsz * (t // TM), 8, 128), F32)],
        scratch_shapes=[pltpu.VMEM((1, 128), F32)],
        compiler_params=_cparams(2), name="outproj_router",
    )(*mix, h, modsel, wout, gffn, lp["wr_hi"], lp["wr_lo"], lp["br"], consts["lstrict"], consts["ustrict"])


def _run_dmas(tile, n_ref, ls_ref, gs_ref, make_copy):
    for wait in (False, True):
        for e in range(MOE_EXPERTS):
            n = n_ref[tile * MOE_EXPERTS + e]
            l0 = ls_ref[tile * MOE_EXPERTS + e]
            g0 = gs_ref[tile * MOE_EXPERTS + e]
            for b in range(RUN_BITS[1] - 1, RUN_BITS[0] - 1, -1):
                off = (n >> (b + 1)) << (b + 1)

                @pl.when(((n >> b) & 1) == 1)
                def _(b=b, off=off, l0=l0, g0=g0, wait=wait):
                    c = make_copy(pl.multiple_of(l0 + off, RUN_ALIGN), pl.multiple_of(g0 + off, RUN_ALIGN), 1 << b)
                    if wait:
                        c.wait()
                    else:
                        c.start(priority=b % 2)


def _dispatch_kernel(n_ref, ls_ref, gs_ref, pe_ref, nu_ref, f_ref, route_ref, xs_ref, sbuf_ref, zbuf_ref,
                     sem, zsem, *, first_tail, n_blocks):
    tile = pl.program_id(0)

    def pad_copies(action):
        for e in range(MOE_EXPERTS):
            prev = pe_ref[e - 1] if e > 0 else 0

            @pl.when(pe_ref[e] > prev)
            def _(e=e):
                start = pl.multiple_of(pe_ref[e] - MOE_MB, MOE_MB)
                action(pltpu.make_async_copy(zbuf_ref, xs_ref.at[pl.ds(start, MOE_MB)], zsem))
        for blk in range(first_tail, n_blocks):
            @pl.when(blk >= nu_ref[0])
            def _(blk=blk):
                action(pltpu.make_async_copy(zbuf_ref, xs_ref.at[pl.ds(blk * MOE_MB, MOE_MB)], zsem))

    @pl.when(tile == 0)
    def _():
        zbuf_ref[...] = jnp.zeros_like(zbuf_ref)
        pad_copies(lambda c: c.start())
        pad_copies(lambda c: c.wait())

    pos = route_ref[...].T
    slot = lax.broadcasted_iota(jnp.int32, (SORT_ROWS, TM), 0).astype(F32)
    perm = ((slot == pos[0:1, :]) | (slot == pos[1:2, :])).astype(BF16)
    sbuf_ref[...] = _dot(perm, f_ref[...].astype(BF16))

    _run_dmas(tile, n_ref, ls_ref, gs_ref,
              lambda l, g, size: pltpu.make_async_copy(sbuf_ref.at[pl.ds(l, size)], xs_ref.at[pl.ds(g, size)], sem))


def _dispatch(plan, f2d, route2d, cap):
    n_tok = f2d.shape[0]
    n_blocks = cap // MOE_MB
    first_tail = 2 * n_tok // MOE_MB
    assert 2 * TM + MOE_EXPERTS * (RUN_ALIGN - 1) <= SORT_ROWS
    return pl.pallas_call(
        functools.partial(_dispatch_kernel, first_tail=first_tail, n_blocks=n_blocks),
        grid_spec=pltpu.PrefetchScalarGridSpec(
            num_scalar_prefetch=5, grid=(n_tok // TM,),
            in_specs=[pl.BlockSpec((TM, D), lambda i, *_: (i, 0)),
                      pl.BlockSpec((TM, 128), lambda i, *_: (i, 0))],
            out_specs=pl.BlockSpec(memory_space=pl.ANY),
            scratch_shapes=[pltpu.VMEM((SORT_ROWS, D), F32), pltpu.VMEM((MOE_MB, D), F32),
                            pltpu.SemaphoreType.DMA(()), pltpu.SemaphoreType.DMA(())]),
        out_shape=jax.ShapeDtypeStruct((cap, D), F32),
        compiler_params=_cparams(1), name="moe_dispatch",
    )(plan["n"], plan["ls"], plan["gs"], plan["pad_end"], plan["n_used"], f2d, route2d)


def _experts_kernel(be_ref, nb_ref, x_ref, wg_ref, wu_ref, wd_ref, y_ref, wgu_s, wd_s):
    i = pl.program_id(0)

    @pl.when((i == 0) | (be_ref[i] != be_ref[jnp.maximum(i - 1, 0)]))
    def _():
        wgu_s[:, 0:MOE_FF] = wg_ref[0, 0].astype(BF16)
        wgu_s[:, MOE_FF:2 * MOE_FF] = wu_ref[0, 0].astype(BF16)
        wd_s[...] = wd_ref[0, 0].astype(BF16)

    @pl.when(i < nb_ref[0])
    def _():
        gu = _dot(x_ref[...].astype(BF16), wgu_s[...])
        a = _silu(gu[:, 0:MOE_FF]) * gu[:, MOE_FF:2 * MOE_FF]
        y_ref[...] = _dot(a.astype(BF16), wd_s[...])

    @pl.when(i >= nb_ref[0])
    def _():
        y_ref[...] = jnp.zeros_like(y_ref)


def _experts(plan, xs, w_gate, w_up, w_down, layer):
    cap = xs.shape[0]
    wspec = lambda a, b: pl.BlockSpec((1, 1, a, b), lambda i, be, nb: (layer, be[i], 0, 0))
    return pl.pallas_call(
        _experts_kernel,
        grid_spec=pltpu.PrefetchScalarGridSpec(
            num_scalar_prefetch=2, grid=(cap // MOE_MB,),
            in_specs=[pl.BlockSpec((MOE_MB, D), lambda i, be, nb: (i, 0)),
                      wspec(D, MOE_FF), wspec(D, MOE_FF), wspec(MOE_FF, D)],
            out_specs=pl.BlockSpec((MOE_MB, D), lambda i, be, nb: (i, 0)),
            scratch_shapes=[pltpu.VMEM((D, 2 * MOE_FF), BF16), pltpu.VMEM((MOE_FF, D), BF16)]),
        out_shape=jax.ShapeDtypeStruct((cap, D), F32),
        compiler_params=_cparams(1), name="moe_experts",
    )(plan["block_e"], plan["n_used"], xs, w_gate, w_up, w_down)


def _combine_kernel(n_ref, ls_ref, gs_ref, h_ref, mod_ref, route_ref, y_ref, o_ref, ybuf_ref, sem, *, skip, nt):
    tile = pl.program_id(0) * nt + pl.program_id(1) + skip

    @pl.when((pl.program_id(0) == 0) & (pl.program_id(1) == 0))
    def _():
        ybuf_ref[...] = jnp.zeros_like(ybuf_ref)

    _run_dmas(tile, n_ref, ls_ref, gs_ref,
              lambda l, g, size: pltpu.make_async_copy(y_ref.at[pl.ds(g, size)], ybuf_ref.at[pl.ds(l, size)], sem))

    slot = lax.broadcasted_iota(jnp.int32, (TM, SORT_ROWS), 1).astype(F32)
    r = route_ref[0]
    wm = (jnp.where(slot == r[:, 0:1], r[:, 2:3], 0.0) + jnp.where(slot == r[:, 1:2], r[:, 3:4], 0.0))
    w_hi = wm.astype(BF16)
    w_lo = (wm - w_hi.astype(F32)).astype(BF16)
    yv = ybuf_ref[...]
    y_hi = yv.astype(BF16)
    y_lo = (yv - y_hi.astype(F32)).astype(BF16)
    out = _dot(w_hi, y_hi) + _dot(w_lo, y_hi) + _dot(w_hi, y_lo)
    o_ref[0] = h_ref[0] + mod_ref[0, 0, 5:6, :] * out


def _combine(plan, h, modsel, route, y, latent_only):
    bsz, t, _ = h.shape
    nt = t // TM
    skip = 1 if latent_only else 0
    row = lambda w_: pl.BlockSpec((1, TM, w_), lambda b, i, *_: (b, i + skip, 0))
    return pl.pallas_call(
        functools.partial(_combine_kernel, skip=skip, nt=nt),
        grid_spec=pltpu.PrefetchScalarGridSpec(
            num_scalar_prefetch=3, grid=(bsz, nt - skip),
            in_specs=[row(D),
                      pl.BlockSpec((1, 1, 6, D), lambda b, i, *_: (b, jnp.minimum(i + skip, 1), 0, 0)),
                      row(128),
                      pl.BlockSpec(memory_space=pl.ANY)],
            out_specs=pl.BlockSpec((1, TM, D), lambda b, i, *_: (b, i, 0)),
            scratch_shapes=[pltpu.VMEM((SORT_ROWS, D), F32), pltpu.SemaphoreType.DMA(())]),
        out_shape=jax.ShapeDtypeStruct((bsz, t - skip * TM, D), F32),
        compiler_params=_cparams(2), name="moe_combine",
    )(plan["n"], plan["ls"], plan["gs"], h, modsel, route, y)


def _moe_plan(meta, counts, n_blocks):
    cnt = counts[0, :MOE_EXPERTS].astype(jnp.int32)
    padded = (cnt + MOE_MB - 1) // MOE_MB * MOE_MB
    pad_end = jnp.cumsum(padded)
    pad_start = pad_end - padded
    m = meta[:, 0:3, 0:MOE_EXPERTS].astype(jnp.int32)
    blk0 = jnp.arange(n_blocks, dtype=jnp.int32) * MOE_MB
    block_e = jnp.minimum(jnp.sum(blk0[:, None] >= pad_end[None, :], axis=-1), MOE_EXPERTS - 1)
    return {"n": m[:, 0].reshape(-1), "ls": m[:, 1].reshape(-1), "gs": (pad_start[None, :] + m[:, 2]).reshape(-1),
            "pad_end": pad_end.astype(jnp.int32), "n_used": (pad_end[-1:] // MOE_MB).astype(jnp.int32),
            "block_e": block_e.astype(jnp.int32)}


def _block_diag(n, seg):
    idx = np.arange(n) // seg
    return jnp.asarray(idx[:, None] == idx[None, :], BF16)


def _constants():
    lower = np.tril(np.ones((TM, TM), np.float32))
    upper = np.triu(np.ones((TM, TM), np.float32))
    return {"bd64": _block_diag(256, 64), "bd128": _block_diag(512, 128), "bd32": _block_diag(256, 32),
            "tril": jnp.asarray(lower, BF16), "triu": jnp.asarray(upper, BF16),
            "mlow": jnp.asarray((lower - 1.0) * 1e30, F32), "mupp": jnp.asarray((upper - 1.0) * 1e30, F32),
            "lstrict": jnp.asarray(np.tril(np.ones((TM, TM)), -1), BF16),
            "ustrict": jnp.asarray(np.triu(np.ones((128, 128)), 1), BF16)}


def _rope_tables(n_lat, t):
    n_freq = 8
    inv = jnp.power(10000.0, -jnp.arange(n_freq, dtype=F32) / n_freq)
    tok = jnp.arange(n_lat, dtype=jnp.int32)
    row = (tok // GRID_W).astype(F32)
    col = (tok % GRID_W).astype(F32)
    ang = jnp.concatenate([row[:, None] * inv, col[:, None] * inv], axis=-1)
    n_c = t - n_lat
    cos = jnp.concatenate([jnp.ones((n_c, 16), F32), jnp.cos(ang)], axis=0)
    sin = jnp.concatenate([jnp.zeros((n_c, 16), F32), jnp.sin(ang)], axis=0)
    z16 = jnp.zeros((t, 16), F32)
    one = lambda w_: jnp.ones((t, w_), F32)
    zero = lambda w_: jnp.zeros((t, w_), F32)
    mc = jnp.concatenate([one(64), cos, cos, one(32)], axis=-1)
    ms1 = jnp.concatenate([zero(64), -sin, z16, zero(32)], axis=-1)
    ms2 = jnp.concatenate([zero(64), z16, sin, zero(32)], axis=-1)
    dc = jnp.concatenate([cos, cos], axis=-1)
    ds1 = jnp.concatenate([-sin, z16], axis=-1)
    ds2 = jnp.concatenate([z16, sin], axis=-1)
    tile = lambda a, n: jnp.tile(a, (1, n))
    return {"mc": tile(mc, 4), "ms1": tile(ms1, 4), "ms2": tile(ms2, 4),
            "dc": tile(dc, 8), "ds1": tile(ds1, 8), "ds2": tile(ds2, 8)}


def _pad_heads(w, width, padded):
    lead = w.shape[:-1]
    w = w.reshape(lead + (HEADS, width))
    w = jnp.pad(w, [(0, 0)] * len(lead) + [(0, 0), (0, padded - width)])
    return w.reshape(lead + (HEADS * padded,))


def _pack_w_in(w_in):
    n_layers = w_in.shape[0]
    z = lambda n: jnp.zeros((n_layers, D, n), w_in.dtype)
    na = w_in[:, :, 0:768]
    mla = w_in[:, :, 768:1184]
    diff = w_in[:, :, 1184:1952]
    ssd = w_in[:, :, 1952:2984]
    na_p = jnp.concatenate([na[:, :, 0:512], _pad_heads(na[:, :, 512:768], V_HD, V_AUG)], axis=-1)
    mla_p = jnp.concatenate([mla[:, :, 0:384], z(64), mla[:, :, 384:416], z(32)], axis=-1)
    diff_p = jnp.concatenate([diff[:, :, 0:512], _pad_heads(diff[:, :, 512:768], V_HD, V_AUG)], axis=-1)
    ssd_p = jnp.concatenate([ssd, z(SSD_W - 1032)], axis=-1)
    return jnp.concatenate([na_p, mla_p, diff_p, ssd_p], axis=-1).astype(BF16)


def _layer_params(l, p):
    row = lambda a: a.reshape(1, -1)
    t4 = lambda a: jnp.tile(a.reshape(1, -1), (1, HEADS))
    wqb = _pad_heads(p["mla_w_qb"][l], MLA_QK, MLA_QK_PAD)
    wkvb = p["mla_w_kvb"][l].reshape(MLA_KV_RANK, HEADS, MLA_NOPE + MLA_V)
    wkvb = jnp.concatenate([_pad_heads(wkvb[:, :, :MLA_NOPE].reshape(MLA_KV_RANK, -1), MLA_NOPE, MLA_QK_PAD),
                            _pad_heads(wkvb[:, :, MLA_NOPE:].reshape(MLA_KV_RANK, -1), MLA_V, V_AUG)], axis=-1)
    gpad = lambda g: jnp.tile(jnp.pad(g, (0, MLA_QK_PAD - MLA_QK)).reshape(1, -1), (1, HEADS))
    lane8 = lambda a: jnp.pad(a.reshape(1, -1), ((0, 0), (0, 128 - 2 * HEADS)))
    wr = jnp.concatenate([p["moe_w_group"][l], p["moe_w_expert"][l],
                          jnp.zeros((D, 128 - MOE_GROUPS - MOE_EXPERTS), F32)], axis=-1)
    wr_hi = wr.astype(BF16)
    br = jnp.concatenate([p["moe_b_group"][l], p["moe_b_expert"][l],
                          jnp.zeros((128 - MOE_GROUPS - MOE_EXPERTS,), F32)]).reshape(1, 128)
    return {
        "g_mix": row(p["g_mix"][l]), "g_ffn": row(p["g_ffn"][l]),
        "na_gq": t4(p["na_g_q"][l]), "na_gk": t4(p["na_g_k"][l]),
        "mla_gqa": row(p["mla_g_qa"][l]), "mla_wqb": wqb.astype(BF16),
        "mla_gkva": row(p["mla_g_kva"][l]), "mla_wkvb": wkvb.astype(BF16),
        "mla_gq": gpad(p["mla_g_q"][l]), "mla_gk": gpad(p["mla_g_k"][l]),
        "diff_gq": jnp.tile(p["diff_g_q"][l].reshape(1, -1), (1, 8)),
        "diff_gk": jnp.tile(p["diff_g_k"][l].reshape(1, -1), (1, 8)),
        "diff_lam": p["diff_lambda"][l], "diff_gsub": row(p["diff_g_sub"][l]),
        "ssd_convw": p["ssd_conv_w"][l], "ssd_convb": row(p["ssd_conv_b"][l]),
        "ssd_dtb": lane8(p["ssd_dt_bias"][l]), "ssd_alog": lane8(p["ssd_a_log"][l]),
        "ssd_dskip": p["ssd_d"][l], "ssd_gnorm": row(p["ssd_g_norm"][l]),
        "wr_hi": wr_hi, "wr_lo": (wr - wr_hi.astype(F32)).astype(BF16), "br": br,
    }


def _mixers(h, modsel, w_in_l, lp, consts, tabs, bias, lam_init):
    na_qkv, mla_qk, mla_vt, diff_qk, diff_vt, ssd_raw = _inproj(h, modsel, lp["g_mix"], w_in_l, consts, lp, tabs)
    return (_na_attention(na_qkv, bias),
            _mla_attention(mla_qk, mla_vt),
            _diff_attention(diff_qk, diff_vt, lp["diff_lam"], lp["diff_gsub"], lam_init),
            _ssd(ssd_raw, lp, consts))


def _moe(hn, f, route, counts, meta, modsel, w_gate, w_up, w_down, layer, latent_only):
    bsz, t, _ = hn.shape
    n_asg = bsz * t * 2
    assert n_asg % MOE_MB == 0
    n_slots = n_asg + (bsz * t // TM) * MOE_EXPERTS * (RUN_ALIGN - 1)
    n_blocks = -(-n_slots // MOE_MB) + MOE_EXPERTS
    plan = _moe_plan(meta, counts, n_blocks)
    xs = _dispatch(plan, f.reshape(bsz * t, D), route.reshape(bsz * t, 128), n_blocks * MOE_MB)
    y = _experts(plan, xs, w_gate, w_up, w_down, layer)
    return _combine(plan, hn, modsel, route, y, latent_only)


def kernel(x, c, ctx, c_ctx, w_mod, b_mod, g_mix, w_in, w_out, na_g_q, na_g_k, na_rel_bias,
           mla_g_qa, mla_w_qb, mla_g_kva, mla_w_kvb, mla_g_q, mla_g_k,
           diff_g_q, diff_g_k, diff_lambda, diff_g_sub,
           ssd_conv_w, ssd_conv_b, ssd_dt_bias, ssd_a_log, ssd_d, ssd_g_norm,
           g_ffn, moe_w_group, moe_b_group, moe_w_expert, moe_b_expert, moe_w_gate, moe_w_up, moe_w_down):
    p = dict(g_mix=g_mix, g_ffn=g_ffn, na_g_q=na_g_q, na_g_k=na_g_k,
             mla_g_qa=mla_g_qa, mla_w_qb=mla_w_qb, mla_g_kva=mla_g_kva, mla_w_kvb=mla_w_kvb,
             mla_g_q=mla_g_q, mla_g_k=mla_g_k, diff_g_q=diff_g_q, diff_g_k=diff_g_k,
             diff_lambda=diff_lambda, diff_g_sub=diff_g_sub,
             ssd_conv_w=ssd_conv_w, ssd_conv_b=ssd_conv_b, ssd_dt_bias=ssd_dt_bias, ssd_a_log=ssd_a_log,
             ssd_d=ssd_d, ssd_g_norm=ssd_g_norm, moe_w_group=moe_w_group, moe_b_group=moe_b_group,
             moe_w_expert=moe_w_expert, moe_b_expert=moe_b_expert)
    bsz, n_lat, _ = x.shape
    n_ctx = ctx.shape[1]
    assert n_ctx == N_CTX == TM == CK and n_lat % TM == 0 and bsz < 16
    t = n_ctx + n_lat
    n_layers = w_mod.shape[0]
    rows = n_lat // GRID_W
    assert rows % NA_R == 0 and rows >= NA_W and (NA_W * GRID_W) % CK == 0

    consts = _constants()
    tabs = _rope_tables(n_lat, t)
    w_in_p = _pack_w_in(w_in)
    w_out_b = w_out.astype(BF16)

    cvec = jnp.concatenate([c, c_ctx[None, :], jnp.zeros((16 - bsz - 1, D), F32)], axis=0)
    mod = _modulation(cvec, w_mod, b_mod).reshape(n_layers, 16, 6, D)

    h = jnp.concatenate([ctx, x], axis=1)
    for l in range(n_layers):
        lp = _layer_params(l, p)
        modsel = jnp.stack([jnp.broadcast_to(mod[l, bsz][None], (bsz, 6, D)), mod[l, :bsz]], axis=1)
        lam_init = 0.8 - 0.6 * math.exp(-0.3 * l)
        bias = _na_bias_table(na_rel_bias[l])
        mix = _mixers(h, modsel, w_in_p[l], lp, consts, tabs, bias, lam_init)
        hn, f, route, counts, meta = _outproj(mix, h, modsel, w_out_b[l], lp["g_ffn"], lp, consts)
        h = _moe(hn, f, route, counts, meta, modsel, moe_w_gate, moe_w_up, moe_w_down, l,
                 latent_only=(l == n_layers - 1))
    return h
```

```python
import functools
import math

import numpy as np
import jax
import jax.numpy as jnp
from jax import lax
from jax.experimental import pallas as pl
from jax.experimental.pallas import tpu as pltpu

F32 = jnp.float32
BF16 = jnp.bfloat16

D = 1024
GRID_W = 64
N_CTX = 256
HEADS = 4
NA_HD = 64
NA_KH = 8
NA_KW = 16
MLA_NOPE = 64
MLA_ROPE = 32
MLA_QK = MLA_NOPE + MLA_ROPE
MLA_QK_PAD = 128
MLA_V = 64
MLA_Q_RANK = 256
MLA_KV_RANK = 128
DIFF_QK = 32
DIFF_V = 64
SSD_INNER = 256
SSD_HD = 64
SSD_STATE = 128
SSD_GROUPS = 2
SSD_CONV = 5
SSD_XBC = 768
MOE_GROUPS = 4
MOE_EPG = 4
MOE_EXPERTS = 16
MOE_FF = 512
EPS = 1e-6

TM = 256
MOE_MB = 256
NA_R = 4
NA_W = 12
V_HD = 64
V_AUG = 128
CK = 256
LOG2E = math.log2(math.e)
RUN_ALIGN = 16
RUN_BITS = (4, 9)
SORT_ROWS = 2 * TM + 256

P_NA = 0
P_MLA = 1024
P_DIFF = 1536
P_SSD = 2560
P_W = 3712
SSD_W = 1152

VMEM_LIMIT = 56 * 1024 * 1024


def _cparams(n_axes):
    return pltpu.CompilerParams(dimension_semantics=("arbitrary",) * n_axes,
                                vmem_limit_bytes=VMEM_LIMIT)


def _dot(a, b):
    return jnp.dot(a, b, preferred_element_type=F32)


def _dot_nt(a, b):
    return lax.dot_general(a, b, (((1,), (1,)), ((), ())), preferred_element_type=F32)


def _dot_tn(a, b):
    return lax.dot_general(a, b, (((0,), (0,)), ((), ())), preferred_element_type=F32)


def _split3(x):
    hi = x.astype(BF16)
    r1 = x - hi.astype(F32)
    mid = r1.astype(BF16)
    lo = (r1 - mid.astype(F32)).astype(BF16)
    return hi, mid, lo


def _split_dot(x, m):
    hi, mid, lo = _split3(x)
    return _dot(hi, m) + _dot(mid, m) + _dot(lo, m)


def _split_dot_left(m, x):
    hi, mid, lo = _split3(x)
    return _dot(m, hi) + _dot(m, mid) + _dot(m, lo)


def _rms(x, g):
    ms = jnp.mean(x * x, axis=-1, keepdims=True)
    return x * lax.rsqrt(ms + EPS) * g


def _seg_rms(x, bd, inv_n, g):
    x2 = x * x
    hi = x2.astype(BF16)
    lo = (x2 - hi.astype(F32)).astype(BF16)
    ms = (_dot(hi, bd) + _dot(lo, bd)) * inv_n
    return x * lax.rsqrt(ms + EPS) * g


def _silu(x):
    return x * jax.nn.sigmoid(x)


def _rope(x, c, s1, s2, width):
    rot = 16
    return x * c + pltpu.roll(x, width - rot, 1) * s1 + pltpu.roll(x, rot, 1) * s2


def _with_ones(v):
    lane = lax.broadcasted_iota(jnp.int32, v.shape, 1)
    return jnp.where(lane % V_AUG == V_HD, 1.0, v)


def _mod_kernel(c_ref, w_ref, b_ref, o_ref):
    s = _silu(c_ref[...])
    o_ref[0] = _dot(s.astype(BF16), w_ref[0].astype(BF16)) + b_ref[0]


def _modulation(cvec, w_mod, b_mod):
    n_layers = w_mod.shape[0]
    tn = 1536
    return pl.pallas_call(
        _mod_kernel,
        grid=(n_layers, 6 * D // tn),
        in_specs=[pl.BlockSpec((16, D), lambda l, j: (0, 0)),
                  pl.BlockSpec((1, D, tn), lambda l, j: (l, 0, j)),
                  pl.BlockSpec((1, 1, tn), lambda l, j: (l, 0, j))],
        out_specs=pl.BlockSpec((1, 16, tn), lambda l, j: (l, 0, j)),
        out_shape=jax.ShapeDtypeStruct((n_layers, 16, 6 * D), F32),
        compiler_params=_cparams(2),
        name="modulation",
    )(cvec, w_mod, b_mod.reshape(n_layers, 1, 6 * D))


def _inproj_kernel(h_ref, mod_ref, gmix_ref, w_ref, bd64_ref, bd128_ref, bd32_ref,
                   nagq_ref, nagk_ref, gqa_ref, wqb_ref, gkva_ref, wkvb_ref, mgq_ref, mgk_ref,
                   dgq_ref, dgk_ref, mc_ref, ms1_ref, ms2_ref, dc_ref, ds1_ref, ds2_ref,
                   na_ref, mla_ref, mlavt_ref, diff_ref, diffvt_ref, ssd_ref):
    x = h_ref[0]
    shift = mod_ref[0, 0, 0:1, :]
    scale = mod_ref[0, 0, 1:2, :]
    a = _rms(x, gmix_ref[...]) * (1.0 + scale) + shift
    p = _dot(a.astype(BF16), w_ref[...])


    bd64 = bd64_ref[...]
    q = p[:, P_NA:P_NA + 256]
    k = p[:, P_NA + 256:P_NA + 512]
    na_ref[0, :, 0:256] = (_seg_rms(q, bd64, 1.0 / NA_HD, nagq_ref[...]) * (NA_HD ** -0.5 * LOG2E)).astype(BF16)
    na_ref[0, :, 256:512] = _seg_rms(k, bd64, 1.0 / NA_HD, nagk_ref[...]).astype(BF16)
    na_ref[0, :, 512:1024] = _with_ones(p[:, P_NA + 512:P_NA + 1024]).astype(BF16)

    bd128 = bd128_ref[...]
    cq = p[:, P_MLA:P_MLA + 256]
    ckv = p[:, P_MLA + 256:P_MLA + 384]
    kr = p[:, P_MLA + 384:P_MLA + 512]
    q2 = _dot(_rms(cq, gqa_ref[...]).astype(BF16), wqb_ref[...])
    kv = _dot(_rms(ckv, gkva_ref[...]).astype(BF16), wkvb_ref[...])
    k2 = kv[:, 0:512] + jnp.concatenate([kr] * HEADS, axis=-1)
    mc, ms1, ms2 = mc_ref[...], ms1_ref[...], ms2_ref[...]
    qn = _rope(_seg_rms(q2, bd128, 1.0 / MLA_QK, mgq_ref[...]), mc, ms1, ms2, 512)
    kn = _rope(_seg_rms(k2, bd128, 1.0 / MLA_QK, mgk_ref[...]), mc, ms1, ms2, 512)
    mla_ref[0, :, 0:512] = (qn * (MLA_QK ** -0.5 * LOG2E)).astype(BF16)
    mla_ref[0, :, 512:1024] = kn.astype(BF16)
    mlavt_ref[0] = _with_ones(kv[:, 512:1024]).T.astype(BF16)

    bd32 = bd32_ref[...]
    dc, ds1, ds2 = dc_ref[...], ds1_ref[...], ds2_ref[...]
    dq = p[:, P_DIFF:P_DIFF + 256]
    dk = p[:, P_DIFF + 256:P_DIFF + 512]
    dqn = _rope(_seg_rms(dq, bd32, 1.0 / DIFF_QK, dgq_ref[...]), dc, ds1, ds2, 256)
    dkn = _rope(_seg_rms(dk, bd32, 1.0 / DIFF_QK, dgk_ref[...]), dc, ds1, ds2, 256)
    diff_ref[0, :, 0:256] = (dqn * (DIFF_QK ** -0.5 * LOG2E)).astype(BF16)
    diff_ref[0, :, 256:512] = dkn.astype(BF16)
    diffvt_ref[0] = _with_ones(p[:, P_DIFF + 512:P_DIFF + 1024]).T.astype(BF16)

    ssd_ref[0] = p[:, P_SSD:P_SSD + SSD_W]


def _const_spec(shape):
    nd = len(shape)
    return pl.BlockSpec(shape, lambda b, i: (0,) * nd)


def _inproj(h, modsel, gmix, w, consts, lp, tabs):
    bsz, t, _ = h.shape
    nt = t // TM
    row = lambda w_: pl.BlockSpec((1, TM, w_), lambda b, i: (b, i, 0))
    tab = lambda w_: pl.BlockSpec((TM, w_), lambda b, i: (i, 0))
    in_specs = [
        row(D),
        pl.BlockSpec((1, 1, 6, D), lambda b, i: (b, jnp.minimum(i, 1), 0, 0)),
        _const_spec((1, D)), _const_spec((D, P_W)),
        _const_spec((256, 256)), _const_spec((512, 512)), _const_spec((256, 256)),
        _const_spec((1, 256)), _const_spec((1, 256)),
        _const_spec((1, 256)), _const_spec((256, 512)), _const_spec((1, 128)), _const_spec((128, 1024)),
        _const_spec((1, 512)), _const_spec((1, 512)),
        _const_spec((1, 256)), _const_spec((1, 256)),
        tab(512), tab(512), tab(512), tab(256), tab(256), tab(256),
    ]
    col = lambda w_: pl.BlockSpec((1, w_, TM), lambda b, i: (b, 0, i))
    out_shape = [jax.ShapeDtypeStruct((bsz, t, 1024), BF16),
                 jax.ShapeDtypeStruct((bsz, t, 1024), BF16),
                 jax.ShapeDtypeStruct((bsz, HEADS * V_AUG, t), BF16),
                 jax.ShapeDtypeStruct((bsz, t, 512), BF16),
                 jax.ShapeDtypeStruct((bsz, HEADS * V_AUG, t), BF16),
                 jax.ShapeDtypeStruct((bsz, t, SSD_W), F32)]
    out_specs = [row(1024), row(1024), col(HEADS * V_AUG), row(512), col(HEADS * V_AUG), row(SSD_W)]
    return pl.pallas_call(
        _inproj_kernel, grid=(bsz, nt), in_specs=in_specs, out_specs=out_specs, out_shape=out_shape,
        compiler_params=_cparams(2), name="inproj",
    )(h, modsel, gmix, w, consts["bd64"], consts["bd128"], consts["bd32"],
      lp["na_gq"], lp["na_gk"], lp["mla_gqa"], lp["mla_wqb"], lp["mla_gkva"], lp["mla_wkvb"],
      lp["mla_gq"], lp["mla_gk"], lp["diff_gq"], lp["diff_gk"],
      tabs["mc"], tabs["ms1"], tabs["ms2"], tabs["dc"], tabs["ds1"], tabs["ds2"])


def _attend(jobs, s_ref):
    n = len(jobs[0][1])
    total = len(jobs) * n
    ahead = total if n == 1 else n + min(2, n - 1)
    assert n > 1 or total * CK <= s_ref.shape[2]
    m_run = [None] * len(jobs)

    def cols(j, c):
        return (0, j * CK) if n == 1 else (j % 2, c * CK)

    def score(t):
        j, c = divmod(t, n)
        qh, chunks = jobs[j]
        s = _dot_nt(qh, chunks[c][0]())
        if chunks[c][2] is not None:
            s = s + chunks[c][2]()
        slot, c0 = cols(j, c)
        s_ref[slot, :, c0:c0 + CK] = s
        for b in range(CK // 128):
            blk = s[:, b * 128:(b + 1) * 128]
            m_run[j] = blk if m_run[j] is None else jnp.maximum(m_run[j], blk)

    for t in range(min(ahead, total)):
        score(t)
    outs = []
    for j in range(len(jobs)):
        m = jnp.broadcast_to(jnp.max(m_run[j], axis=-1, keepdims=True), (TM, 128))
        acc = None
        for c in range(n):
            slot, c0 = cols(j, c)
            e = jnp.concatenate(
                [jnp.exp2(s_ref[slot, :, c0 + b * 128:c0 + (b + 1) * 128] - m) for b in range(CK // 128)], axis=-1)
            pv = _dot(e.astype(BF16), jobs[j][1][c][1]())
            acc = pv if acc is None else acc + pv
            if j * n + c + ahead < total:
                score(j * n + c + ahead)
        outs.append(acc[:, 0:V_HD] / acc[:, V_HD:V_HD + 1])
    return outs


def _attend_t(jobs, s_ref):
    n = len(jobs[0][1])
    total = len(jobs) * n
    ahead = total if n == 1 else n + min(2, n - 1)
    assert n > 1 or total * CK <= s_ref.shape[1]
    m_run = [None] * len(jobs)

    def rows(j, c):
        return (0, slice(j * CK, (j + 1) * CK)) if n == 1 else (j % 2, slice(c * CK, (c + 1) * CK))

    def score(t):
        j, c = divmod(t, n)
        qh, chunks = jobs[j]
        st = _dot_nt(chunks[c][0](), qh)
        slot, r = rows(j, c)
        s_ref[slot, r, :] = st
        m8 = jnp.max(st.reshape(CK // 8, 8, TM), axis=0)
        m_run[j] = m8 if m_run[j] is None else jnp.maximum(m_run[j], m8)

    for t in range(min(ahead, total)):
        score(t)
    outs = []
    for j in range(len(jobs)):
        m = jnp.max(m_run[j], axis=0, keepdims=True)
        acc = None
        for c in range(n):
            slot, r = rows(j, c)
            et = jnp.exp2(s_ref[slot, r, :] - m).astype(BF16)
            pv = _dot(jobs[j][1][c][1](), et)
            acc = pv if acc is None else acc + pv
            if j * n + c + ahead < total:
                score(j * n + c + ahead)
        outs.append(acc[0:V_HD, :] / acc[V_HD:V_HD + 1, :])
    return outs


def _kvt_chunks(k_ref, vt_ref, k_sl, h, n_chunks):
    return [(lambda c=c: k_ref[0, c * CK:(c + 1) * CK, k_sl],
             lambda c=c: vt_ref[0, h * V_AUG:(h + 1) * V_AUG, c * CK:(c + 1) * CK]) for c in range(n_chunks)]


def _kv_chunks(k_ref, v_ref, k_sl, v_sl, n_chunks, first=0, start=None, bias_fn=None):
    out = []
    for c in range(n_chunks):
        if start is None:
            rows = slice((first + c) * CK, (first + c + 1) * CK)
        else:
            rows = pl.ds(start + c * CK, CK)
        out.append((lambda rows=rows: k_ref[0, rows, k_sl],
                    lambda rows=rows: v_ref[0, rows, v_sl],
                    None if bias_fn is None else functools.partial(bias_fn, c)))
    return out


def _na_kernel(q_ref, k_ref, v_ref, tab_ref, o_ref, s_ref, *, rows):
    i = pl.program_id(1)
    r0 = (i - 1) * NA_R
    s0 = jnp.clip(r0 - NA_KH // 2, 0, rows - NA_W)

    def bias_chunk(h, c):
        lane = lax.broadcasted_iota(jnp.int32, (1, 2 * GRID_W), 1)
        row_blocks = []
        for qr in range(NA_R):
            q_row = r0 + qr
            lo = jnp.clip(q_row - NA_KH // 2, 0, rows - NA_KH)
            pieces = []
            for u in range(CK // (2 * GRID_W)):
                k_row = s0 + c * (CK // GRID_W) + 2 * u
                pen = [jnp.where((k_row + d >= lo) & (k_row + d < lo + NA_KH), 0.0, -1e30) for d in range(2)]
                idx = jnp.clip(k_row - q_row + (NA_KH - 1), -1, 2 * NA_KH - 1) + 1
                pieces.append(tab_ref[h, idx] + jnp.where(lane < GRID_W, pen[0], pen[1]))
            row_blocks.append(jnp.concatenate(pieces, axis=-1))
        return jnp.concatenate(row_blocks, axis=0)

    def run(window_start):
        jobs = []
        for h in range(HEADS):
            k_sl = slice(h * NA_HD, (h + 1) * NA_HD)
            v_sl = slice(h * V_AUG, (h + 1) * V_AUG)
            chunks = _kv_chunks(k_ref, v_ref, k_sl, v_sl, 1)
            if window_start is not None:
                chunks += _kv_chunks(k_ref, v_ref, k_sl, v_sl, NA_W * GRID_W // CK, start=window_start,
                                     bias_fn=functools.partial(bias_chunk, h))
            jobs.append((q_ref[0, :, k_sl], chunks))
        o_ref[0] = jnp.concatenate(_attend(jobs, s_ref), axis=-1).astype(BF16)

    @pl.when(i == 0)
    def _():
        run(None)

    @pl.when(i > 0)
    def _():
        run(pl.multiple_of(N_CTX + s0 * GRID_W, GRID_W))


def _na_attention(na_qkv, table):
    bsz, t, _ = na_qkv.shape
    rows = (t - N_CTX) // GRID_W
    return pl.pallas_call(
        functools.partial(_na_kernel, rows=rows), grid=(bsz, t // TM),
        in_specs=[pl.BlockSpec((1, TM, 256), lambda b, i: (b, i, 0)),
                  pl.BlockSpec((1, t, 256), lambda b, i: (b, 0, 1)),
                  pl.BlockSpec((1, t, 512), lambda b, i: (b, 0, 1)),
                  _const_spec(table.shape)],
        out_specs=pl.BlockSpec((1, TM, 256), lambda b, i: (b, i, 0)),
        out_shape=jax.ShapeDtypeStruct((bsz, t, 256), BF16),
        scratch_shapes=[pltpu.VMEM((2, TM, CK + NA_W * GRID_W), F32)],
        compiler_params=_cparams(2), name="na_attention",
    )(na_qkv, na_qkv, na_qkv, table)


def _na_bias_table(rel_bias):
    cq = np.arange(GRID_W)
    col_lo = np.clip(cq - NA_KW // 2, 0, GRID_W - NA_KW)
    col_ok = (cq[None, :] >= col_lo[:, None]) & (cq[None, :] < col_lo[:, None] + NA_KW)
    col_off = np.clip(cq[None, :] - cq[:, None], 1 - NA_KW, NA_KW - 1) + (NA_KW - 1)
    col_sel = jnp.asarray(col_off[..., None] == np.arange(2 * NA_KW - 1), F32)
    t1 = jnp.einsum("hab,qkb->haqk", rel_bias * LOG2E, col_sel, precision=lax.Precision.HIGHEST)
    t1 = jnp.where(col_ok[None, None], t1, -1e30)
    fill = jnp.full((HEADS, 1, GRID_W, GRID_W), -1e30, F32)
    ext = jnp.concatenate([fill, t1, fill, fill], axis=1)
    return jnp.concatenate([ext[:, :-1], ext[:, 1:]], axis=-1)


def _mla_kernel(q_ref, k_ref, v_ref, o_ref, s_ref):
    i = pl.program_id(1)

    def run(n_chunks):
        jobs = []
        for h in range(HEADS):
            sl = slice(h * MLA_QK_PAD, (h + 1) * MLA_QK_PAD)
            jobs.append((q_ref[0, :, sl], _kvt_chunks(k_ref, v_ref, sl, h, n_chunks)))
        o_ref[0] = jnp.concatenate(_attend_t(jobs, s_ref), axis=0).T.astype(BF16)

    @pl.when(i == 0)
    def _():
        run(1)

    @pl.when(i > 0)
    def _():
        run(k_ref.shape[1] // CK)


def _mla_attention(mla_qk, mla_vt):
    bsz, t, _ = mla_qk.shape
    return pl.pallas_call(
        _mla_kernel, grid=(bsz, t // TM),
        in_specs=[pl.BlockSpec((1, TM, 512), lambda b, i: (b, i, 0)),
                  pl.BlockSpec((1, t, 512), lambda b, i: (b, 0, 1)),
                  pl.BlockSpec((1, HEADS * V_AUG, t), lambda b, i: (b, 0, 0))],
        out_specs=pl.BlockSpec((1, TM, 256), lambda b, i: (b, i, 0)),
        out_shape=jax.ShapeDtypeStruct((bsz, t, 256), BF16),
        scratch_shapes=[pltpu.VMEM((2, t, TM), F32)],
        compiler_params=_cparams(2), name="mla_attention",
    )(mla_qk, mla_qk, mla_vt)


def _diff_kernel(q_ref, k_ref, v_ref, lam_ref, gsub_ref, o_ref, s_ref, *, lam_init):
    i = pl.program_id(1)
    lv = lam_ref[...]
    lam = (jnp.exp(jnp.sum(lv[0:1] * lv[1:2], axis=-1, keepdims=True))
           - jnp.exp(jnp.sum(lv[2:3] * lv[3:4], axis=-1, keepdims=True)) + lam_init)

    def run(n_chunks):
        first = lax.broadcasted_iota(jnp.int32, (TM, 2 * DIFF_QK), 1) < DIFF_QK
        jobs = []
        for h in range(HEADS):
            sl = slice(h * 2 * DIFF_QK, (h + 1) * 2 * DIFF_QK)
            chunks = _kvt_chunks(k_ref, v_ref, sl, h, n_chunks)
            qh = q_ref[0, :, sl]
            zero = jnp.zeros_like(qh)
            jobs += [(jnp.where(first, qh, zero), chunks), (jnp.where(first, zero, qh), chunks)]
        ot = _attend_t(jobs, s_ref)
        outs = []
        for h in range(HEADS):
            d = ot[2 * h] - lam * ot[2 * h + 1]
            ms = jnp.mean(d * d, axis=0, keepdims=True)
            outs.append(d * lax.rsqrt(ms + EPS) * gsub_ref[...] * (1.0 - lam_init))
        o_ref[0] = jnp.concatenate(outs, axis=0).T.astype(BF16)

    @pl.when(i == 0)
    def _():
        run(1)

    @pl.when(i > 0)
    def _():
        run(k_ref.shape[1] // CK)


def _diff_attention(diff_qk, diff_vt, lam_vecs, g_sub, lam_init):
    bsz, t, _ = diff_qk.shape
    g_sub_t = jnp.broadcast_to(g_sub.reshape(DIFF_V, 1), (DIFF_V, TM))
    return pl.pallas_call(
        functools.partial(_diff_kernel, lam_init=lam_init), grid=(bsz, t // TM),
        in_specs=[pl.BlockSpec((1, TM, 256), lambda b, i: (b, i, 0)),
                  pl.BlockSpec((1, t, 256), lambda b, i: (b, 0, 1)),
                  pl.BlockSpec((1, HEADS * V_AUG, t), lambda b, i: (b, 0, 0)),
                  _const_spec((4, DIFF_QK)), _const_spec((DIFF_V, TM))],
        out_specs=pl.BlockSpec((1, TM, 256), lambda b, i: (b, i, 0)),
        out_shape=jax.ShapeDtypeStruct((bsz, t, 256), BF16),
        scratch_shapes=[pltpu.VMEM((2, t, TM), F32)],
        compiler_params=_cparams(2), name="diff_attention",
    )(diff_qk, diff_qk, diff_vt, lam_vecs, g_sub_t)


def _softplus(x):
    return jnp.maximum(x, 0.0) + jnp.log1p(jnp.exp(-jnp.abs(x)))


def _ssd_kernel(raw_ref, convw_ref, convb_ref, dtb_ref, alog_ref, dskip_ref, gnorm_ref,
                tril_ref, triu_ref, mlow_ref, mupp_ref, o_ref, xact_ref, yacc_ref, state_ref):
    t = raw_ref.shape[1]
    nt = t // TM
    xbc0 = SSD_INNER
    dt0 = SSD_INNER + SSD_XBC

    cw = convw_ref[...]
    cb = convb_ref[...]
    for j in range(nt):
        lo = j * TM
        cur = raw_ref[0, lo:lo + TM, xbc0:xbc0 + SSD_XBC]
        zeros8 = jnp.zeros((8, SSD_XBC), F32)
        prev = raw_ref[0, lo - 8:lo, xbc0:xbc0 + SSD_XBC] if j >= 2 else zeros8
        nxt = raw_ref[0, lo + TM:lo + TM + 8, xbc0:xbc0 + SSD_XBC] if 1 <= j < nt - 1 else zeros8
        u = jnp.concatenate([prev, cur, nxt], axis=0)
        acc = cb
        for kk in range(SSD_CONV):
            off = 8 - SSD_CONV // 2 + kk
            acc = acc + cw[kk:kk + 1, :] * u[off:off + TM, :]
        xact_ref[lo:lo + TM, :] = _silu(acc)

    a_pad = -jnp.exp(alog_ref[...])
    for d in range(2):
        tri_ref = tril_ref if d == 0 else triu_ref
        off_ref = mlow_ref if d == 0 else mupp_ref
        state_ref[...] = jnp.zeros_like(state_ref)

        def chunk(c, carry, d=d, tri_ref=tri_ref, off_ref=off_ref):
            if d == 0:
                blk = c
            else:
                blk = jnp.where(c == 0, 0, nt - c)
            off = pl.multiple_of(blk * TM, TM)
            rows = pl.ds(off, TM)
            dt = _softplus(raw_ref[0, rows, dt0:dt0 + 128] + dtb_ref[...])
            la = dt * a_pad
            cum = _split_dot_left(tri_ref[...], la)
            cum_t = cum.T
            dt_t = dt.T
            total = cum[TM - 1:TM, :] if d == 0 else cum[0:1, :]
            e_in = jnp.exp(cum)
            w_t = (jnp.exp(total - cum) * dt).T
            e_tot = jnp.exp(total)
            xs = xact_ref[rows, 0:SSD_INNER]
            xs_b = xs.astype(BF16)
            ys = []
            for g in range(SSD_GROUPS):
                bm = xact_ref[rows, SSD_INNER + g * SSD_STATE:SSD_INNER + (g + 1) * SSD_STATE]
                cm_b = xact_ref[rows, SSD_INNER + (SSD_GROUPS + g) * SSD_STATE:
                                SSD_INNER + (SSD_GROUPS + g + 1) * SSD_STATE].astype(BF16)
                gm = _dot_nt(cm_b, bm.astype(BF16))
                bm_t = bm.T
                for hh in range(HEADS // SSD_GROUPS):
                    h = g * (HEADS // SSD_GROUPS) + hh
                    j = d * HEADS + h
                    x_b = xs_b[:, h * SSD_HD:(h + 1) * SSD_HD]
                    dec = jnp.exp(cum[:, j:j + 1] - cum_t[j:j + 1, :] + off_ref[...])
                    y_d = _dot((gm * dec * dt_t[j:j + 1, :]).astype(BF16), x_b)
                    st = state_ref[h]
                    y_o = e_in[:, j:j + 1] * _dot(cm_b, st.astype(BF16))
                    new = _dot((bm_t * w_t[j:j + 1, :]).astype(BF16), x_b)
                    state_ref[h] = st * e_tot[:, j:j + 1] + new
                    ys.append(y_d + y_o + dskip_ref[d, h] * xs[:, h * SSD_HD:(h + 1) * SSD_HD])
            y = jnp.concatenate(ys, axis=-1)
            if d == 0:
                yacc_ref[rows, :] = y
            else:
                yacc_ref[rows, :] += y
            return carry

        lax.fori_loop(0, nt, chunk, 0)

    for j in range(nt):
        lo = j * TM
        y = yacc_ref[lo:lo + TM, :] * _silu(raw_ref[0, lo:lo + TM, 0:SSD_INNER])
        o_ref[0, lo:lo + TM, :] = _rms(y, gnorm_ref[...]).astype(BF16)


def _ssd(ssd_raw, lp, consts):
    bsz, t, _ = ssd_raw.shape
    c1 = lambda shape: pl.BlockSpec(shape, lambda b: (0,) * len(shape))
    return pl.pallas_call(
        _ssd_kernel, grid=(bsz,),
        in_specs=[pl.BlockSpec((1, t, SSD_W), lambda b: (b, 0, 0)),
                  c1((SSD_CONV, SSD_XBC)), c1((1, SSD_XBC)), c1((1, 128)), c1((1, 128)),
                  pl.BlockSpec(memory_space=pltpu.SMEM),
                  c1((1, SSD_INNER)), c1((TM, TM)), c1((TM, TM)), c1((TM, TM)), c1((TM, TM))],
        out_specs=pl.BlockSpec((1, t, SSD_INNER), lambda b: (b, 0, 0)),
        out_shape=jax.ShapeDtypeStruct((bsz, t, SSD_INNER), BF16),
        scratch_shapes=[pltpu.VMEM((t, SSD_XBC), F32), pltpu.VMEM((t, SSD_INNER), F32),
                        pltpu.VMEM((HEADS, SSD_STATE, SSD_HD), F32)],
        compiler_params=_cparams(1), name="ssd",
    )(ssd_raw, lp["ssd_convw"], lp["ssd_convb"], lp["ssd_dtb"], lp["ssd_alog"], lp["ssd_dskip"],
      lp["ssd_gnorm"], consts["tril"], consts["triu"], consts["mlow"], consts["mupp"])


def _outproj_kernel(na_ref, mla_ref, diff_ref, ssd_ref, h_ref, mod_ref, wout_ref, gffn_ref,
                    wrh_ref, wrl_ref, br_ref, lstrict_ref, ustrict_ref,
                    hout_ref, f_ref, route_ref, cnt_ref, meta_ref, run_ref):
    first = (pl.program_id(0) == 0) & (pl.program_id(1) == 0)

    @pl.when(first)
    def _():
        run_ref[...] = jnp.zeros_like(run_ref)

    o = (_dot(na_ref[0], wout_ref[0:256, :]) + _dot(mla_ref[0], wout_ref[256:512, :])
         + _dot(diff_ref[0], wout_ref[512:768, :]) + _dot(ssd_ref[0], wout_ref[768:1024, :]))
    gate = mod_ref[0, 0, 2:3, :]
    hn = h_ref[0] + gate * o
    hout_ref[0] = hn
    f = _rms(hn, gffn_ref[...]) * (1.0 + mod_ref[0, 0, 4:5, :]) + mod_ref[0, 0, 3:4, :]
    f_ref[0] = f

    f_hi = f.astype(BF16)
    f_lo = (f - f_hi.astype(F32)).astype(BF16)
    logits = _dot(f_hi, wrh_ref[...]) + _dot(f_lo, wrh_ref[...]) + _dot(f_hi, wrl_ref[...]) + br_ref[...]
    lane = lax.broadcasted_iota(jnp.int32, logits.shape, 1)
    lane_f = lane.astype(F32)
    neg = jnp.float32(-jnp.inf)
    big = jnp.float32(1e9)
    gl = jnp.where(lane < MOE_GROUPS, logits, neg)
    gmax = jnp.max(gl, axis=-1, keepdims=True)
    g_top_p = 1.0 / jnp.sum(jnp.exp(gl - gmax), axis=-1, keepdims=True)
    g_top = jnp.min(jnp.where(gl == gmax, lane_f, big), axis=-1, keepdims=True).astype(jnp.int32)
    in_group = (lane >= MOE_GROUPS) & (lane < MOE_GROUPS + MOE_EXPERTS) & (((lane - MOE_GROUPS) // MOE_EPG) == g_top)
    el = jnp.where(in_group, logits, neg)
    m1 = jnp.max(el, axis=-1, keepdims=True)
    i1 = jnp.min(jnp.where(el == m1, lane_f, big), axis=-1, keepdims=True)
    el2 = jnp.where(lane_f == i1, neg, el)
    m2 = jnp.max(el2, axis=-1, keepdims=True)
    i2 = jnp.min(jnp.where(el2 == m2, lane_f, big), axis=-1, keepdims=True)
    x2 = jnp.exp(m2 - m1)
    w1 = g_top_p / (1.0 + x2)
    w2 = g_top_p * x2 / (1.0 + x2)
    e1 = i1 - MOE_GROUPS
    e2 = i2 - MOE_GROUPS

    onehot = ((lane_f == e1) | (lane_f == e2)).astype(F32)
    tile_cnt = jnp.floor((jnp.sum(onehot, axis=0, keepdims=True) + (RUN_ALIGN - 1.0)) * (1.0 / RUN_ALIGN)) * RUN_ALIGN
    tile_start = _dot(jnp.broadcast_to(tile_cnt, (8, 128)).astype(BF16), ustrict_ref[...])[0:1]
    pos = _dot(lstrict_ref[...], onehot.astype(BF16)) + tile_start
    p1 = jnp.sum(jnp.where(lane_f == e1, pos, 0.0), axis=-1, keepdims=True)
    p2 = jnp.sum(jnp.where(lane_f == e2, pos, 0.0), axis=-1, keepdims=True)
    run_old = run_ref[...]
    run_ref[...] = run_old + tile_cnt
    cnt_ref[...] = run_old + tile_cnt
    route = jnp.zeros(logits.shape, F32)
    for idx, val in enumerate((p1, p2, w1, w2)):
        route = jnp.where(lane == idx, val, route)
    route_ref[0] = route
    sub = lax.broadcasted_iota(jnp.int32, (8, 128), 0)
    meta_ref[0] = jnp.where(sub == 0, tile_cnt, jnp.where(sub == 1, tile_start, jnp.where(sub == 2, run_old, 0.0)))


def _outproj(mix, h, modsel, wout, gffn, lp, consts):
    bsz, t, _ = h.shape
    row = lambda w_: pl.BlockSpec((1, TM, w_), lambda b, i: (b, i, 0))
    return pl.pallas_call(
        _outproj_kernel, grid=(bsz, t // TM),
        in_specs=[row(256), row(256), row(256), row(256), row(D),
                  pl.BlockSpec((1, 1, 6, D), lambda b, i: (b, jnp.minimum(i, 1), 0, 0)),
                  _const_spec((D, D)), _const_spec((1, D)), _const_spec((D, 128)), _const_spec((D, 128)),
                  _const_spec((1, 128)), _const_spec((TM, TM)), _const_spec((128, 128))],
        out_specs=[row(D), row(D), row(128), _const_spec((1, 128)),
                   pl.BlockSpec((1, 8, 128), lambda b, i: (b * (t // TM) + i, 0, 0))],
        out_shape=[jax.ShapeDtypeStruct((bsz, t, D), F32), jax.ShapeDtypeStruct((bsz, t, D), F32),
                   jax.ShapeDtypeStruct((bsz, t, 128), F32), jax.ShapeDtypeStruct((1, 128), F32),
                   jax.ShapeDtypeStruct((bsz * (t // TM), 8, 128), F32)],
        scratch_shapes=[pltpu.VMEM((1, 128), F32)],
        compiler_params=_cparams(2), name="outproj_router",
    )(*mix, h, modsel, wout, gffn, lp["wr_hi"], lp["wr_lo"], lp["br"], consts["lstrict"], consts["ustrict"])


def _run_dmas(tile, n_ref, ls_ref, gs_ref, make_copy):
    def walk(wait):
        for e in range(MOE_EXPERTS):
            n = n_ref[tile * MOE_EXPERTS + e]
            l0 = ls_ref[tile * MOE_EXPERTS + e]
            g0 = gs_ref[tile * MOE_EXPERTS + e]
            for b in range(RUN_BITS[1] - 1, RUN_BITS[0] - 1, -1):
                off = (n >> (b + 1)) << (b + 1)

                @pl.when(((n >> b) & 1) == 1)
                def _(b=b, off=off, l0=l0, g0=g0):
                    c = make_copy(pl.multiple_of(l0 + off, RUN_ALIGN), pl.multiple_of(g0 + off, RUN_ALIGN), 1 << b)
                    if wait:
                        c.wait()
                    else:
                        c.start(priority=b % 2)
    return walk


def _dispatch_kernel(n_ref, ls_ref, gs_ref, pe_ref, nu_ref, f_ref, route_ref, xs_ref, sbuf_ref, zbuf_ref,
                     sem, zsem, *, first_tail, n_blocks):
    tile = pl.program_id(0)

    def pad_copies(action):
        for e in range(MOE_EXPERTS):
            prev = pe_ref[e - 1] if e > 0 else 0

            @pl.when(pe_ref[e] > prev)
            def _(e=e):
                start = pl.multiple_of(pe_ref[e] - MOE_MB, MOE_MB)
                action(pltpu.make_async_copy(zbuf_ref, xs_ref.at[pl.ds(start, MOE_MB)], zsem))
        for blk in range(first_tail, n_blocks):
            @pl.when(blk >= nu_ref[0])
            def _(blk=blk):
                action(pltpu.make_async_copy(zbuf_ref, xs_ref.at[pl.ds(blk * MOE_MB, MOE_MB)], zsem))

    @pl.when(tile == 0)
    def _():
        zbuf_ref[...] = jnp.zeros_like(zbuf_ref)
        pad_copies(lambda c: c.start())
        pad_copies(lambda c: c.wait())

    buf = tile % 2
    pos = route_ref[...].T
    slot = lax.broadcasted_iota(jnp.int32, (SORT_ROWS, TM), 0).astype(F32)
    perm = ((slot == pos[0:1, :]) | (slot == pos[1:2, :])).astype(BF16)
    sbuf_ref[buf] = _dot(perm, f_ref[...].astype(BF16)).astype(BF16)

    def runs(t, s):
        return _run_dmas(t, n_ref, ls_ref, gs_ref, lambda l, g, size: pltpu.make_async_copy(
            sbuf_ref.at[s, pl.ds(l, size)], xs_ref.at[pl.ds(g, size)], sem.at[s]))

    runs(tile, buf)(False)

    @pl.when(tile >= 1)
    def _():
        runs(tile - 1, 1 - buf)(True)

    @pl.when(tile == pl.num_programs(0) - 1)
    def _():
        runs(tile, buf)(True)


def _dispatch(plan, f2d, route2d, cap):
    n_tok = f2d.shape[0]
    n_blocks = cap // MOE_MB
    first_tail = 2 * n_tok // MOE_MB
    assert 2 * TM + MOE_EXPERTS * (RUN_ALIGN - 1) <= SORT_ROWS
    return pl.pallas_call(
        functools.partial(_dispatch_kernel, first_tail=first_tail, n_blocks=n_blocks),
        grid_spec=pltpu.PrefetchScalarGridSpec(
            num_scalar_prefetch=5, grid=(n_tok // TM,),
            in_specs=[pl.BlockSpec((TM, D), lambda i, *_: (i, 0)),
                      pl.BlockSpec((TM, 128), lambda i, *_: (i, 0))],
            out_specs=pl.BlockSpec(memory_space=pl.ANY),
            scratch_shapes=[pltpu.VMEM((2, SORT_ROWS, D), BF16), pltpu.VMEM((MOE_MB, D), BF16),
                            pltpu.SemaphoreType.DMA((2,)), pltpu.SemaphoreType.DMA(())]),
        out_shape=jax.ShapeDtypeStruct((cap, D), BF16),
        compiler_params=_cparams(1), name="moe_dispatch",
    )(plan["n"], plan["ls"], plan["gs"], plan["pad_end"], plan["n_used"], f2d, route2d)


def _experts_kernel(be_ref, nb_ref, x_ref, wg_ref, wu_ref, wd_ref, y_ref, wgu_s, wd_s):
    i = pl.program_id(0)

    @pl.when((i == 0) | (be_ref[i] != be_ref[jnp.maximum(i - 1, 0)]))
    def _():
        wgu_s[:, 0:MOE_FF] = wg_ref[0, 0].astype(BF16)
        wgu_s[:, MOE_FF:2 * MOE_FF] = wu_ref[0, 0].astype(BF16)
        wd_s[...] = wd_ref[0, 0].astype(BF16)

    @pl.when(i < nb_ref[0])
    def _():
        gu = _dot(x_ref[...], wgu_s[...])
        a = _silu(gu[:, 0:MOE_FF]) * gu[:, MOE_FF:2 * MOE_FF]
        y_ref[...] = _dot(a.astype(BF16), wd_s[...]).astype(BF16)

    @pl.when(i >= nb_ref[0])
    def _():
        y_ref[...] = jnp.zeros_like(y_ref)


def _experts(plan, xs, w_gate, w_up, w_down, layer):
    cap = xs.shape[0]
    wspec = lambda a, b: pl.BlockSpec((1, 1, a, b), lambda i, be, nb: (layer, be[i], 0, 0))
    return pl.pallas_call(
        _experts_kernel,
        grid_spec=pltpu.PrefetchScalarGridSpec(
            num_scalar_prefetch=2, grid=(cap // MOE_MB,),
            in_specs=[pl.BlockSpec((MOE_MB, D), lambda i, be, nb: (i, 0)),
                      wspec(D, MOE_FF), wspec(D, MOE_FF), wspec(MOE_FF, D)],
            out_specs=pl.BlockSpec((MOE_MB, D), lambda i, be, nb: (i, 0)),
            scratch_shapes=[pltpu.VMEM((D, 2 * MOE_FF), BF16), pltpu.VMEM((MOE_FF, D), BF16)]),
        out_shape=jax.ShapeDtypeStruct((cap, D), BF16),
        compiler_params=_cparams(1), name="moe_experts",
    )(plan["block_e"], plan["n_used"], xs, w_gate, w_up, w_down)


def _combine_kernel(n_ref, ls_ref, gs_ref, h_ref, mod_ref, route_ref, y_ref, o_ref, ybuf_ref, sem, *, skip, nt):
    per_sample = pl.num_programs(1)
    step = pl.program_id(0) * per_sample + pl.program_id(1)
    buf = step % 2

    def runs(s, slot):
        tile = (s // per_sample) * nt + s % per_sample + skip
        return _run_dmas(tile, n_ref, ls_ref, gs_ref, lambda l, g, size: pltpu.make_async_copy(
            y_ref.at[pl.ds(g, size)], ybuf_ref.at[slot, pl.ds(l, size)], sem.at[slot]))

    @pl.when(step == 0)
    def _():
        ybuf_ref[...] = jnp.zeros_like(ybuf_ref)
        runs(step, buf)(False)

    @pl.when(step + 1 < pl.num_programs(0) * per_sample)
    def _():
        runs(step + 1, 1 - buf)(False)

    runs(step, buf)(True)

    slot = lax.broadcasted_iota(jnp.int32, (TM, SORT_ROWS), 1).astype(F32)
    r = route_ref[0]
    wm = (jnp.where(slot == r[:, 0:1], r[:, 2:3], 0.0) + jnp.where(slot == r[:, 1:2], r[:, 3:4], 0.0))
    w_hi = wm.astype(BF16)
    w_lo = (wm - w_hi.astype(F32)).astype(BF16)
    yv = ybuf_ref[buf]
    out = _dot(w_hi, yv) + _dot(w_lo, yv)
    o_ref[0] = h_ref[0] + mod_ref[0, 0, 5:6, :] * out


def _combine(plan, h, modsel, route, y, latent_only):
    bsz, t, _ = h.shape
    nt = t // TM
    skip = 1 if latent_only else 0
    row = lambda w_: pl.BlockSpec((1, TM, w_), lambda b, i, *_: (b, i + skip, 0))
    return pl.pallas_call(
        functools.partial(_combine_kernel, skip=skip, nt=nt),
        grid_spec=pltpu.PrefetchScalarGridSpec(
            num_scalar_prefetch=3, grid=(bsz, nt - skip),
            in_specs=[row(D),
                      pl.BlockSpec((1, 1, 6, D), lambda b, i, *_: (b, jnp.minimum(i + skip, 1), 0, 0)),
                      row(128),
                      pl.BlockSpec(memory_space=pl.ANY)],
            out_specs=pl.BlockSpec((1, TM, D), lambda b, i, *_: (b, i, 0)),
            scratch_shapes=[pltpu.VMEM((2, SORT_ROWS, D), BF16), pltpu.SemaphoreType.DMA((2,))]),
        out_shape=jax.ShapeDtypeStruct((bsz, t - skip * TM, D), F32),
        compiler_params=_cparams(2), name="moe_combine",
    )(plan["n"], plan["ls"], plan["gs"], h, modsel, route, y)


def _moe_plan(meta, counts, n_blocks):
    cnt = counts[0, :MOE_EXPERTS].astype(jnp.int32)
    padded = (cnt + MOE_MB - 1) // MOE_MB * MOE_MB
    pad_end = jnp.cumsum(padded)
    pad_start = pad_end - padded
    m = meta[:, 0:3, 0:MOE_EXPERTS].astype(jnp.int32)
    blk0 = jnp.arange(n_blocks, dtype=jnp.int32) * MOE_MB
    block_e = jnp.minimum(jnp.sum(blk0[:, None] >= pad_end[None, :], axis=-1), MOE_EXPERTS - 1)
    return {"n": m[:, 0].reshape(-1), "ls": m[:, 1].reshape(-1), "gs": (pad_start[None, :] + m[:, 2]).reshape(-1),
            "pad_end": pad_end.astype(jnp.int32), "n_used": (pad_end[-1:] // MOE_MB).astype(jnp.int32),
            "block_e": block_e.astype(jnp.int32)}


def _block_diag(n, seg):
    idx = np.arange(n) // seg
    return jnp.asarray(idx[:, None] == idx[None, :], BF16)


def _constants():
    lower = np.tril(np.ones((TM, TM), np.float32))
    upper = np.triu(np.ones((TM, TM), np.float32))
    return {"bd64": _block_diag(256, 64), "bd128": _block_diag(512, 128), "bd32": _block_diag(256, 32),
            "tril": jnp.asarray(lower, BF16), "triu": jnp.asarray(upper, BF16),
            "mlow": jnp.asarray((lower - 1.0) * 1e30, F32), "mupp": jnp.asarray((upper - 1.0) * 1e30, F32),
            "lstrict": jnp.asarray(np.tril(np.ones((TM, TM)), -1), BF16),
            "ustrict": jnp.asarray(np.triu(np.ones((128, 128)), 1), BF16)}


def _rope_tables(n_lat, t):
    n_freq = 8
    inv = jnp.power(10000.0, -jnp.arange(n_freq, dtype=F32) / n_freq)
    tok = jnp.arange(n_lat, dtype=jnp.int32)
    row = (tok // GRID_W).astype(F32)
    col = (tok % GRID_W).astype(F32)
    ang = jnp.concatenate([row[:, None] * inv, col[:, None] * inv], axis=-1)
    n_c = t - n_lat
    cos = jnp.concatenate([jnp.ones((n_c, 16), F32), jnp.cos(ang)], axis=0)
    sin = jnp.concatenate([jnp.zeros((n_c, 16), F32), jnp.sin(ang)], axis=0)
    z16 = jnp.zeros((t, 16), F32)
    one = lambda w_: jnp.ones((t, w_), F32)
    zero = lambda w_: jnp.zeros((t, w_), F32)
    mc = jnp.concatenate([one(64), cos, cos, one(32)], axis=-1)
    ms1 = jnp.concatenate([zero(64), -sin, z16, zero(32)], axis=-1)
    ms2 = jnp.concatenate([zero(64), z16, sin, zero(32)], axis=-1)
    dc = jnp.concatenate([cos, cos], axis=-1)
    ds1 = jnp.concatenate([-sin, z16], axis=-1)
    ds2 = jnp.concatenate([z16, sin], axis=-1)
    tile = lambda a, n: jnp.tile(a, (1, n))
    return {"mc": tile(mc, 4), "ms1": tile(ms1, 4), "ms2": tile(ms2, 4),
            "dc": tile(dc, 8), "ds1": tile(ds1, 8), "ds2": tile(ds2, 8)}


def _pad_heads(w, width, padded):
    lead = w.shape[:-1]
    w = w.reshape(lead + (HEADS, width))
    w = jnp.pad(w, [(0, 0)] * len(lead) + [(0, 0), (0, padded - width)])
    return w.reshape(lead + (HEADS * padded,))


def _pack_w_in(w_in):
    n_layers = w_in.shape[0]
    z = lambda n: jnp.zeros((n_layers, D, n), w_in.dtype)
    na = w_in[:, :, 0:768]
    mla = w_in[:, :, 768:1184]
    diff = w_in[:, :, 1184:1952]
    ssd = w_in[:, :, 1952:2984]
    na_p = jnp.concatenate([na[:, :, 0:512], _pad_heads(na[:, :, 512:768], V_HD, V_AUG)], axis=-1)
    mla_p = jnp.concatenate([mla[:, :, 0:384], z(64), mla[:, :, 384:416], z(32)], axis=-1)
    diff_p = jnp.concatenate([diff[:, :, 0:512], _pad_heads(diff[:, :, 512:768], V_HD, V_AUG)], axis=-1)
    ssd_p = jnp.concatenate([ssd, z(SSD_W - 1032)], axis=-1)
    return jnp.concatenate([na_p, mla_p, diff_p, ssd_p], axis=-1).astype(BF16)


def _layer_params(l, p):
    row = lambda a: a.reshape(1, -1)
    t4 = lambda a: jnp.tile(a.reshape(1, -1), (1, HEADS))
    wqb = _pad_heads(p["mla_w_qb"][l], MLA_QK, MLA_QK_PAD)
    wkvb = p["mla_w_kvb"][l].reshape(MLA_KV_RANK, HEADS, MLA_NOPE + MLA_V)
    wkvb = jnp.concatenate([_pad_heads(wkvb[:, :, :MLA_NOPE].reshape(MLA_KV_RANK, -1), MLA_NOPE, MLA_QK_PAD),
                            _pad_heads(wkvb[:, :, MLA_NOPE:].reshape(MLA_KV_RANK, -1), MLA_V, V_AUG)], axis=-1)
    gpad = lambda g: jnp.tile(jnp.pad(g, (0, MLA_QK_PAD - MLA_QK)).reshape(1, -1), (1, HEADS))
    lane8 = lambda a: jnp.pad(a.reshape(1, -1), ((0, 0), (0, 128 - 2 * HEADS)))
    wr = jnp.concatenate([p["moe_w_group"][l], p["moe_w_expert"][l],
                          jnp.zeros((D, 128 - MOE_GROUPS - MOE_EXPERTS), F32)], axis=-1)
    wr_hi = wr.astype(BF16)
    br = jnp.concatenate([p["moe_b_group"][l], p["moe_b_expert"][l],
                          jnp.zeros((128 - MOE_GROUPS - MOE_EXPERTS,), F32)]).reshape(1, 128)
    return {
        "g_mix": row(p["g_mix"][l]), "g_ffn": row(p["g_ffn"][l]),
        "na_gq": t4(p["na_g_q"][l]), "na_gk": t4(p["na_g_k"][l]),
        "mla_gqa": row(p["mla_g_qa"][l]), "mla_wqb": wqb.astype(BF16),
        "mla_gkva": row(p["mla_g_kva"][l]), "mla_wkvb": wkvb.astype(BF16),
        "mla_gq": gpad(p["mla_g_q"][l]), "mla_gk": gpad(p["mla_g_k"][l]),
        "diff_gq": jnp.tile(p["diff_g_q"][l].reshape(1, -1), (1, 8)),
        "diff_gk": jnp.tile(p["diff_g_k"][l].reshape(1, -1), (1, 8)),
        "diff_lam": p["diff_lambda"][l], "diff_gsub": row(p["diff_g_sub"][l]),
        "ssd_convw": p["ssd_conv_w"][l], "ssd_convb": row(p["ssd_conv_b"][l]),
        "ssd_dtb": lane8(p["ssd_dt_bias"][l]), "ssd_alog": lane8(p["ssd_a_log"][l]),
        "ssd_dskip": p["ssd_d"][l], "ssd_gnorm": row(p["ssd_g_norm"][l]),
        "wr_hi": wr_hi, "wr_lo": (wr - wr_hi.astype(F32)).astype(BF16), "br": br,
    }


def _mixers(h, modsel, w_in_l, lp, consts, tabs, bias, lam_init):
    na_qkv, mla_qk, mla_vt, diff_qk, diff_vt, ssd_raw = _inproj(h, modsel, lp["g_mix"], w_in_l, consts, lp, tabs)
    return (_na_attention(na_qkv, bias),
            _mla_attention(mla_qk, mla_vt),
            _diff_attention(diff_qk, diff_vt, lp["diff_lam"], lp["diff_gsub"], lam_init),
            _ssd(ssd_raw, lp, consts))


def _moe(hn, f, route, counts, meta, modsel, w_gate, w_up, w_down, layer, latent_only):
    bsz, t, _ = hn.shape
    n_asg = bsz * t * 2
    assert n_asg % MOE_MB == 0
    n_slots = n_asg + (bsz * t // TM) * MOE_EXPERTS * (RUN_ALIGN - 1)
    n_blocks = -(-n_slots // MOE_MB) + MOE_EXPERTS
    plan = _moe_plan(meta, counts, n_blocks)
    xs = _dispatch(plan, f.reshape(bsz * t, D), route.reshape(bsz * t, 128), n_blocks * MOE_MB)
    y = _experts(plan, xs, w_gate, w_up, w_down, layer)
    return _combine(plan, hn, modsel, route, y, latent_only)


def kernel(x, c, ctx, c_ctx, w_mod, b_mod, g_mix, w_in, w_out, na_g_q, na_g_k, na_rel_bias,
           mla_g_qa, mla_w_qb, mla_g_kva, mla_w_kvb, mla_g_q, mla_g_k,
           diff_g_q, diff_g_k, diff_lambda, diff_g_sub,
           ssd_conv_w, ssd_conv_b, ssd_dt_bias, ssd_a_log, ssd_d, ssd_g_norm,
           g_ffn, moe_w_group, moe_b_group, moe_w_expert, moe_b_expert, moe_w_gate, moe_w_up, moe_w_down):
    p = dict(g_mix=g_mix, g_ffn=g_ffn, na_g_q=na_g_q, na_g_k=na_g_k,
             mla_g_qa=mla_g_qa, mla_w_qb=mla_w_qb, mla_g_kva=mla_g_kva, mla_w_kvb=mla_w_kvb,
             mla_g_q=mla_g_q, mla_g_k=mla_g_k, diff_g_q=diff_g_q, diff_g_k=diff_g_k,
             diff_lambda=diff_lambda, diff_g_sub=diff_g_sub,
             ssd_conv_w=ssd_conv_w, ssd_conv_b=ssd_conv_b, ssd_dt_bias=ssd_dt_bias, ssd_a_log=ssd_a_log,
             ssd_d=ssd_d, ssd_g_norm=ssd_g_norm, moe_w_group=moe_w_group, moe_b_group=moe_b_group,
             moe_w_expert=moe_w_expert, moe_b_expert=moe_b_expert)
    bsz, n_lat, _ = x.shape
    n_ctx = ctx.shape[1]
    assert n_ctx == N_CTX == TM == CK and n_lat % TM == 0 and bsz < 16
    t = n_ctx + n_lat
    n_layers = w_mod.shape[0]
    rows = n_lat // GRID_W
    assert rows % NA_R == 0 and rows >= NA_W and (NA_W * GRID_W) % CK == 0

    consts = _constants()
    tabs = _rope_tables(n_lat, t)
    w_in_p = _pack_w_in(w_in)
    w_out_b = w_out.astype(BF16)

    cvec = jnp.concatenate([c, c_ctx[None, :], jnp.zeros((16 - bsz - 1, D), F32)], axis=0)
    mod = _modulation(cvec, w_mod, b_mod).reshape(n_layers, 16, 6, D)

    h = jnp.concatenate([ctx, x], axis=1)
    for l in range(n_layers):
        lp = _layer_params(l, p)
        modsel = jnp.stack([jnp.broadcast_to(mod[l, bsz][None], (bsz, 6, D)), mod[l, :bsz]], axis=1)
        lam_init = 0.8 - 0.6 * math.exp(-0.3 * l)
        bias = _na_bias_table(na_rel_bias[l])
        mix = _mixers(h, modsel, w_in_p[l], lp, consts, tabs, bias, lam_init)
        hn, f, route, counts, meta = _outproj(mix, h, modsel, w_out_b[l], lp["g_ffn"], lp, consts)
        h = _moe(hn, f, route, counts, meta, modsel, moe_w_gate, moe_w_up, moe_w_down, l,
                 latent_only=(l == n_layers - 1))
    return h
```

```python
import functools
import math

import numpy as np
import jax
import jax.numpy as jnp
from jax import lax
from jax.experimental import pallas as pl
from jax.experimental.pallas import tpu as pltpu

F32 = jnp.float32
BF16 = jnp.bfloat16

D = 1024
GRID_W = 64
N_CTX = 256
HEADS = 4
NA_HD = 64
NA_KH = 8
NA_KW = 16
MLA_NOPE = 64
MLA_ROPE = 32
MLA_QK = MLA_NOPE + MLA_ROPE
MLA_QK_PAD = 128
MLA_V = 64
MLA_Q_RANK = 256
MLA_KV_RANK = 128
DIFF_QK = 32
DIFF_V = 64
SSD_INNER = 256
SSD_HD = 64
SSD_STATE = 128
SSD_GROUPS = 2
SSD_CONV = 5
SSD_XBC = 768
MOE_GROUPS = 4
MOE_EPG = 4
MOE_EXPERTS = 16
MOE_FF = 512
EPS = 1e-6

TM = 256
MOE_MB = 512
NA_R = 4
NA_W = 12
V_HD = 64
V_AUG = 128
CK = 256
LOG2E = math.log2(math.e)
RUN_ALIGN = 16
RUN_BITS = (4, 9)
SORT_ROWS = 2 * TM + 256

P_NA = 0
P_MLA = 1024
P_DIFF = 1536
P_SSD = 2560
P_W = 3712
SSD_W = 1152

VMEM_LIMIT = 56 * 1024 * 1024


def _cparams(n_axes):
    return pltpu.CompilerParams(dimension_semantics=("arbitrary",) * n_axes,
                                vmem_limit_bytes=VMEM_LIMIT)


def _dot(a, b):
    return jnp.dot(a, b, preferred_element_type=F32)


def _dot_nt(a, b):
    return lax.dot_general(a, b, (((1,), (1,)), ((), ())), preferred_element_type=F32)


def _dot_tn(a, b):
    return lax.dot_general(a, b, (((0,), (0,)), ((), ())), preferred_element_type=F32)


def _split3(x):
    hi = x.astype(BF16)
    r1 = x - hi.astype(F32)
    mid = r1.astype(BF16)
    lo = (r1 - mid.astype(F32)).astype(BF16)
    return hi, mid, lo


def _split_dot(x, m):
    hi, mid, lo = _split3(x)
    return _dot(hi, m) + _dot(mid, m) + _dot(lo, m)


def _split_dot_left(m, x):
    hi, mid, lo = _split3(x)
    return _dot(m, hi) + _dot(m, mid) + _dot(m, lo)


def _rms(x, g):
    ms = jnp.mean(x * x, axis=-1, keepdims=True)
    return x * lax.rsqrt(ms + EPS) * g


def _seg_rms(x, bd, inv_n, g):
    x2 = x * x
    hi = x2.astype(BF16)
    lo = (x2 - hi.astype(F32)).astype(BF16)
    ms = (_dot(hi, bd) + _dot(lo, bd)) * inv_n
    return x * lax.rsqrt(ms + EPS) * g


def _silu(x):
    return x * jax.nn.sigmoid(x)


def _rope(x, c, s1, s2, width):
    rot = 16
    return x * c + pltpu.roll(x, width - rot, 1) * s1 + pltpu.roll(x, rot, 1) * s2


def _with_ones(v):
    lane = lax.broadcasted_iota(jnp.int32, v.shape, 1)
    return jnp.where(lane % V_AUG == V_HD, 1.0, v)


def _mod_kernel(c_ref, w_ref, b_ref, o_ref):
    s = _silu(c_ref[...])
    o_ref[0] = _dot(s.astype(BF16), w_ref[0].astype(BF16)) + b_ref[0]


def _modulation(cvec, w_mod, b_mod):
    n_layers = w_mod.shape[0]
    tn = 1536
    return pl.pallas_call(
        _mod_kernel,
        grid=(n_layers, 6 * D // tn),
        in_specs=[pl.BlockSpec((16, D), lambda l, j: (0, 0)),
                  pl.BlockSpec((1, D, tn), lambda l, j: (l, 0, j)),
                  pl.BlockSpec((1, 1, tn), lambda l, j: (l, 0, j))],
        out_specs=pl.BlockSpec((1, 16, tn), lambda l, j: (l, 0, j)),
        out_shape=jax.ShapeDtypeStruct((n_layers, 16, 6 * D), F32),
        compiler_params=_cparams(2),
        name="modulation",
    )(cvec, w_mod, b_mod.reshape(n_layers, 1, 6 * D))


def _inproj_kernel(h_ref, mod_ref, gmix_ref, w_ref, bd64_ref, bd128_ref, bd32_ref,
                   nagq_ref, nagk_ref, gqa_ref, wqb_ref, gkva_ref, wkvb_ref, mgq_ref, mgk_ref,
                   dgq_ref, dgk_ref, mc_ref, ms1_ref, ms2_ref, dc_ref, ds1_ref, ds2_ref,
                   na_ref, mla_ref, mlavt_ref, diff_ref, diffvt_ref, ssd_ref):
    x = h_ref[0]
    shift = mod_ref[0, 0, 0:1, :]
    scale = mod_ref[0, 0, 1:2, :]
    a = _rms(x, gmix_ref[...]) * (1.0 + scale) + shift
    p = _dot(a.astype(BF16), w_ref[...])


    bd64 = bd64_ref[...]
    q = p[:, P_NA:P_NA + 256]
    k = p[:, P_NA + 256:P_NA + 512]
    na_ref[0, :, 0:256] = (_seg_rms(q, bd64, 1.0 / NA_HD, nagq_ref[...]) * (NA_HD ** -0.5 * LOG2E)).astype(BF16)
    na_ref[0, :, 256:512] = _seg_rms(k, bd64, 1.0 / NA_HD, nagk_ref[...]).astype(BF16)
    na_ref[0, :, 512:1024] = _with_ones(p[:, P_NA + 512:P_NA + 1024]).astype(BF16)

    bd128 = bd128_ref[...]
    cq = p[:, P_MLA:P_MLA + 256]
    ckv = p[:, P_MLA + 256:P_MLA + 384]
    kr = p[:, P_MLA + 384:P_MLA + 512]
    q2 = _dot(_rms(cq, gqa_ref[...]).astype(BF16), wqb_ref[...])
    kv = _dot(_rms(ckv, gkva_ref[...]).astype(BF16), wkvb_ref[...])
    k2 = kv[:, 0:512] + jnp.concatenate([kr] * HEADS, axis=-1)
    mc, ms1, ms2 = mc_ref[...], ms1_ref[...], ms2_ref[...]
    qn = _rope(_seg_rms(q2, bd128, 1.0 / MLA_QK, mgq_ref[...]), mc, ms1, ms2, 512)
    kn = _rope(_seg_rms(k2, bd128, 1.0 / MLA_QK, mgk_ref[...]), mc, ms1, ms2, 512)
    mla_ref[0, :, 0:512] = (qn * (MLA_QK ** -0.5 * LOG2E)).astype(BF16)
    mla_ref[0, :, 512:1024] = kn.astype(BF16)
    mlavt_ref[0] = _with_ones(kv[:, 512:1024]).T.astype(BF16)

    bd32 = bd32_ref[...]
    dc, ds1, ds2 = dc_ref[...], ds1_ref[...], ds2_ref[...]
    dq = p[:, P_DIFF:P_DIFF + 256]
    dk = p[:, P_DIFF + 256:P_DIFF + 512]
    dqn = _rope(_seg_rms(dq, bd32, 1.0 / DIFF_QK, dgq_ref[...]), dc, ds1, ds2, 256)
    dkn = _rope(_seg_rms(dk, bd32, 1.0 / DIFF_QK, dgk_ref[...]), dc, ds1, ds2, 256)
    diff_ref[0, :, 0:256] = (dqn * (DIFF_QK ** -0.5 * LOG2E)).astype(BF16)
    diff_ref[0, :, 256:512] = dkn.astype(BF16)
    diffvt_ref[0] = _with_ones(p[:, P_DIFF + 512:P_DIFF + 1024]).T.astype(BF16)

    ssd_ref[0] = p[:, P_SSD:P_SSD + SSD_W]


def _const_spec(shape):
    nd = len(shape)
    return pl.BlockSpec(shape, lambda b, i: (0,) * nd)


def _inproj(h, modsel, gmix, w, consts, lp, tabs):
    bsz, t, _ = h.shape
    nt = t // TM
    row = lambda w_: pl.BlockSpec((1, TM, w_), lambda b, i: (b, i, 0))
    tab = lambda w_: pl.BlockSpec((TM, w_), lambda b, i: (i, 0))
    in_specs = [
        row(D),
        pl.BlockSpec((1, 1, 6, D), lambda b, i: (b, jnp.minimum(i, 1), 0, 0)),
        _const_spec((1, D)), _const_spec((D, P_W)),
        _const_spec((256, 256)), _const_spec((512, 512)), _const_spec((256, 256)),
        _const_spec((1, 256)), _const_spec((1, 256)),
        _const_spec((1, 256)), _const_spec((256, 512)), _const_spec((1, 128)), _const_spec((128, 1024)),
        _const_spec((1, 512)), _const_spec((1, 512)),
        _const_spec((1, 256)), _const_spec((1, 256)),
        tab(512), tab(512), tab(512), tab(256), tab(256), tab(256),
    ]
    col = lambda w_: pl.BlockSpec((1, w_, TM), lambda b, i: (b, 0, i))
    out_shape = [jax.ShapeDtypeStruct((bsz, t, 1024), BF16),
                 jax.ShapeDtypeStruct((bsz, t, 1024), BF16),
                 jax.ShapeDtypeStruct((bsz, HEADS * V_AUG, t), BF16),
                 jax.ShapeDtypeStruct((bsz, t, 512), BF16),
                 jax.ShapeDtypeStruct((bsz, HEADS * V_AUG, t), BF16),
                 jax.ShapeDtypeStruct((bsz, t, SSD_W), F32)]
    out_specs = [row(1024), row(1024), col(HEADS * V_AUG), row(512), col(HEADS * V_AUG), row(SSD_W)]
    return pl.pallas_call(
        _inproj_kernel, grid=(bsz, nt), in_specs=in_specs, out_specs=out_specs, out_shape=out_shape,
        compiler_params=_cparams(2), name="inproj",
    )(h, modsel, gmix, w, consts["bd64"], consts["bd128"], consts["bd32"],
      lp["na_gq"], lp["na_gk"], lp["mla_gqa"], lp["mla_wqb"], lp["mla_gkva"], lp["mla_wkvb"],
      lp["mla_gq"], lp["mla_gk"], lp["diff_gq"], lp["diff_gk"],
      tabs["mc"], tabs["ms1"], tabs["ms2"], tabs["dc"], tabs["ds1"], tabs["ds2"])


def _attend(jobs, s_ref):
    n = len(jobs[0][1])
    total = len(jobs) * n
    ahead = total if n == 1 else n + min(2, n - 1)
    assert n > 1 or total * CK <= s_ref.shape[2]
    m_run = [None] * len(jobs)

    def cols(j, c):
        return (0, j * CK) if n == 1 else (j % 2, c * CK)

    def score(t):
        j, c = divmod(t, n)
        qh, chunks = jobs[j]
        s = _dot_nt(qh, chunks[c][0]())
        if chunks[c][2] is not None:
            s = s + chunks[c][2]()
        slot, c0 = cols(j, c)
        s_ref[slot, :, c0:c0 + CK] = s
        for b in range(CK // 128):
            blk = s[:, b * 128:(b + 1) * 128]
            m_run[j] = blk if m_run[j] is None else jnp.maximum(m_run[j], blk)

    for t in range(min(ahead, total)):
        score(t)
    outs = []
    for j in range(len(jobs)):
        m = jnp.broadcast_to(jnp.max(m_run[j], axis=-1, keepdims=True), (TM, 128))
        acc = None
        for c in range(n):
            slot, c0 = cols(j, c)
            e = jnp.concatenate(
                [jnp.exp2(s_ref[slot, :, c0 + b * 128:c0 + (b + 1) * 128] - m) for b in range(CK // 128)], axis=-1)
            pv = _dot(e.astype(BF16), jobs[j][1][c][1]())
            acc = pv if acc is None else acc + pv
            if j * n + c + ahead < total:
                score(j * n + c + ahead)
        outs.append(acc[:, 0:V_HD] / acc[:, V_HD:V_HD + 1])
    return outs


def _attend_t(jobs, s_ref):
    n = len(jobs[0][1])
    total = len(jobs) * n
    ahead = total if n == 1 else n + min(2, n - 1)
    assert n > 1 or total * CK <= s_ref.shape[1]
    m_run = [None] * len(jobs)

    def rows(j, c):
        return (0, slice(j * CK, (j + 1) * CK)) if n == 1 else (j % 2, slice(c * CK, (c + 1) * CK))

    def score(t):
        j, c = divmod(t, n)
        qh, chunks = jobs[j]
        st = _dot_nt(chunks[c][0](), qh)
        slot, r = rows(j, c)
        s_ref[slot, r, :] = st
        m8 = jnp.max(st.reshape(CK // 8, 8, TM), axis=0)
        m_run[j] = m8 if m_run[j] is None else jnp.maximum(m_run[j], m8)

    for t in range(min(ahead, total)):
        score(t)
    outs = []
    for j in range(len(jobs)):
        m = jnp.max(m_run[j], axis=0, keepdims=True)
        acc = None
        for c in range(n):
            slot, r = rows(j, c)
            et = jnp.exp2(s_ref[slot, r, :] - m).astype(BF16)
            pv = _dot(jobs[j][1][c][1](), et)
            acc = pv if acc is None else acc + pv
            if j * n + c + ahead < total:
                score(j * n + c + ahead)
        outs.append(acc[0:V_HD, :] / acc[V_HD:V_HD + 1, :])
    return outs


def _kvt_chunks(k_ref, vt_ref, k_sl, h, n_chunks):
    return [(lambda c=c: k_ref[0, c * CK:(c + 1) * CK, k_sl],
             lambda c=c: vt_ref[0, h * V_AUG:(h + 1) * V_AUG, c * CK:(c + 1) * CK]) for c in range(n_chunks)]


def _kv_chunks(k_ref, v_ref, k_sl, v_sl, n_chunks, first=0, start=None, bias_fn=None):
    out = []
    for c in range(n_chunks):
        if start is None:
            rows = slice((first + c) * CK, (first + c + 1) * CK)
        else:
            rows = pl.ds(start + c * CK, CK)
        out.append((lambda rows=rows: k_ref[0, rows, k_sl],
                    lambda rows=rows: v_ref[0, rows, v_sl],
                    None if bias_fn is None else functools.partial(bias_fn, c)))
    return out


def _na_kernel(q_ref, k_ref, v_ref, tab_ref, o_ref, s_ref, *, rows):
    i = pl.program_id(1)
    r0 = (i - 1) * NA_R
    s0 = jnp.clip(r0 - NA_KH // 2, 0, rows - NA_W)

    def bias_chunk(h, c):
        lane = lax.broadcasted_iota(jnp.int32, (1, 2 * GRID_W), 1)
        row_blocks = []
        for qr in range(NA_R):
            q_row = r0 + qr
            lo = jnp.clip(q_row - NA_KH // 2, 0, rows - NA_KH)
            pieces = []
            for u in range(CK // (2 * GRID_W)):
                k_row = s0 + c * (CK // GRID_W) + 2 * u
                pen = [jnp.where((k_row + d >= lo) & (k_row + d < lo + NA_KH), 0.0, -1e30) for d in range(2)]
                idx = jnp.clip(k_row - q_row + (NA_KH - 1), -1, 2 * NA_KH - 1) + 1
                pieces.append(tab_ref[h, idx] + jnp.where(lane < GRID_W, pen[0], pen[1]))
            row_blocks.append(jnp.concatenate(pieces, axis=-1))
        return jnp.concatenate(row_blocks, axis=0)

    def run(window_start):
        jobs = []
        for h in range(HEADS):
            k_sl = slice(h * NA_HD, (h + 1) * NA_HD)
            v_sl = slice(h * V_AUG, (h + 1) * V_AUG)
            chunks = _kv_chunks(k_ref, v_ref, k_sl, v_sl, 1)
            if window_start is not None:
                chunks += _kv_chunks(k_ref, v_ref, k_sl, v_sl, NA_W * GRID_W // CK, start=window_start,
                                     bias_fn=functools.partial(bias_chunk, h))
            jobs.append((q_ref[0, :, k_sl], chunks))
        o_ref[0] = jnp.concatenate(_attend(jobs, s_ref), axis=-1).astype(BF16)

    @pl.when(i == 0)
    def _():
        run(None)

    @pl.when(i > 0)
    def _():
        run(pl.multiple_of(N_CTX + s0 * GRID_W, GRID_W))


def _na_bias_table(rel_bias):
    cq = np.arange(GRID_W)
    col_lo = np.clip(cq - NA_KW // 2, 0, GRID_W - NA_KW)
    col_ok = (cq[None, :] >= col_lo[:, None]) & (cq[None, :] < col_lo[:, None] + NA_KW)
    col_off = np.clip(cq[None, :] - cq[:, None], 1 - NA_KW, NA_KW - 1) + (NA_KW - 1)
    col_sel = jnp.asarray(col_off[..., None] == np.arange(2 * NA_KW - 1), F32)
    t1 = jnp.einsum("hab,qkb->haqk", rel_bias * LOG2E, col_sel, precision=lax.Precision.HIGHEST)
    t1 = jnp.where(col_ok[None, None], t1, -1e30)
    fill = jnp.full((HEADS, 1, GRID_W, GRID_W), -1e30, F32)
    ext = jnp.concatenate([fill, t1, fill, fill], axis=1)
    return jnp.concatenate([ext[:, :-1], ext[:, 1:]], axis=-1)


def _mla_kernel(q_ref, k_ref, v_ref, o_ref, s_ref):
    i = pl.program_id(1)

    def run(n_chunks):
        jobs = []
        for h in range(HEADS):
            sl = slice(h * MLA_QK_PAD, (h + 1) * MLA_QK_PAD)
            jobs.append((q_ref[0, :, sl], _kvt_chunks(k_ref, v_ref, sl, h, n_chunks)))
        o_ref[0] = jnp.concatenate(_attend_t(jobs, s_ref), axis=0).T.astype(BF16)

    @pl.when(i == 0)
    def _():
        run(1)

    @pl.when(i > 0)
    def _():
        run(k_ref.shape[1] // CK)


def _diff_kernel(q_ref, k_ref, v_ref, lam_ref, gsub_ref, o_ref, s_ref, *, lam_init):
    i = pl.program_id(1)
    lv = lam_ref[...]
    lam = (jnp.exp(jnp.sum(lv[0:1] * lv[1:2], axis=-1, keepdims=True))
           - jnp.exp(jnp.sum(lv[2:3] * lv[3:4], axis=-1, keepdims=True)) + lam_init)

    def run(n_chunks):
        first = lax.broadcasted_iota(jnp.int32, (TM, 2 * DIFF_QK), 1) < DIFF_QK
        jobs = []
        for h in range(HEADS):
            sl = slice(h * 2 * DIFF_QK, (h + 1) * 2 * DIFF_QK)
            chunks = _kvt_chunks(k_ref, v_ref, sl, h, n_chunks)
            qh = q_ref[0, :, sl]
            zero = jnp.zeros_like(qh)
            jobs += [(jnp.where(first, qh, zero), chunks), (jnp.where(first, zero, qh), chunks)]
        ot = _attend_t(jobs, s_ref)
        outs = []
        for h in range(HEADS):
            d = ot[2 * h] - lam * ot[2 * h + 1]
            ms = jnp.mean(d * d, axis=0, keepdims=True)
            outs.append(d * lax.rsqrt(ms + EPS) * gsub_ref[...] * (1.0 - lam_init))
        o_ref[0] = jnp.concatenate(outs, axis=0).T.astype(BF16)

    @pl.when(i == 0)
    def _():
        run(1)

    @pl.when(i > 0)
    def _():
        run(k_ref.shape[1] // CK)


def _attention_kernel(na_q, na_k, na_v, na_tab, mla_q, mla_k, mla_vt, diff_q, diff_k, diff_vt, lam, gsub,
                      na_o, mla_o, diff_o, na_s, mla_s, diff_s, *, rows, lam_init):
    _na_kernel(na_q, na_k, na_v, na_tab, na_o, na_s, rows=rows)
    _mla_kernel(mla_q, mla_k, mla_vt, mla_o, mla_s)
    _diff_kernel(diff_q, diff_k, diff_vt, lam, gsub, diff_o, diff_s, lam_init=lam_init)


def _attention(na_qkv, na_table, mla_qk, mla_vt, diff_qk, diff_vt, lam_vecs, g_sub, lam_init):
    bsz, t, _ = na_qkv.shape
    rows = (t - N_CTX) // GRID_W
    g_sub_t = jnp.broadcast_to(g_sub.reshape(DIFF_V, 1), (DIFF_V, TM))
    q_tile = lambda w_: pl.BlockSpec((1, TM, w_), lambda b, i: (b, i, 0))
    keys = lambda w_, j: pl.BlockSpec((1, t, w_), lambda b, i: (b, 0, j))
    vt_all = pl.BlockSpec((1, HEADS * V_AUG, t), lambda b, i: (b, 0, 0))
    out = jax.ShapeDtypeStruct((bsz, t, 256), BF16)
    return pl.pallas_call(
        functools.partial(_attention_kernel, rows=rows, lam_init=lam_init), grid=(bsz, t // TM),
        in_specs=[q_tile(256), keys(256, 1), keys(512, 1), _const_spec(na_table.shape),
                  q_tile(512), keys(512, 1), vt_all,
                  q_tile(256), keys(256, 1), vt_all, _const_spec((4, DIFF_QK)), _const_spec((DIFF_V, TM))],
        out_specs=[q_tile(256), q_tile(256), q_tile(256)],
        out_shape=[out, out, out],
        scratch_shapes=[pltpu.VMEM((2, TM, CK + NA_W * GRID_W), F32), pltpu.VMEM((2, t, TM), F32),
                        pltpu.VMEM((2, t, TM), F32)],
        compiler_params=_cparams(2), name="attention",
    )(na_qkv, na_qkv, na_qkv, na_table, mla_qk, mla_qk, mla_vt, diff_qk, diff_qk, diff_vt, lam_vecs, g_sub_t)


def _softplus(x):
    return jnp.maximum(x, 0.0) + jnp.log1p(jnp.exp(-jnp.abs(x)))


def _ssd_kernel(raw_ref, convw_ref, convb_ref, dtb_ref, alog_ref, dskip_ref, gnorm_ref,
                tril_ref, triu_ref, mlow_ref, mupp_ref, o_ref, xact_ref, yacc_ref, state_ref):
    t = raw_ref.shape[1]
    nt = t // TM
    xbc0 = SSD_INNER
    dt0 = SSD_INNER + SSD_XBC

    cw = convw_ref[...]
    cb = convb_ref[...]
    for j in range(nt):
        lo = j * TM
        cur = raw_ref[0, lo:lo + TM, xbc0:xbc0 + SSD_XBC]
        zeros8 = jnp.zeros((8, SSD_XBC), F32)
        prev = raw_ref[0, lo - 8:lo, xbc0:xbc0 + SSD_XBC] if j >= 2 else zeros8
        nxt = raw_ref[0, lo + TM:lo + TM + 8, xbc0:xbc0 + SSD_XBC] if 1 <= j < nt - 1 else zeros8
        u = jnp.concatenate([prev, cur, nxt], axis=0)
        acc = cb
        for kk in range(SSD_CONV):
            off = 8 - SSD_CONV // 2 + kk
            acc = acc + cw[kk:kk + 1, :] * u[off:off + TM, :]
        xact_ref[lo:lo + TM, :] = _silu(acc)

    a_pad = -jnp.exp(alog_ref[...])
    for d in range(2):
        tri_ref = tril_ref if d == 0 else triu_ref
        off_ref = mlow_ref if d == 0 else mupp_ref
        state_ref[...] = jnp.zeros_like(state_ref)

        def chunk(c, carry, d=d, tri_ref=tri_ref, off_ref=off_ref):
            if d == 0:
                blk = c
            else:
                blk = jnp.where(c == 0, 0, nt - c)
            off = pl.multiple_of(blk * TM, TM)
            rows = pl.ds(off, TM)
            dt = _softplus(raw_ref[0, rows, dt0:dt0 + 128] + dtb_ref[...])
            la = dt * a_pad
            cum = _split_dot_left(tri_ref[...], la)
            cum_t = cum.T
            dt_t = dt.T
            total = cum[TM - 1:TM, :] if d == 0 else cum[0:1, :]
            e_in = jnp.exp(cum)
            w_t = (jnp.exp(total - cum) * dt).T
            e_tot = jnp.exp(total)
            xs = xact_ref[rows, 0:SSD_INNER]
            xs_b = xs.astype(BF16)
            ys = []
            for g in range(SSD_GROUPS):
                bm = xact_ref[rows, SSD_INNER + g * SSD_STATE:SSD_INNER + (g + 1) * SSD_STATE]
                cm_b = xact_ref[rows, SSD_INNER + (SSD_GROUPS + g) * SSD_STATE:
                                SSD_INNER + (SSD_GROUPS + g + 1) * SSD_STATE].astype(BF16)
                gm = _dot_nt(cm_b, bm.astype(BF16))
                bm_t = bm.T
                for hh in range(HEADS // SSD_GROUPS):
                    h = g * (HEADS // SSD_GROUPS) + hh
                    j = d * HEADS + h
                    x_b = xs_b[:, h * SSD_HD:(h + 1) * SSD_HD]
                    dec = jnp.exp(cum[:, j:j + 1] - cum_t[j:j + 1, :] + off_ref[...])
                    y_d = _dot((gm * dec * dt_t[j:j + 1, :]).astype(BF16), x_b)
                    st = state_ref[h]
                    y_o = e_in[:, j:j + 1] * _dot(cm_b, st.astype(BF16))
                    new = _dot((bm_t * w_t[j:j + 1, :]).astype(BF16), x_b)
                    state_ref[h] = st * e_tot[:, j:j + 1] + new
                    ys.append(y_d + y_o + dskip_ref[d, h] * xs[:, h * SSD_HD:(h + 1) * SSD_HD])
            y = jnp.concatenate(ys, axis=-1)
            if d == 0:
                yacc_ref[rows, :] = y
            else:
                yacc_ref[rows, :] += y
            return carry

        lax.fori_loop(0, nt, chunk, 0)

    for j in range(nt):
        lo = j * TM
        y = yacc_ref[lo:lo + TM, :] * _silu(raw_ref[0, lo:lo + TM, 0:SSD_INNER])
        o_ref[0, lo:lo + TM, :] = _rms(y, gnorm_ref[...]).astype(BF16)


def _ssd(ssd_raw, lp, consts):
    bsz, t, _ = ssd_raw.shape
    c1 = lambda shape: pl.BlockSpec(shape, lambda b: (0,) * len(shape))
    return pl.pallas_call(
        _ssd_kernel, grid=(bsz,),
        in_specs=[pl.BlockSpec((1, t, SSD_W), lambda b: (b, 0, 0)),
                  c1((SSD_CONV, SSD_XBC)), c1((1, SSD_XBC)), c1((1, 128)), c1((1, 128)),
                  pl.BlockSpec(memory_space=pltpu.SMEM),
                  c1((1, SSD_INNER)), c1((TM, TM)), c1((TM, TM)), c1((TM, TM)), c1((TM, TM))],
        out_specs=pl.BlockSpec((1, t, SSD_INNER), lambda b: (b, 0, 0)),
        out_shape=jax.ShapeDtypeStruct((bsz, t, SSD_INNER), BF16),
        scratch_shapes=[pltpu.VMEM((t, SSD_XBC), F32), pltpu.VMEM((t, SSD_INNER), F32),
                        pltpu.VMEM((HEADS, SSD_STATE, SSD_HD), F32)],
        compiler_params=_cparams(1), name="ssd",
    )(ssd_raw, lp["ssd_convw"], lp["ssd_convb"], lp["ssd_dtb"], lp["ssd_alog"], lp["ssd_dskip"],
      lp["ssd_gnorm"], consts["tril"], consts["triu"], consts["mlow"], consts["mupp"])


def _outproj_kernel(na_ref, mla_ref, diff_ref, ssd_ref, h_ref, mod_ref, wout_ref, gffn_ref,
                    wrh_ref, wrl_ref, br_ref, lstrict_ref, ustrict_ref,
                    hout_ref, f_ref, route_ref, cnt_ref, meta_ref, run_ref):
    first = (pl.program_id(0) == 0) & (pl.program_id(1) == 0)

    @pl.when(first)
    def _():
        run_ref[...] = jnp.zeros_like(run_ref)

    o = (_dot(na_ref[0], wout_ref[0:256, :]) + _dot(mla_ref[0], wout_ref[256:512, :])
         + _dot(diff_ref[0], wout_ref[512:768, :]) + _dot(ssd_ref[0], wout_ref[768:1024, :]))
    gate = mod_ref[0, 0, 2:3, :]
    hn = h_ref[0] + gate * o
    hout_ref[0] = hn
    f = _rms(hn, gffn_ref[...]) * (1.0 + mod_ref[0, 0, 4:5, :]) + mod_ref[0, 0, 3:4, :]
    f_ref[0] = f

    f_hi = f.astype(BF16)
    f_lo = (f - f_hi.astype(F32)).astype(BF16)
    logits = _dot(f_hi, wrh_ref[...]) + _dot(f_lo, wrh_ref[...]) + _dot(f_hi, wrl_ref[...]) + br_ref[...]
    lane = lax.broadcasted_iota(jnp.int32, logits.shape, 1)
    lane_f = lane.astype(F32)
    neg = jnp.float32(-jnp.inf)
    big = jnp.float32(1e9)
    gl = jnp.where(lane < MOE_GROUPS, logits, neg)
    gmax = jnp.max(gl, axis=-1, keepdims=True)
    g_top_p = 1.0 / jnp.sum(jnp.exp(gl - gmax), axis=-1, keepdims=True)
    g_top = jnp.min(jnp.where(gl == gmax, lane_f, big), axis=-1, keepdims=True).astype(jnp.int32)
    in_group = (lane >= MOE_GROUPS) & (lane < MOE_GROUPS + MOE_EXPERTS) & (((lane - MOE_GROUPS) // MOE_EPG) == g_top)
    el = jnp.where(in_group, logits, neg)
    m1 = jnp.max(el, axis=-1, keepdims=True)
    i1 = jnp.min(jnp.where(el == m1, lane_f, big), axis=-1, keepdims=True)
    el2 = jnp.where(lane_f == i1, neg, el)
    m2 = jnp.max(el2, axis=-1, keepdims=True)
    i2 = jnp.min(jnp.where(el2 == m2, lane_f, big), axis=-1, keepdims=True)
    x2 = jnp.exp(m2 - m1)
    w1 = g_top_p / (1.0 + x2)
    w2 = g_top_p * x2 / (1.0 + x2)
    e1 = i1 - MOE_GROUPS
    e2 = i2 - MOE_GROUPS

    onehot = ((lane_f == e1) | (lane_f == e2)).astype(F32)
    tile_cnt = jnp.floor((jnp.sum(onehot, axis=0, keepdims=True) + (RUN_ALIGN - 1.0)) * (1.0 / RUN_ALIGN)) * RUN_ALIGN
    tile_start = _dot(jnp.broadcast_to(tile_cnt, (8, 128)).astype(BF16), ustrict_ref[...])[0:1]
    pos = _dot(lstrict_ref[...], onehot.astype(BF16)) + tile_start
    p1 = jnp.sum(jnp.where(lane_f == e1, pos, 0.0), axis=-1, keepdims=True)
    p2 = jnp.sum(jnp.where(lane_f == e2, pos, 0.0), axis=-1, keepdims=True)
    run_old = run_ref[...]
    run_ref[...] = run_old + tile_cnt
    cnt_ref[...] = run_old + tile_cnt
    route = jnp.zeros(logits.shape, F32)
    for idx, val in enumerate((p1, p2, w1, w2)):
        route = jnp.where(lane == idx, val, route)
    route_ref[0] = route
    sub = lax.broadcasted_iota(jnp.int32, (8, 128), 0)
    meta_ref[0] = jnp.where(sub == 0, tile_cnt, jnp.where(sub == 1, tile_start, jnp.where(sub == 2, run_old, 0.0)))


def _outproj(mix, h, modsel, wout, gffn, lp, consts):
    bsz, t, _ = h.shape
    row = lambda w_: pl.BlockSpec((1, TM, w_), lambda b, i: (b, i, 0))
    return pl.pallas_call(
        _outproj_kernel, grid=(bsz, t // TM),
        in_specs=[row(256), row(256), row(256), row(256), row(D),
                  pl.BlockSpec((1, 1, 6, D), lambda b, i: (b, jnp.minimum(i, 1), 0, 0)),
                  _const_spec((D, D)), _const_spec((1, D)), _const_spec((D, 128)), _const_spec((D, 128)),
                  _const_spec((1, 128)), _const_spec((TM, TM)), _const_spec((128, 128))],
        out_specs=[row(D), row(D), row(128), _const_spec((1, 128)),
                   pl.BlockSpec((1, 8, 128), lambda b, i: (b * (t // TM) + i, 0, 0))],
        out_shape=[jax.ShapeDtypeStruct((bsz, t, D), F32), jax.ShapeDtypeStruct((bsz, t, D), F32),
                   jax.ShapeDtypeStruct((bsz, t, 128), F32), jax.ShapeDtypeStruct((1, 128), F32),
                   jax.ShapeDtypeStruct((bsz * (t // TM), 8, 128), F32)],
        scratch_shapes=[pltpu.VMEM((1, 128), F32)],
        compiler_params=_cparams(2), name="outproj_router",
    )(*mix, h, modsel, wout, gffn, lp["wr_hi"], lp["wr_lo"], lp["br"], consts["lstrict"], consts["ustrict"])


def _run_dmas(tile, n_ref, ls_ref, gs_ref, make_copy):
    def walk(wait):
        for e in range(MOE_EXPERTS):
            n = n_ref[tile * MOE_EXPERTS + e]
            l0 = ls_ref[tile * MOE_EXPERTS + e]
            g0 = gs_ref[tile * MOE_EXPERTS + e]
            for b in range(RUN_BITS[1] - 1, RUN_BITS[0] - 1, -1):
                off = (n >> (b + 1)) << (b + 1)

                @pl.when(((n >> b) & 1) == 1)
                def _(b=b, off=off, l0=l0, g0=g0):
                    c = make_copy(pl.multiple_of(l0 + off, RUN_ALIGN), pl.multiple_of(g0 + off, RUN_ALIGN), 1 << b)
                    if wait:
                        c.wait()
                    else:
                        c.start(priority=b % 2)
    return walk


def _dispatch_kernel(n_ref, ls_ref, gs_ref, pe_ref, nu_ref, f_ref, route_ref, xs_ref, sbuf_ref, zbuf_ref,
                     sem, zsem, *, first_tail, n_blocks):
    tile = pl.program_id(0)

    def pad_copies(action):
        for e in range(MOE_EXPERTS):
            prev = pe_ref[e - 1] if e > 0 else 0

            @pl.when(pe_ref[e] > prev)
            def _(e=e):
                start = pl.multiple_of(pe_ref[e] - MOE_MB, MOE_MB)
                action(pltpu.make_async_copy(zbuf_ref, xs_ref.at[pl.ds(start, MOE_MB)], zsem))
        for blk in range(first_tail, n_blocks):
            @pl.when(blk >= nu_ref[0])
            def _(blk=blk):
                action(pltpu.make_async_copy(zbuf_ref, xs_ref.at[pl.ds(blk * MOE_MB, MOE_MB)], zsem))

    @pl.when(tile == 0)
    def _():
        zbuf_ref[...] = jnp.zeros_like(zbuf_ref)
        pad_copies(lambda c: c.start())
        pad_copies(lambda c: c.wait())

    buf = tile % 2
    pos = route_ref[...].T
    slot = lax.broadcasted_iota(jnp.int32, (SORT_ROWS, TM), 0).astype(F32)
    perm = ((slot == pos[0:1, :]) | (slot == pos[1:2, :])).astype(BF16)
    sbuf_ref[buf] = _dot(perm, f_ref[...].astype(BF16)).astype(BF16)

    def runs(t, s):
        return _run_dmas(t, n_ref, ls_ref, gs_ref, lambda l, g, size: pltpu.make_async_copy(
            sbuf_ref.at[s, pl.ds(l, size)], xs_ref.at[pl.ds(g, size)], sem.at[s]))

    runs(tile, buf)(False)

    @pl.when(tile >= 1)
    def _():
        runs(tile - 1, 1 - buf)(True)

    @pl.when(tile == pl.num_programs(0) - 1)
    def _():
        runs(tile, buf)(True)


def _dispatch(plan, f2d, route2d, cap):
    n_tok = f2d.shape[0]
    n_blocks = cap // MOE_MB
    first_tail = 2 * n_tok // MOE_MB
    assert 2 * TM + MOE_EXPERTS * (RUN_ALIGN - 1) <= SORT_ROWS
    return pl.pallas_call(
        functools.partial(_dispatch_kernel, first_tail=first_tail, n_blocks=n_blocks),
        grid_spec=pltpu.PrefetchScalarGridSpec(
            num_scalar_prefetch=5, grid=(n_tok // TM,),
            in_specs=[pl.BlockSpec((TM, D), lambda i, *_: (i, 0)),
                      pl.BlockSpec((TM, 128), lambda i, *_: (i, 0))],
            out_specs=pl.BlockSpec(memory_space=pl.ANY),
            scratch_shapes=[pltpu.VMEM((2, SORT_ROWS, D), BF16), pltpu.VMEM((MOE_MB, D), BF16),
                            pltpu.SemaphoreType.DMA((2,)), pltpu.SemaphoreType.DMA(())]),
        out_shape=jax.ShapeDtypeStruct((cap, D), BF16),
        compiler_params=_cparams(1), name="moe_dispatch",
    )(plan["n"], plan["ls"], plan["gs"], plan["pad_end"], plan["n_used"], f2d, route2d)


def _experts_kernel(be_ref, nb_ref, x_ref, wg_ref, wu_ref, wd_ref, y_ref, wgu_s, wd_s):
    i = pl.program_id(0)

    @pl.when((i == 0) | (be_ref[i] != be_ref[jnp.maximum(i - 1, 0)]))
    def _():
        wgu_s[:, 0:MOE_FF] = wg_ref[0, 0].astype(BF16)
        wgu_s[:, MOE_FF:2 * MOE_FF] = wu_ref[0, 0].astype(BF16)
        wd_s[...] = wd_ref[0, 0].astype(BF16)

    @pl.when(i < nb_ref[0])
    def _():
        gu = _dot(x_ref[...], wgu_s[...])
        a = _silu(gu[:, 0:MOE_FF]) * gu[:, MOE_FF:2 * MOE_FF]
        y_ref[...] = _dot(a.astype(BF16), wd_s[...]).astype(BF16)

    @pl.when(i >= nb_ref[0])
    def _():
        y_ref[...] = jnp.zeros_like(y_ref)


def _experts(plan, xs, w_gate, w_up, w_down, layer):
    cap = xs.shape[0]
    wspec = lambda a, b: pl.BlockSpec((1, 1, a, b), lambda i, be, nb: (layer, be[i], 0, 0))
    return pl.pallas_call(
        _experts_kernel,
        grid_spec=pltpu.PrefetchScalarGridSpec(
            num_scalar_prefetch=2, grid=(cap // MOE_MB,),
            in_specs=[pl.BlockSpec((MOE_MB, D), lambda i, be, nb: (i, 0)),
                      wspec(D, MOE_FF), wspec(D, MOE_FF), wspec(MOE_FF, D)],
            out_specs=pl.BlockSpec((MOE_MB, D), lambda i, be, nb: (i, 0)),
            scratch_shapes=[pltpu.VMEM((D, 2 * MOE_FF), BF16), pltpu.VMEM((MOE_FF, D), BF16)]),
        out_shape=jax.ShapeDtypeStruct((cap, D), BF16),
        compiler_params=_cparams(1), name="moe_experts",
    )(plan["block_e"], plan["n_used"], xs, w_gate, w_up, w_down)


def _combine_kernel(n_ref, ls_ref, gs_ref, h_ref, mod_ref, route_ref, y_ref, o_ref, ybuf_ref, sem, *, skip, nt):
    per_sample = pl.num_programs(1)
    step = pl.program_id(0) * per_sample + pl.program_id(1)
    buf = step % 2

    def runs(s, slot):
        tile = (s // per_sample) * nt + s % per_sample + skip
        return _run_dmas(tile, n_ref, ls_ref, gs_ref, lambda l, g, size: pltpu.make_async_copy(
            y_ref.at[pl.ds(g, size)], ybuf_ref.at[slot, pl.ds(l, size)], sem.at[slot]))

    @pl.when(step == 0)
    def _():
        ybuf_ref[...] = jnp.zeros_like(ybuf_ref)
        runs(step, buf)(False)

    @pl.when(step + 1 < pl.num_programs(0) * per_sample)
    def _():
        runs(step + 1, 1 - buf)(False)

    runs(step, buf)(True)

    slot = lax.broadcasted_iota(jnp.int32, (TM, SORT_ROWS), 1).astype(F32)
    r = route_ref[0]
    wm = (jnp.where(slot == r[:, 0:1], r[:, 2:3], 0.0) + jnp.where(slot == r[:, 1:2], r[:, 3:4], 0.0))
    w_hi = wm.astype(BF16)
    w_lo = (wm - w_hi.astype(F32)).astype(BF16)
    yv = ybuf_ref[buf]
    out = _dot(w_hi, yv) + _dot(w_lo, yv)
    o_ref[0] = h_ref[0] + mod_ref[0, 0, 5:6, :] * out


def _combine(plan, h, modsel, route, y, latent_only):
    bsz, t, _ = h.shape
    nt = t // TM
    skip = 1 if latent_only else 0
    row = lambda w_: pl.BlockSpec((1, TM, w_), lambda b, i, *_: (b, i + skip, 0))
    return pl.pallas_call(
        functools.partial(_combine_kernel, skip=skip, nt=nt),
        grid_spec=pltpu.PrefetchScalarGridSpec(
            num_scalar_prefetch=3, grid=(bsz, nt - skip),
            in_specs=[row(D),
                      pl.BlockSpec((1, 1, 6, D), lambda b, i, *_: (b, jnp.minimum(i + skip, 1), 0, 0)),
                      row(128),
                      pl.BlockSpec(memory_space=pl.ANY)],
            out_specs=pl.BlockSpec((1, TM, D), lambda b, i, *_: (b, i, 0)),
            scratch_shapes=[pltpu.VMEM((2, SORT_ROWS, D), BF16), pltpu.SemaphoreType.DMA((2,))]),
        out_shape=jax.ShapeDtypeStruct((bsz, t - skip * TM, D), F32),
        compiler_params=_cparams(2), name="moe_combine",
    )(plan["n"], plan["ls"], plan["gs"], h, modsel, route, y)


def _moe_plan(meta, counts, n_blocks):
    cnt = counts[0, :MOE_EXPERTS].astype(jnp.int32)
    padded = (cnt + MOE_MB - 1) // MOE_MB * MOE_MB
    pad_end = jnp.cumsum(padded)
    pad_start = pad_end - padded
    m = meta[:, 0:3, 0:MOE_EXPERTS].astype(jnp.int32)
    blk0 = jnp.arange(n_blocks, dtype=jnp.int32) * MOE_MB
    block_e = jnp.minimum(jnp.sum(blk0[:, None] >= pad_end[None, :], axis=-1), MOE_EXPERTS - 1)
    return {"n": m[:, 0].reshape(-1), "ls": m[:, 1].reshape(-1), "gs": (pad_start[None, :] + m[:, 2]).reshape(-1),
            "pad_end": pad_end.astype(jnp.int32), "n_used": (pad_end[-1:] // MOE_MB).astype(jnp.int32),
            "block_e": block_e.astype(jnp.int32)}


def _block_diag(n, seg):
    idx = np.arange(n) // seg
    return jnp.asarray(idx[:, None] == idx[None, :], BF16)


def _constants():
    lower = np.tril(np.ones((TM, TM), np.float32))
    upper = np.triu(np.ones((TM, TM), np.float32))
    return {"bd64": _block_diag(256, 64), "bd128": _block_diag(512, 128), "bd32": _block_diag(256, 32),
            "tril": jnp.asarray(lower, BF16), "triu": jnp.asarray(upper, BF16),
            "mlow": jnp.asarray((lower - 1.0) * 1e30, F32), "mupp": jnp.asarray((upper - 1.0) * 1e30, F32),
            "lstrict": jnp.asarray(np.tril(np.ones((TM, TM)), -1), BF16),
            "ustrict": jnp.asarray(np.triu(np.ones((128, 128)), 1), BF16)}


def _rope_tables(n_lat, t):
    n_freq = 8
    inv = jnp.power(10000.0, -jnp.arange(n_freq, dtype=F32) / n_freq)
    tok = jnp.arange(n_lat, dtype=jnp.int32)
    row = (tok // GRID_W).astype(F32)
    col = (tok % GRID_W).astype(F32)
    ang = jnp.concatenate([row[:, None] * inv, col[:, None] * inv], axis=-1)
    n_c = t - n_lat
    cos = jnp.concatenate([jnp.ones((n_c, 16), F32), jnp.cos(ang)], axis=0)
    sin = jnp.concatenate([jnp.zeros((n_c, 16), F32), jnp.sin(ang)], axis=0)
    z16 = jnp.zeros((t, 16), F32)
    one = lambda w_: jnp.ones((t, w_), F32)
    zero = lambda w_: jnp.zeros((t, w_), F32)
    mc = jnp.concatenate([one(64), cos, cos, one(32)], axis=-1)
    ms1 = jnp.concatenate([zero(64), -sin, z16, zero(32)], axis=-1)
    ms2 = jnp.concatenate([zero(64), z16, sin, zero(32)], axis=-1)
    dc = jnp.concatenate([cos, cos], axis=-1)
    ds1 = jnp.concatenate([-sin, z16], axis=-1)
    ds2 = jnp.concatenate([z16, sin], axis=-1)
    tile = lambda a, n: jnp.tile(a, (1, n))
    return {"mc": tile(mc, 4), "ms1": tile(ms1, 4), "ms2": tile(ms2, 4),
            "dc": tile(dc, 8), "ds1": tile(ds1, 8), "ds2": tile(ds2, 8)}


def _pad_heads(w, width, padded):
    lead = w.shape[:-1]
    w = w.reshape(lead + (HEADS, width))
    w = jnp.pad(w, [(0, 0)] * len(lead) + [(0, 0), (0, padded - width)])
    return w.reshape(lead + (HEADS * padded,))


def _pack_kernel(w_ref, o_ref):
    w = w_ref[0]
    z = lambda n: jnp.zeros((w.shape[0], n), F32)

    def value_heads(lo):
        out = []
        for h in range(HEADS):
            out += [w[:, lo + h * V_HD:lo + (h + 1) * V_HD], z(V_AUG - V_HD)]
        return out

    na = [w[:, 0:512]] + value_heads(512)
    mla = [w[:, 768:1152], z(MLA_NOPE), w[:, 1152:1184], z(MLA_QK_PAD - MLA_QK)]
    diff = [w[:, 1184:1696]] + value_heads(1696)
    ssd = [w[:, 1952:2984], z(SSD_W - 1032)]
    o_ref[0] = jnp.concatenate(na + mla + diff + ssd, axis=-1).astype(BF16)


def _pack_w_in(w_in):
    n_layers, _, n_in = w_in.shape
    return pl.pallas_call(
        _pack_kernel, grid=(n_layers, D // TM),
        in_specs=[pl.BlockSpec((1, TM, n_in), lambda l, i: (l, i, 0))],
        out_specs=pl.BlockSpec((1, TM, P_W), lambda l, i: (l, i, 0)),
        out_shape=jax.ShapeDtypeStruct((n_layers, D, P_W), BF16),
        compiler_params=_cparams(2), name="pack_w_in",
    )(w_in)


def _layer_params(l, p):
    row = lambda a: a.reshape(1, -1)
    t4 = lambda a: jnp.tile(a.reshape(1, -1), (1, HEADS))
    wqb = _pad_heads(p["mla_w_qb"][l], MLA_QK, MLA_QK_PAD)
    wkvb = p["mla_w_kvb"][l].reshape(MLA_KV_RANK, HEADS, MLA_NOPE + MLA_V)
    wkvb = jnp.concatenate([_pad_heads(wkvb[:, :, :MLA_NOPE].reshape(MLA_KV_RANK, -1), MLA_NOPE, MLA_QK_PAD),
                            _pad_heads(wkvb[:, :, MLA_NOPE:].reshape(MLA_KV_RANK, -1), MLA_V, V_AUG)], axis=-1)
    gpad = lambda g: jnp.tile(jnp.pad(g, (0, MLA_QK_PAD - MLA_QK)).reshape(1, -1), (1, HEADS))
    lane8 = lambda a: jnp.pad(a.reshape(1, -1), ((0, 0), (0, 128 - 2 * HEADS)))
    wr = jnp.concatenate([p["moe_w_group"][l], p["moe_w_expert"][l],
                          jnp.zeros((D, 128 - MOE_GROUPS - MOE_EXPERTS), F32)], axis=-1)
    wr_hi = wr.astype(BF16)
    br = jnp.concatenate([p["moe_b_group"][l], p["moe_b_expert"][l],
                          jnp.zeros((128 - MOE_GROUPS - MOE_EXPERTS,), F32)]).reshape(1, 128)
    return {
        "g_mix": row(p["g_mix"][l]), "g_ffn": row(p["g_ffn"][l]),
        "na_gq": t4(p["na_g_q"][l]), "na_gk": t4(p["na_g_k"][l]),
        "mla_gqa": row(p["mla_g_qa"][l]), "mla_wqb": wqb.astype(BF16),
        "mla_gkva": row(p["mla_g_kva"][l]), "mla_wkvb": wkvb.astype(BF16),
        "mla_gq": gpad(p["mla_g_q"][l]), "mla_gk": gpad(p["mla_g_k"][l]),
        "diff_gq": jnp.tile(p["diff_g_q"][l].reshape(1, -1), (1, 8)),
        "diff_gk": jnp.tile(p["diff_g_k"][l].reshape(1, -1), (1, 8)),
        "diff_lam": p["diff_lambda"][l], "diff_gsub": row(p["diff_g_sub"][l]),
        "ssd_convw": p["ssd_conv_w"][l], "ssd_convb": row(p["ssd_conv_b"][l]),
        "ssd_dtb": lane8(p["ssd_dt_bias"][l]), "ssd_alog": lane8(p["ssd_a_log"][l]),
        "ssd_dskip": p["ssd_d"][l], "ssd_gnorm": row(p["ssd_g_norm"][l]),
        "wr_hi": wr_hi, "wr_lo": (wr - wr_hi.astype(F32)).astype(BF16), "br": br,
    }


def _mixers(h, modsel, w_in_l, lp, consts, tabs, bias, lam_init):
    na_qkv, mla_qk, mla_vt, diff_qk, diff_vt, ssd_raw = _inproj(h, modsel, lp["g_mix"], w_in_l, consts, lp, tabs)
    return (*_attention(na_qkv, bias, mla_qk, mla_vt, diff_qk, diff_vt, lp["diff_lam"], lp["diff_gsub"], lam_init),
            _ssd(ssd_raw, lp, consts))


def _moe(hn, f, route, counts, meta, modsel, w_gate, w_up, w_down, layer, latent_only):
    bsz, t, _ = hn.shape
    n_asg = bsz * t * 2
    assert n_asg % MOE_MB == 0
    n_slots = n_asg + (bsz * t // TM) * MOE_EXPERTS * (RUN_ALIGN - 1)
    n_blocks = -(-n_slots // MOE_MB) + MOE_EXPERTS
    plan = _moe_plan(meta, counts, n_blocks)
    xs = _dispatch(plan, f.reshape(bsz * t, D), route.reshape(bsz * t, 128), n_blocks * MOE_MB)
    y = _experts(plan, xs, w_gate, w_up, w_down, layer)
    return _combine(plan, hn, modsel, route, y, latent_only)


def kernel(x, c, ctx, c_ctx, w_mod, b_mod, g_mix, w_in, w_out, na_g_q, na_g_k, na_rel_bias,
           mla_g_qa, mla_w_qb, mla_g_kva, mla_w_kvb, mla_g_q, mla_g_k,
           diff_g_q, diff_g_k, diff_lambda, diff_g_sub,
           ssd_conv_w, ssd_conv_b, ssd_dt_bias, ssd_a_log, ssd_d, ssd_g_norm,
           g_ffn, moe_w_group, moe_b_group, moe_w_expert, moe_b_expert, moe_w_gate, moe_w_up, moe_w_down):
    p = dict(g_mix=g_mix, g_ffn=g_ffn, na_g_q=na_g_q, na_g_k=na_g_k,
             mla_g_qa=mla_g_qa, mla_w_qb=mla_w_qb, mla_g_kva=mla_g_kva, mla_w_kvb=mla_w_kvb,
             mla_g_q=mla_g_q, mla_g_k=mla_g_k, diff_g_q=diff_g_q, diff_g_k=diff_g_k,
             diff_lambda=diff_lambda, diff_g_sub=diff_g_sub,
             ssd_conv_w=ssd_conv_w, ssd_conv_b=ssd_conv_b, ssd_dt_bias=ssd_dt_bias, ssd_a_log=ssd_a_log,
             ssd_d=ssd_d, ssd_g_norm=ssd_g_norm, moe_w_group=moe_w_group, moe_b_group=moe_b_group,
             moe_w_expert=moe_w_expert, moe_b_expert=moe_b_expert)
    bsz, n_lat, _ = x.shape
    n_ctx = ctx.shape[1]
    assert n_ctx == N_CTX == TM == CK and n_lat % TM == 0 and bsz < 16
    t = n_ctx + n_lat
    n_layers = w_mod.shape[0]
    rows = n_lat // GRID_W
    assert rows % NA_R == 0 and rows >= NA_W and (NA_W * GRID_W) % CK == 0

    consts = _constants()
    tabs = _rope_tables(n_lat, t)
    w_in_p = _pack_w_in(w_in)
    w_out_b = w_out.astype(BF16)

    cvec = jnp.concatenate([c, c_ctx[None, :], jnp.zeros((16 - bsz - 1, D), F32)], axis=0)
    mod = _modulation(cvec, w_mod, b_mod).reshape(n_layers, 16, 6, D)

    h = jnp.concatenate([ctx, x], axis=1)
    for l in range(n_layers):
        lp = _layer_params(l, p)
        modsel = jnp.stack([jnp.broadcast_to(mod[l, bsz][None], (bsz, 6, D)), mod[l, :bsz]], axis=1)
        lam_init = 0.8 - 0.6 * math.exp(-0.3 * l)
        bias = _na_bias_table(na_rel_bias[l])
        mix = _mixers(h, modsel, w_in_p[l], lp, consts, tabs, bias, lam_init)
        hn, f, route, counts, meta = _outproj(mix, h, modsel, w_out_b[l], lp["g_ffn"], lp, consts)
        h = _moe(hn, f, route, counts, meta, modsel, moe_w_gate, moe_w_up, moe_w_down, l,
                 latent_only=(l == n_layers - 1))
    return h
```

```python
import functools
import math

import numpy as np
import jax
import jax.numpy as jnp
from jax import lax
from jax.experimental import pallas as pl
from jax.experimental.pallas import tpu as pltpu

F32 = jnp.float32
BF16 = jnp.bfloat16

D = 1024
GRID_W = 64
N_CTX = 256
HEADS = 4
NA_HD = 64
NA_KH = 8
NA_KW = 16
MLA_NOPE = 64
MLA_ROPE = 32
MLA_QK = MLA_NOPE + MLA_ROPE
MLA_QK_PAD = 128
MLA_V = 64
MLA_Q_RANK = 256
MLA_KV_RANK = 128
DIFF_QK = 32
DIFF_V = 64
SSD_INNER = 256
SSD_HD = 64
SSD_STATE = 128
SSD_GROUPS = 2
SSD_CONV = 5
SSD_XBC = 768
MOE_GROUPS = 4
MOE_EPG = 4
MOE_EXPERTS = 16
MOE_FF = 512
EPS = 1e-6

TM = 256
MOE_MB = 512
NA_R = 4
NA_W = 12
V_HD = 64
V_AUG = 128
CK = 256
LOG2E = math.log2(math.e)
RUN_ALIGN = 16
RUN_BITS = (4, 9)
SORT_ROWS = 2 * TM + 256

P_NA = 0
P_MLA = 1024
P_DIFF = 1536
P_SSD = 2560
P_W = 3712
SSD_W = 1152

VMEM_LIMIT = 56 * 1024 * 1024


def _cparams(n_axes):
    return pltpu.CompilerParams(dimension_semantics=("arbitrary",) * n_axes,
                                vmem_limit_bytes=VMEM_LIMIT)


def _dot(a, b):
    return jnp.dot(a, b, preferred_element_type=F32)


def _dot_nt(a, b):
    return lax.dot_general(a, b, (((1,), (1,)), ((), ())), preferred_element_type=F32)


def _dot_tn(a, b):
    return lax.dot_general(a, b, (((0,), (0,)), ((), ())), preferred_element_type=F32)


def _split3(x):
    hi = x.astype(BF16)
    r1 = x - hi.astype(F32)
    mid = r1.astype(BF16)
    lo = (r1 - mid.astype(F32)).astype(BF16)
    return hi, mid, lo


def _split_dot(x, m):
    hi, mid, lo = _split3(x)
    return _dot(hi, m) + _dot(mid, m) + _dot(lo, m)


def _split_dot_left(m, x):
    hi, mid, lo = _split3(x)
    return _dot(m, hi) + _dot(m, mid) + _dot(m, lo)


def _rms(x, g):
    ms = jnp.mean(x * x, axis=-1, keepdims=True)
    return x * lax.rsqrt(ms + EPS) * g


def _seg_rms(x, bd, inv_n, g):
    x2 = x * x
    hi = x2.astype(BF16)
    lo = (x2 - hi.astype(F32)).astype(BF16)
    ms = (_dot(hi, bd) + _dot(lo, bd)) * inv_n
    return x * lax.rsqrt(ms + EPS) * g


def _silu(x):
    return x * jax.nn.sigmoid(x)


def _rope(x, c, s1, s2, width):
    rot = 16
    return x * c + pltpu.roll(x, width - rot, 1) * s1 + pltpu.roll(x, rot, 1) * s2


def _with_ones(v):
    lane = lax.broadcasted_iota(jnp.int32, v.shape, 1)
    return jnp.where(lane % V_AUG == V_HD, 1.0, v)


def _mod_kernel(c_ref, w_ref, b_ref, o_ref):
    s = _silu(c_ref[...])
    o_ref[0] = _dot(s.astype(BF16), w_ref[0].astype(BF16)) + b_ref[0]


def _modulation(cvec, w_mod, b_mod):
    n_layers = w_mod.shape[0]
    tn = 1536
    return pl.pallas_call(
        _mod_kernel,
        grid=(n_layers, 6 * D // tn),
        in_specs=[pl.BlockSpec((16, D), lambda l, j: (0, 0)),
                  pl.BlockSpec((1, D, tn), lambda l, j: (l, 0, j)),
                  pl.BlockSpec((1, 1, tn), lambda l, j: (l, 0, j))],
        out_specs=pl.BlockSpec((1, 16, tn), lambda l, j: (l, 0, j)),
        out_shape=jax.ShapeDtypeStruct((n_layers, 16, 6 * D), F32),
        compiler_params=_cparams(2),
        name="modulation",
    )(cvec, w_mod, b_mod.reshape(n_layers, 1, 6 * D))


def _inproj_kernel(h_ref, mod_ref, gmix_ref, w_ref, bd64_ref, bd128_ref, bd32_ref,
                   nagq_ref, nagk_ref, gqa_ref, wqb_ref, gkva_ref, wkvb_ref, mgq_ref, mgk_ref,
                   dgq_ref, dgk_ref, mc_ref, ms1_ref, ms2_ref, dc_ref, ds1_ref, ds2_ref,
                   na_ref, mla_ref, mlavt_ref, diff_ref, diffvt_ref, ssd_ref):
    x = h_ref[0]
    shift = mod_ref[0, 0, 0:1, :]
    scale = mod_ref[0, 0, 1:2, :]
    a = _rms(x, gmix_ref[...]) * (1.0 + scale) + shift
    p = _dot(a.astype(BF16), w_ref[...])


    bd64 = bd64_ref[...]
    q = p[:, P_NA:P_NA + 256]
    k = p[:, P_NA + 256:P_NA + 512]
    na_ref[0, :, 0:256] = (_seg_rms(q, bd64, 1.0 / NA_HD, nagq_ref[...]) * (NA_HD ** -0.5 * LOG2E)).astype(BF16)
    na_ref[0, :, 256:512] = _seg_rms(k, bd64, 1.0 / NA_HD, nagk_ref[...]).astype(BF16)
    na_ref[0, :, 512:1024] = _with_ones(p[:, P_NA + 512:P_NA + 1024]).astype(BF16)

    bd128 = bd128_ref[...]
    cq = p[:, P_MLA:P_MLA + 256]
    ckv = p[:, P_MLA + 256:P_MLA + 384]
    kr = p[:, P_MLA + 384:P_MLA + 512]
    q2 = _dot(_rms(cq, gqa_ref[...]).astype(BF16), wqb_ref[...])
    kv = _dot(_rms(ckv, gkva_ref[...]).astype(BF16), wkvb_ref[...])
    k2 = kv[:, 0:512] + jnp.concatenate([kr] * HEADS, axis=-1)
    mc, ms1, ms2 = mc_ref[...], ms1_ref[...], ms2_ref[...]
    qn = _rope(_seg_rms(q2, bd128, 1.0 / MLA_QK, mgq_ref[...]), mc, ms1, ms2, 512)
    kn = _rope(_seg_rms(k2, bd128, 1.0 / MLA_QK, mgk_ref[...]), mc, ms1, ms2, 512)
    mla_ref[0, :, 0:512] = (qn * (MLA_QK ** -0.5 * LOG2E)).astype(BF16)
    mla_ref[0, :, 512:1024] = kn.astype(BF16)
    mlavt_ref[0] = _with_ones(kv[:, 512:1024]).T.astype(BF16)

    bd32 = bd32_ref[...]
    dc, ds1, ds2 = dc_ref[...], ds1_ref[...], ds2_ref[...]
    dq = p[:, P_DIFF:P_DIFF + 256]
    dk = p[:, P_DIFF + 256:P_DIFF + 512]
    dqn = _rope(_seg_rms(dq, bd32, 1.0 / DIFF_QK, dgq_ref[...]), dc, ds1, ds2, 256)
    dkn = _rope(_seg_rms(dk, bd32, 1.0 / DIFF_QK, dgk_ref[...]), dc, ds1, ds2, 256)
    diff_ref[0, :, 0:256] = (dqn * (DIFF_QK ** -0.5 * LOG2E)).astype(BF16)
    diff_ref[0, :, 256:512] = dkn.astype(BF16)
    diffvt_ref[0] = _with_ones(p[:, P_DIFF + 512:P_DIFF + 1024]).T.astype(BF16)

    ssd_ref[0] = p[:, P_SSD:P_SSD + SSD_W]


def _const_spec(shape):
    nd = len(shape)
    return pl.BlockSpec(shape, lambda b, i: (0,) * nd)


def _stream_operand(h):
    if not isinstance(h, tuple):
        bsz, t, _ = h.shape
        return bsz, t, [h], [pl.BlockSpec((1, TM, D), lambda b, i, *_: (b, i, 0))], []
    ctx, x = h
    bsz, t = x.shape[0], ctx.shape[1] + x.shape[1]
    specs = [pl.BlockSpec((1, TM, D), lambda b, i, *_: (b, 0, 0)),
             pl.BlockSpec((1, TM, D), lambda b, i, *_: (b, jnp.maximum(i - 1, 0), 0))]
    return bsz, t, [ctx, x], specs, [pltpu.VMEM((1, TM, D), F32)]


def _merge_stream(kernel_fn, pos):
    def wrapped(*refs):
        refs = list(refs)
        hbuf = refs.pop()
        hbuf[0] = jnp.where(pl.program_id(1) == 0, refs[pos][0], refs[pos + 1][0])
        kernel_fn(*refs[:pos], hbuf, *refs[pos + 2:])
    return wrapped


def _inproj(h, modsel, gmix, w, consts, lp, tabs):
    bsz, t, h_args, h_specs, h_scratch = _stream_operand(h)
    nt = t // TM
    row = lambda w_: pl.BlockSpec((1, TM, w_), lambda b, i: (b, i, 0))
    tab = lambda w_: pl.BlockSpec((TM, w_), lambda b, i: (i, 0))
    in_specs = h_specs + [
        pl.BlockSpec((1, 1, 6, D), lambda b, i: (b, jnp.minimum(i, 1), 0, 0)),
        _const_spec((1, D)), _const_spec((D, P_W)),
        _const_spec((256, 256)), _const_spec((512, 512)), _const_spec((256, 256)),
        _const_spec((1, 256)), _const_spec((1, 256)),
        _const_spec((1, 256)), _const_spec((256, 512)), _const_spec((1, 128)), _const_spec((128, 1024)),
        _const_spec((1, 512)), _const_spec((1, 512)),
        _const_spec((1, 256)), _const_spec((1, 256)),
        tab(512), tab(512), tab(512), tab(256), tab(256), tab(256),
    ]
    col = lambda w_: pl.BlockSpec((1, w_, TM), lambda b, i: (b, 0, i))
    out_shape = [jax.ShapeDtypeStruct((bsz, t, 1024), BF16),
                 jax.ShapeDtypeStruct((bsz, t, 1024), BF16),
                 jax.ShapeDtypeStruct((bsz, HEADS * V_AUG, t), BF16),
                 jax.ShapeDtypeStruct((bsz, t, 512), BF16),
                 jax.ShapeDtypeStruct((bsz, HEADS * V_AUG, t), BF16),
                 jax.ShapeDtypeStruct((bsz, t, SSD_W), F32)]
    out_specs = [row(1024), row(1024), col(HEADS * V_AUG), row(512), col(HEADS * V_AUG), row(SSD_W)]
    kern = _merge_stream(_inproj_kernel, 0) if h_scratch else _inproj_kernel
    return pl.pallas_call(
        kern, grid=(bsz, nt), in_specs=in_specs, out_specs=out_specs, out_shape=out_shape,
        scratch_shapes=h_scratch, compiler_params=_cparams(2), name="inproj",
    )(*h_args, modsel, gmix, w, consts["bd64"], consts["bd128"], consts["bd32"],
      lp["na_gq"], lp["na_gk"], lp["mla_gqa"], lp["mla_wqb"], lp["mla_gkva"], lp["mla_wkvb"],
      lp["mla_gq"], lp["mla_gk"], lp["diff_gq"], lp["diff_gk"],
      tabs["mc"], tabs["ms1"], tabs["ms2"], tabs["dc"], tabs["ds1"], tabs["ds2"])


def _attend(jobs, s_ref):
    n = len(jobs[0][1])
    total = len(jobs) * n
    ahead = total if n == 1 else n + min(2, n - 1)
    assert n > 1 or total * CK <= s_ref.shape[2]
    m_run = [None] * len(jobs)

    def cols(j, c):
        return (0, j * CK) if n == 1 else (j % 2, c * CK)

    def score(t):
        j, c = divmod(t, n)
        qh, chunks = jobs[j]
        s = _dot_nt(qh, chunks[c][0]())
        if chunks[c][2] is not None:
            s = s + chunks[c][2]()
        slot, c0 = cols(j, c)
        s_ref[slot, :, c0:c0 + CK] = s
        for b in range(CK // 128):
            blk = s[:, b * 128:(b + 1) * 128]
            m_run[j] = blk if m_run[j] is None else jnp.maximum(m_run[j], blk)

    for t in range(min(ahead, total)):
        score(t)
    outs = []
    for j in range(len(jobs)):
        m = jnp.broadcast_to(jnp.max(m_run[j], axis=-1, keepdims=True), (TM, 128))
        acc = None
        for c in range(n):
            slot, c0 = cols(j, c)
            e = jnp.concatenate(
                [jnp.exp2(s_ref[slot, :, c0 + b * 128:c0 + (b + 1) * 128] - m) for b in range(CK // 128)], axis=-1)
            pv = _dot(e.astype(BF16), jobs[j][1][c][1]())
            acc = pv if acc is None else acc + pv
            if j * n + c + ahead < total:
                score(j * n + c + ahead)
        outs.append(acc[:, 0:V_HD] / acc[:, V_HD:V_HD + 1])
    return outs


def _attend_t(jobs, s_ref):
    n = len(jobs[0][1])
    total = len(jobs) * n
    ahead = total if n == 1 else n + min(2, n - 1)
    assert n > 1 or total * CK <= s_ref.shape[1]
    m_run = [None] * len(jobs)

    def rows(j, c):
        return (0, slice(j * CK, (j + 1) * CK)) if n == 1 else (j % 2, slice(c * CK, (c + 1) * CK))

    def score(t):
        j, c = divmod(t, n)
        qh, chunks = jobs[j]
        st = _dot_nt(chunks[c][0](), qh)
        slot, r = rows(j, c)
        s_ref[slot, r, :] = st
        m8 = jnp.max(st.reshape(CK // 8, 8, TM), axis=0)
        m_run[j] = m8 if m_run[j] is None else jnp.maximum(m_run[j], m8)

    for t in range(min(ahead, total)):
        score(t)
    outs = []
    for j in range(len(jobs)):
        m = jnp.max(m_run[j], axis=0, keepdims=True)
        acc = None
        for c in range(n):
            slot, r = rows(j, c)
            et = jnp.exp2(s_ref[slot, r, :] - m).astype(BF16)
            pv = _dot(jobs[j][1][c][1](), et)
            acc = pv if acc is None else acc + pv
            if j * n + c + ahead < total:
                score(j * n + c + ahead)
        outs.append(acc[0:V_HD, :] / acc[V_HD:V_HD + 1, :])
    return outs


def _kvt_chunks(k_ref, vt_ref, k_sl, h, n_chunks):
    return [(lambda c=c: k_ref[0, c * CK:(c + 1) * CK, k_sl],
             lambda c=c: vt_ref[0, h * V_AUG:(h + 1) * V_AUG, c * CK:(c + 1) * CK]) for c in range(n_chunks)]


def _kv_chunks(k_ref, v_ref, k_sl, v_sl, n_chunks, first=0, start=None, bias_fn=None):
    out = []
    for c in range(n_chunks):
        if start is None:
            rows = slice((first + c) * CK, (first + c + 1) * CK)
        else:
            rows = pl.ds(start + c * CK, CK)
        out.append((lambda rows=rows: k_ref[0, rows, k_sl],
                    lambda rows=rows: v_ref[0, rows, v_sl],
                    None if bias_fn is None else functools.partial(bias_fn, c)))
    return out


def _na_kernel(q_ref, k_ref, v_ref, tab_ref, o_ref, s_ref, *, rows):
    i = pl.program_id(1)
    r0 = (i - 1) * NA_R
    s0 = jnp.clip(r0 - NA_KH // 2, 0, rows - NA_W)

    def bias_chunk(h, c):
        lane = lax.broadcasted_iota(jnp.int32, (1, 2 * GRID_W), 1)
        row_blocks = []
        for qr in range(NA_R):
            q_row = r0 + qr
            lo = jnp.clip(q_row - NA_KH // 2, 0, rows - NA_KH)
            pieces = []
            for u in range(CK // (2 * GRID_W)):
                k_row = s0 + c * (CK // GRID_W) + 2 * u
                pen = [jnp.where((k_row + d >= lo) & (k_row + d < lo + NA_KH), 0.0, -1e30) for d in range(2)]
                idx = jnp.clip(k_row - q_row + (NA_KH - 1), -1, 2 * NA_KH - 1) + 1
                pieces.append(tab_ref[h, idx] + jnp.where(lane < GRID_W, pen[0], pen[1]))
            row_blocks.append(jnp.concatenate(pieces, axis=-1))
        return jnp.concatenate(row_blocks, axis=0)

    def run(window_start):
        jobs = []
        for h in range(HEADS):
            k_sl = slice(h * NA_HD, (h + 1) * NA_HD)
            v_sl = slice(h * V_AUG, (h + 1) * V_AUG)
            chunks = _kv_chunks(k_ref, v_ref, k_sl, v_sl, 1)
            if window_start is not None:
                chunks += _kv_chunks(k_ref, v_ref, k_sl, v_sl, NA_W * GRID_W // CK, start=window_start,
                                     bias_fn=functools.partial(bias_chunk, h))
            jobs.append((q_ref[0, :, k_sl], chunks))
        o_ref[0] = jnp.concatenate(_attend(jobs, s_ref), axis=-1).astype(BF16)

    @pl.when(i == 0)
    def _():
        run(None)

    @pl.when(i > 0)
    def _():
        run(pl.multiple_of(N_CTX + s0 * GRID_W, GRID_W))


def _na_bias_table(rel_bias):
    cq = np.arange(GRID_W)
    col_lo = np.clip(cq - NA_KW // 2, 0, GRID_W - NA_KW)
    col_ok = (cq[None, :] >= col_lo[:, None]) & (cq[None, :] < col_lo[:, None] + NA_KW)
    col_off = np.clip(cq[None, :] - cq[:, None], 1 - NA_KW, NA_KW - 1) + (NA_KW - 1)
    col_sel = jnp.asarray(col_off[..., None] == np.arange(2 * NA_KW - 1), F32)
    t1 = jnp.einsum("hab,qkb->haqk", rel_bias * LOG2E, col_sel, precision=lax.Precision.HIGHEST)
    t1 = jnp.where(col_ok[None, None], t1, -1e30)
    fill = jnp.full((HEADS, 1, GRID_W, GRID_W), -1e30, F32)
    ext = jnp.concatenate([fill, t1, fill, fill], axis=1)
    return jnp.concatenate([ext[:, :-1], ext[:, 1:]], axis=-1)


def _mla_kernel(q_ref, k_ref, v_ref, o_ref, s_ref):
    i = pl.program_id(1)

    def run(n_chunks):
        jobs = []
        for h in range(HEADS):
            sl = slice(h * MLA_QK_PAD, (h + 1) * MLA_QK_PAD)
            jobs.append((q_ref[0, :, sl], _kvt_chunks(k_ref, v_ref, sl, h, n_chunks)))
        o_ref[0] = jnp.concatenate(_attend_t(jobs, s_ref), axis=0).T.astype(BF16)

    @pl.when(i == 0)
    def _():
        run(1)

    @pl.when(i > 0)
    def _():
        run(k_ref.shape[1] // CK)


def _diff_kernel(q_ref, k_ref, v_ref, lam_ref, gsub_ref, o_ref, s_ref, *, lam_init):
    i = pl.program_id(1)
    lv = lam_ref[...]
    lam = (jnp.exp(jnp.sum(lv[0:1] * lv[1:2], axis=-1, keepdims=True))
           - jnp.exp(jnp.sum(lv[2:3] * lv[3:4], axis=-1, keepdims=True)) + lam_init)

    def run(n_chunks):
        first = lax.broadcasted_iota(jnp.int32, (TM, 2 * DIFF_QK), 1) < DIFF_QK
        jobs = []
        for h in range(HEADS):
            sl = slice(h * 2 * DIFF_QK, (h + 1) * 2 * DIFF_QK)
            chunks = _kvt_chunks(k_ref, v_ref, sl, h, n_chunks)
            qh = q_ref[0, :, sl]
            zero = jnp.zeros_like(qh)
            jobs += [(jnp.where(first, qh, zero), chunks), (jnp.where(first, zero, qh), chunks)]
        ot = _attend_t(jobs, s_ref)
        outs = []
        for h in range(HEADS):
            d = ot[2 * h] - lam * ot[2 * h + 1]
            ms = jnp.mean(d * d, axis=0, keepdims=True)
            outs.append(d * lax.rsqrt(ms + EPS) * gsub_ref[...] * (1.0 - lam_init))
        o_ref[0] = jnp.concatenate(outs, axis=0).T.astype(BF16)

    @pl.when(i == 0)
    def _():
        run(1)

    @pl.when(i > 0)
    def _():
        run(k_ref.shape[1] // CK)


def _attention_kernel(na_q, na_k, na_v, na_tab, mla_q, mla_k, mla_vt, diff_q, diff_k, diff_vt, lam, gsub,
                      na_o, mla_o, diff_o, na_s, mla_s, diff_s, *, rows, lam_init):
    _na_kernel(na_q, na_k, na_v, na_tab, na_o, na_s, rows=rows)
    _mla_kernel(mla_q, mla_k, mla_vt, mla_o, mla_s)
    _diff_kernel(diff_q, diff_k, diff_vt, lam, gsub, diff_o, diff_s, lam_init=lam_init)


def _attention(na_qkv, na_table, mla_qk, mla_vt, diff_qk, diff_vt, lam_vecs, g_sub, lam_init):
    bsz, t, _ = na_qkv.shape
    rows = (t - N_CTX) // GRID_W
    g_sub_t = jnp.broadcast_to(g_sub.reshape(DIFF_V, 1), (DIFF_V, TM))
    q_tile = lambda w_: pl.BlockSpec((1, TM, w_), lambda b, i: (b, i, 0))
    keys = lambda w_, j: pl.BlockSpec((1, t, w_), lambda b, i: (b, 0, j))
    vt_all = pl.BlockSpec((1, HEADS * V_AUG, t), lambda b, i: (b, 0, 0))
    out = jax.ShapeDtypeStruct((bsz, t, 256), BF16)
    return pl.pallas_call(
        functools.partial(_attention_kernel, rows=rows, lam_init=lam_init), grid=(bsz, t // TM),
        in_specs=[q_tile(256), keys(256, 1), keys(512, 1), _const_spec(na_table.shape),
                  q_tile(512), keys(512, 1), vt_all,
                  q_tile(256), keys(256, 1), vt_all, _const_spec((4, DIFF_QK)), _const_spec((DIFF_V, TM))],
        out_specs=[q_tile(256), q_tile(256), q_tile(256)],
        out_shape=[out, out, out],
        scratch_shapes=[pltpu.VMEM((2, TM, CK + NA_W * GRID_W), F32), pltpu.VMEM((2, t, TM), F32),
                        pltpu.VMEM((2, t, TM), F32)],
        compiler_params=_cparams(2), name="attention",
    )(na_qkv, na_qkv, na_qkv, na_table, mla_qk, mla_qk, mla_vt, diff_qk, diff_qk, diff_vt, lam_vecs, g_sub_t)


def _softplus(x):
    return jnp.maximum(x, 0.0) + jnp.log1p(jnp.exp(-jnp.abs(x)))


def _ssd_kernel(raw_ref, convw_ref, convb_ref, dtb_ref, alog_ref, dskip_ref, gnorm_ref,
                tril_ref, triu_ref, mlow_ref, mupp_ref, o_ref, xact_ref, yacc_ref, state_ref):
    t = raw_ref.shape[1]
    nt = t // TM
    xbc0 = SSD_INNER
    dt0 = SSD_INNER + SSD_XBC

    cw = convw_ref[...]
    cb = convb_ref[...]
    for j in range(nt):
        lo = j * TM
        cur = raw_ref[0, lo:lo + TM, xbc0:xbc0 + SSD_XBC]
        zeros8 = jnp.zeros((8, SSD_XBC), F32)
        prev = raw_ref[0, lo - 8:lo, xbc0:xbc0 + SSD_XBC] if j >= 2 else zeros8
        nxt = raw_ref[0, lo + TM:lo + TM + 8, xbc0:xbc0 + SSD_XBC] if 1 <= j < nt - 1 else zeros8
        u = jnp.concatenate([prev, cur, nxt], axis=0)
        acc = cb
        for kk in range(SSD_CONV):
            off = 8 - SSD_CONV // 2 + kk
            acc = acc + cw[kk:kk + 1, :] * u[off:off + TM, :]
        xact_ref[lo:lo + TM, :] = _silu(acc)

    a_pad = -jnp.exp(alog_ref[...])
    for d in range(2):
        tri_ref = tril_ref if d == 0 else triu_ref
        off_ref = mlow_ref if d == 0 else mupp_ref
        state_ref[...] = jnp.zeros_like(state_ref)

        def chunk(c, carry, d=d, tri_ref=tri_ref, off_ref=off_ref):
            if d == 0:
                blk = c
            else:
                blk = jnp.where(c == 0, 0, nt - c)
            off = pl.multiple_of(blk * TM, TM)
            rows = pl.ds(off, TM)
            dt = _softplus(raw_ref[0, rows, dt0:dt0 + 128] + dtb_ref[...])
            la = dt * a_pad
            cum = _split_dot_left(tri_ref[...], la)
            cum_t = cum.T
            dt_t = dt.T
            total = cum[TM - 1:TM, :] if d == 0 else cum[0:1, :]
            e_in = jnp.exp(cum)
            w_t = (jnp.exp(total - cum) * dt).T
            e_tot = jnp.exp(total)
            xs = xact_ref[rows, 0:SSD_INNER]
            xs_b = xs.astype(BF16)
            ys = []
            for g in range(SSD_GROUPS):
                bm = xact_ref[rows, SSD_INNER + g * SSD_STATE:SSD_INNER + (g + 1) * SSD_STATE]
                cm_b = xact_ref[rows, SSD_INNER + (SSD_GROUPS + g) * SSD_STATE:
                                SSD_INNER + (SSD_GROUPS + g + 1) * SSD_STATE].astype(BF16)
                gm = _dot_nt(cm_b, bm.astype(BF16))
                bm_t = bm.T
                for hh in range(HEADS // SSD_GROUPS):
                    h = g * (HEADS // SSD_GROUPS) + hh
                    j = d * HEADS + h
                    x_b = xs_b[:, h * SSD_HD:(h + 1) * SSD_HD]
                    dec = jnp.exp(cum[:, j:j + 1] - cum_t[j:j + 1, :] + off_ref[...])
                    y_d = _dot((gm * dec * dt_t[j:j + 1, :]).astype(BF16), x_b)
                    st = state_ref[h]
                    y_o = e_in[:, j:j + 1] * _dot(cm_b, st.astype(BF16))
                    new = _dot((bm_t * w_t[j:j + 1, :]).astype(BF16), x_b)
                    state_ref[h] = st * e_tot[:, j:j + 1] + new
                    ys.append(y_d + y_o + dskip_ref[d, h] * xs[:, h * SSD_HD:(h + 1) * SSD_HD])
            y = jnp.concatenate(ys, axis=-1)
            if d == 0:
                yacc_ref[rows, :] = y
            else:
                yacc_ref[rows, :] += y
            return carry

        lax.fori_loop(0, nt, chunk, 0)

    for j in range(nt):
        lo = j * TM
        y = yacc_ref[lo:lo + TM, :] * _silu(raw_ref[0, lo:lo + TM, 0:SSD_INNER])
        o_ref[0, lo:lo + TM, :] = _rms(y, gnorm_ref[...]).astype(BF16)


def _ssd(ssd_raw, lp, consts):
    bsz, t, _ = ssd_raw.shape
    c1 = lambda shape: pl.BlockSpec(shape, lambda b: (0,) * len(shape))
    return pl.pallas_call(
        _ssd_kernel, grid=(bsz,),
        in_specs=[pl.BlockSpec((1, t, SSD_W), lambda b: (b, 0, 0)),
                  c1((SSD_CONV, SSD_XBC)), c1((1, SSD_XBC)), c1((1, 128)), c1((1, 128)),
                  pl.BlockSpec(memory_space=pltpu.SMEM),
                  c1((1, SSD_INNER)), c1((TM, TM)), c1((TM, TM)), c1((TM, TM)), c1((TM, TM))],
        out_specs=pl.BlockSpec((1, t, SSD_INNER), lambda b: (b, 0, 0)),
        out_shape=jax.ShapeDtypeStruct((bsz, t, SSD_INNER), BF16),
        scratch_shapes=[pltpu.VMEM((t, SSD_XBC), F32), pltpu.VMEM((t, SSD_INNER), F32),
                        pltpu.VMEM((HEADS, SSD_STATE, SSD_HD), F32)],
        compiler_params=_cparams(1), name="ssd",
    )(ssd_raw, lp["ssd_convw"], lp["ssd_convb"], lp["ssd_dtb"], lp["ssd_alog"], lp["ssd_dskip"],
      lp["ssd_gnorm"], consts["tril"], consts["triu"], consts["mlow"], consts["mupp"])


def _outproj_kernel(na_ref, mla_ref, diff_ref, ssd_ref, h_ref, mod_ref, wout_ref, gffn_ref,
                    wrh_ref, wrl_ref, br_ref, lstrict_ref, ustrict_ref,
                    hout_ref, f_ref, route_ref, cnt_ref, meta_ref, run_ref):
    first = (pl.program_id(0) == 0) & (pl.program_id(1) == 0)

    @pl.when(first)
    def _():
        run_ref[...] = jnp.zeros_like(run_ref)

    o = (_dot(na_ref[0], wout_ref[0:256, :]) + _dot(mla_ref[0], wout_ref[256:512, :])
         + _dot(diff_ref[0], wout_ref[512:768, :]) + _dot(ssd_ref[0], wout_ref[768:1024, :]))
    gate = mod_ref[0, 0, 2:3, :]
    hn = h_ref[0] + gate * o
    hout_ref[0] = hn
    f = _rms(hn, gffn_ref[...]) * (1.0 + mod_ref[0, 0, 4:5, :]) + mod_ref[0, 0, 3:4, :]
    f_ref[0] = f.astype(BF16)

    f_hi = f.astype(BF16)
    f_lo = (f - f_hi.astype(F32)).astype(BF16)
    logits = _dot(f_hi, wrh_ref[...]) + _dot(f_lo, wrh_ref[...]) + _dot(f_hi, wrl_ref[...]) + br_ref[...]
    lane = lax.broadcasted_iota(jnp.int32, logits.shape, 1)
    lane_f = lane.astype(F32)
    neg = jnp.float32(-jnp.inf)
    big = jnp.float32(1e9)
    gl = jnp.where(lane < MOE_GROUPS, logits, neg)
    gmax = jnp.max(gl, axis=-1, keepdims=True)
    g_top_p = 1.0 / jnp.sum(jnp.exp(gl - gmax), axis=-1, keepdims=True)
    g_top = jnp.min(jnp.where(gl == gmax, lane_f, big), axis=-1, keepdims=True).astype(jnp.int32)
    in_group = (lane >= MOE_GROUPS) & (lane < MOE_GROUPS + MOE_EXPERTS) & (((lane - MOE_GROUPS) // MOE_EPG) == g_top)
    el = jnp.where(in_group, logits, neg)
    m1 = jnp.max(el, axis=-1, keepdims=True)
    i1 = jnp.min(jnp.where(el == m1, lane_f, big), axis=-1, keepdims=True)
    el2 = jnp.where(lane_f == i1, neg, el)
    m2 = jnp.max(el2, axis=-1, keepdims=True)
    i2 = jnp.min(jnp.where(el2 == m2, lane_f, big), axis=-1, keepdims=True)
    x2 = jnp.exp(m2 - m1)
    w1 = g_top_p / (1.0 + x2)
    w2 = g_top_p * x2 / (1.0 + x2)
    e1 = i1 - MOE_GROUPS
    e2 = i2 - MOE_GROUPS

    onehot = ((lane_f == e1) | (lane_f == e2)).astype(F32)
    tile_cnt = jnp.floor((jnp.sum(onehot, axis=0, keepdims=True) + (RUN_ALIGN - 1.0)) * (1.0 / RUN_ALIGN)) * RUN_ALIGN
    tile_start = _dot(jnp.broadcast_to(tile_cnt, (8, 128)).astype(BF16), ustrict_ref[...])[0:1]
    pos = _dot(lstrict_ref[...], onehot.astype(BF16)) + tile_start
    p1 = jnp.sum(jnp.where(lane_f == e1, pos, 0.0), axis=-1, keepdims=True)
    p2 = jnp.sum(jnp.where(lane_f == e2, pos, 0.0), axis=-1, keepdims=True)
    run_old = run_ref[...]
    run_ref[...] = run_old + tile_cnt
    cnt_ref[...] = run_old + tile_cnt
    route = jnp.zeros(logits.shape, F32)
    for idx, val in enumerate((p1, p2, w1, w2)):
        route = jnp.where(lane == idx, val, route)
    route_ref[0] = route
    sub = lax.broadcasted_iota(jnp.int32, (8, 128), 0)
    meta_ref[0] = jnp.where(sub == 0, tile_cnt, jnp.where(sub == 1, tile_start, jnp.where(sub == 2, run_old, 0.0)))


def _outproj(mix, h, modsel, wout, gffn, lp, consts):
    bsz, t, h_args, h_specs, h_scratch = _stream_operand(h)
    row = lambda w_: pl.BlockSpec((1, TM, w_), lambda b, i: (b, i, 0))
    kern = _merge_stream(_outproj_kernel, 4) if h_scratch else _outproj_kernel
    return pl.pallas_call(
        kern, grid=(bsz, t // TM),
        in_specs=[row(256), row(256), row(256), row(256)] + h_specs + [
                  pl.BlockSpec((1, 1, 6, D), lambda b, i: (b, jnp.minimum(i, 1), 0, 0)),
                  _const_spec((D, D)), _const_spec((1, D)), _const_spec((D, 128)), _const_spec((D, 128)),
                  _const_spec((1, 128)), _const_spec((TM, TM)), _const_spec((128, 128))],
        out_specs=[row(D), row(D), row(128), _const_spec((1, 128)),
                   pl.BlockSpec((1, 8, 128), lambda b, i: (b * (t // TM) + i, 0, 0))],
        out_shape=[jax.ShapeDtypeStruct((bsz, t, D), F32), jax.ShapeDtypeStruct((bsz, t, D), BF16),
                   jax.ShapeDtypeStruct((bsz, t, 128), F32), jax.ShapeDtypeStruct((1, 128), F32),
                   jax.ShapeDtypeStruct((bsz * (t // TM), 8, 128), F32)],
        scratch_shapes=[pltpu.VMEM((1, 128), F32)] + h_scratch,
        compiler_params=_cparams(2), name="outproj_router",
    )(*mix, *h_args, modsel, wout, gffn, lp["wr_hi"], lp["wr_lo"], lp["br"], consts["lstrict"], consts["ustrict"])


def _run_dmas(tile, n_ref, ls_ref, gs_ref, make_copy):
    def walk(wait):
        for e in range(MOE_EXPERTS):
            n = n_ref[tile * MOE_EXPERTS + e]
            l0 = ls_ref[tile * MOE_EXPERTS + e]
            g0 = gs_ref[tile * MOE_EXPERTS + e]
            for b in range(RUN_BITS[1] - 1, RUN_BITS[0] - 1, -1):
                off = (n >> (b + 1)) << (b + 1)

                @pl.when(((n >> b) & 1) == 1)
                def _(b=b, off=off, l0=l0, g0=g0):
                    c = make_copy(pl.multiple_of(l0 + off, RUN_ALIGN), pl.multiple_of(g0 + off, RUN_ALIGN), 1 << b)
                    if wait:
                        c.wait()
                    else:
                        c.start(priority=b % 2)
    return walk


def _dispatch_kernel(n_ref, ls_ref, gs_ref, pe_ref, nu_ref, f_ref, route_ref, xs_ref, sbuf_ref, zbuf_ref,
                     sem, zsem, *, first_tail, n_blocks):
    tile = pl.program_id(0)

    def pad_copies(action):
        for e in range(MOE_EXPERTS):
            prev = pe_ref[e - 1] if e > 0 else 0

            @pl.when(pe_ref[e] > prev)
            def _(e=e):
                start = pl.multiple_of(pe_ref[e] - MOE_MB, MOE_MB)
                action(pltpu.make_async_copy(zbuf_ref, xs_ref.at[pl.ds(start, MOE_MB)], zsem))
        for blk in range(first_tail, n_blocks):
            @pl.when(blk >= nu_ref[0])
            def _(blk=blk):
                action(pltpu.make_async_copy(zbuf_ref, xs_ref.at[pl.ds(blk * MOE_MB, MOE_MB)], zsem))

    @pl.when(tile == 0)
    def _():
        zbuf_ref[...] = jnp.zeros_like(zbuf_ref)
        pad_copies(lambda c: c.start())
        pad_copies(lambda c: c.wait())

    buf = tile % 2
    pos = route_ref[...].T
    slot = lax.broadcasted_iota(jnp.int32, (SORT_ROWS, TM), 0).astype(F32)
    perm = ((slot == pos[0:1, :]) | (slot == pos[1:2, :])).astype(BF16)
    sbuf_ref[buf] = _dot(perm, f_ref[...]).astype(BF16)

    def runs(t, s):
        return _run_dmas(t, n_ref, ls_ref, gs_ref, lambda l, g, size: pltpu.make_async_copy(
            sbuf_ref.at[s, pl.ds(l, size)], xs_ref.at[pl.ds(g, size)], sem.at[s]))

    runs(tile, buf)(False)

    @pl.when(tile >= 1)
    def _():
        runs(tile - 1, 1 - buf)(True)

    @pl.when(tile == pl.num_programs(0) - 1)
    def _():
        runs(tile, buf)(True)


def _dispatch(plan, f2d, route2d, cap):
    n_tok = f2d.shape[0]
    n_blocks = cap // MOE_MB
    first_tail = 2 * n_tok // MOE_MB
    assert 2 * TM + MOE_EXPERTS * (RUN_ALIGN - 1) <= SORT_ROWS
    return pl.pallas_call(
        functools.partial(_dispatch_kernel, first_tail=first_tail, n_blocks=n_blocks),
        grid_spec=pltpu.PrefetchScalarGridSpec(
            num_scalar_prefetch=5, grid=(n_tok // TM,),
            in_specs=[pl.BlockSpec((TM, D), lambda i, *_: (i, 0)),
                      pl.BlockSpec((TM, 128), lambda i, *_: (i, 0))],
            out_specs=pl.BlockSpec(memory_space=pl.ANY),
            scratch_shapes=[pltpu.VMEM((2, SORT_ROWS, D), BF16), pltpu.VMEM((MOE_MB, D), BF16),
                            pltpu.SemaphoreType.DMA((2,)), pltpu.SemaphoreType.DMA(())]),
        out_shape=jax.ShapeDtypeStruct((cap, D), BF16),
        compiler_params=_cparams(1), name="moe_dispatch",
    )(plan["n"], plan["ls"], plan["gs"], plan["pad_end"], plan["n_used"], f2d, route2d)


def _experts_kernel(be_ref, nb_ref, x_ref, wg_ref, wu_ref, wd_ref, y_ref, wgu_s, wd_s):
    i = pl.program_id(0)

    @pl.when((i == 0) | (be_ref[i] != be_ref[jnp.maximum(i - 1, 0)]))
    def _():
        wgu_s[:, 0:MOE_FF] = wg_ref[0, 0].astype(BF16)
        wgu_s[:, MOE_FF:2 * MOE_FF] = wu_ref[0, 0].astype(BF16)
        wd_s[...] = wd_ref[0, 0].astype(BF16)

    @pl.when(i < nb_ref[0])
    def _():
        gu = _dot(x_ref[...], wgu_s[...])
        a = _silu(gu[:, 0:MOE_FF]) * gu[:, MOE_FF:2 * MOE_FF]
        y_ref[...] = _dot(a.astype(BF16), wd_s[...]).astype(BF16)

    @pl.when(i >= nb_ref[0])
    def _():
        y_ref[...] = jnp.zeros_like(y_ref)


def _experts(plan, xs, w_gate, w_up, w_down, layer):
    cap = xs.shape[0]
    wspec = lambda a, b: pl.BlockSpec((1, 1, a, b), lambda i, be, nb: (layer, be[i], 0, 0))
    return pl.pallas_call(
        _experts_kernel,
        grid_spec=pltpu.PrefetchScalarGridSpec(
            num_scalar_prefetch=2, grid=(cap // MOE_MB,),
            in_specs=[pl.BlockSpec((MOE_MB, D), lambda i, be, nb: (i, 0)),
                      wspec(D, MOE_FF), wspec(D, MOE_FF), wspec(MOE_FF, D)],
            out_specs=pl.BlockSpec((MOE_MB, D), lambda i, be, nb: (i, 0)),
            scratch_shapes=[pltpu.VMEM((D, 2 * MOE_FF), BF16), pltpu.VMEM((MOE_FF, D), BF16)]),
        out_shape=jax.ShapeDtypeStruct((cap, D), BF16),
        compiler_params=_cparams(1), name="moe_experts",
    )(plan["block_e"], plan["n_used"], xs, w_gate, w_up, w_down)


def _combine_kernel(n_ref, ls_ref, gs_ref, h_ref, mod_ref, route_ref, y_ref, o_ref, ybuf_ref, sem, *, skip, nt):
    per_sample = pl.num_programs(1)
    step = pl.program_id(0) * per_sample + pl.program_id(1)
    buf = step % 2

    def runs(s, slot):
        tile = (s // per_sample) * nt + s % per_sample + skip
        return _run_dmas(tile, n_ref, ls_ref, gs_ref, lambda l, g, size: pltpu.make_async_copy(
            y_ref.at[pl.ds(g, size)], ybuf_ref.at[slot, pl.ds(l, size)], sem.at[slot]))

    @pl.when(step == 0)
    def _():
        ybuf_ref[...] = jnp.zeros_like(ybuf_ref)
        runs(step, buf)(False)

    @pl.when(step + 1 < pl.num_programs(0) * per_sample)
    def _():
        runs(step + 1, 1 - buf)(False)

    runs(step, buf)(True)

    slot = lax.broadcasted_iota(jnp.int32, (TM, SORT_ROWS), 1).astype(F32)
    r = route_ref[0]
    wm = (jnp.where(slot == r[:, 0:1], r[:, 2:3], 0.0) + jnp.where(slot == r[:, 1:2], r[:, 3:4], 0.0))
    w_hi = wm.astype(BF16)
    w_lo = (wm - w_hi.astype(F32)).astype(BF16)
    yv = ybuf_ref[buf]
    out = _dot(w_hi, yv) + _dot(w_lo, yv)
    o_ref[0] = h_ref[0] + mod_ref[0, 0, 5:6, :] * out


def _combine(plan, h, modsel, route, y, latent_only):
    bsz, t, _ = h.shape
    nt = t // TM
    skip = 1 if latent_only else 0
    row = lambda w_: pl.BlockSpec((1, TM, w_), lambda b, i, *_: (b, i + skip, 0))
    return pl.pallas_call(
        functools.partial(_combine_kernel, skip=skip, nt=nt),
        grid_spec=pltpu.PrefetchScalarGridSpec(
            num_scalar_prefetch=3, grid=(bsz, nt - skip),
            in_specs=[row(D),
                      pl.BlockSpec((1, 1, 6, D), lambda b, i, *_: (b, jnp.minimum(i + skip, 1), 0, 0)),
                      row(128),
                      pl.BlockSpec(memory_space=pl.ANY)],
            out_specs=pl.BlockSpec((1, TM, D), lambda b, i, *_: (b, i, 0)),
            scratch_shapes=[pltpu.VMEM((2, SORT_ROWS, D), BF16), pltpu.SemaphoreType.DMA((2,))]),
        out_shape=jax.ShapeDtypeStruct((bsz, t - skip * TM, D), F32),
        compiler_params=_cparams(2), name="moe_combine",
    )(plan["n"], plan["ls"], plan["gs"], h, modsel, route, y)


def _moe_plan(meta, counts, n_blocks):
    cnt = counts[0, :MOE_EXPERTS].astype(jnp.int32)
    padded = (cnt + MOE_MB - 1) // MOE_MB * MOE_MB
    pad_end = jnp.cumsum(padded)
    pad_start = pad_end - padded
    m = meta[:, 0:3, 0:MOE_EXPERTS].astype(jnp.int32)
    blk0 = jnp.arange(n_blocks, dtype=jnp.int32) * MOE_MB
    block_e = jnp.minimum(jnp.sum(blk0[:, None] >= pad_end[None, :], axis=-1), MOE_EXPERTS - 1)
    return {"n": m[:, 0].reshape(-1), "ls": m[:, 1].reshape(-1), "gs": (pad_start[None, :] + m[:, 2]).reshape(-1),
            "pad_end": pad_end.astype(jnp.int32), "n_used": (pad_end[-1:] // MOE_MB).astype(jnp.int32),
            "block_e": block_e.astype(jnp.int32)}


def _block_diag(n, seg):
    idx = np.arange(n) // seg
    return jnp.asarray(idx[:, None] == idx[None, :], BF16)


def _constants():
    lower = np.tril(np.ones((TM, TM), np.float32))
    upper = np.triu(np.ones((TM, TM), np.float32))
    return {"bd64": _block_diag(256, 64), "bd128": _block_diag(512, 128), "bd32": _block_diag(256, 32),
            "tril": jnp.asarray(lower, BF16), "triu": jnp.asarray(upper, BF16),
            "mlow": jnp.asarray((lower - 1.0) * 1e30, F32), "mupp": jnp.asarray((upper - 1.0) * 1e30, F32),
            "lstrict": jnp.asarray(np.tril(np.ones((TM, TM)), -1), BF16),
            "ustrict": jnp.asarray(np.triu(np.ones((128, 128)), 1), BF16)}


def _rope_tables(n_lat, t):
    n_freq = 8
    inv = jnp.power(10000.0, -jnp.arange(n_freq, dtype=F32) / n_freq)
    tok = jnp.arange(n_lat, dtype=jnp.int32)
    row = (tok // GRID_W).astype(F32)
    col = (tok % GRID_W).astype(F32)
    ang = jnp.concatenate([row[:, None] * inv, col[:, None] * inv], axis=-1)
    n_c = t - n_lat
    cos = jnp.concatenate([jnp.ones((n_c, 16), F32), jnp.cos(ang)], axis=0)
    sin = jnp.concatenate([jnp.zeros((n_c, 16), F32), jnp.sin(ang)], axis=0)
    z16 = jnp.zeros((t, 16), F32)
    one = lambda w_: jnp.ones((t, w_), F32)
    zero = lambda w_: jnp.zeros((t, w_), F32)
    mc = jnp.concatenate([one(64), cos, cos, one(32)], axis=-1)
    ms1 = jnp.concatenate([zero(64), -sin, z16, zero(32)], axis=-1)
    ms2 = jnp.concatenate([zero(64), z16, sin, zero(32)], axis=-1)
    dc = jnp.concatenate([cos, cos], axis=-1)
    ds1 = jnp.concatenate([-sin, z16], axis=-1)
    ds2 = jnp.concatenate([z16, sin], axis=-1)
    tile = lambda a, n: jnp.tile(a, (1, n))
    return {"mc": tile(mc, 4), "ms1": tile(ms1, 4), "ms2": tile(ms2, 4),
            "dc": tile(dc, 8), "ds1": tile(ds1, 8), "ds2": tile(ds2, 8)}


def _pad_heads(w, width, padded):
    lead = w.shape[:-1]
    w = w.reshape(lead + (HEADS, width))
    w = jnp.pad(w, [(0, 0)] * len(lead) + [(0, 0), (0, padded - width)])
    return w.reshape(lead + (HEADS * padded,))


def _pack_kernel(w_ref, o_ref):
    w = w_ref[0]
    z = lambda n: jnp.zeros((w.shape[0], n), F32)

    def value_heads(lo):
        out = []
        for h in range(HEADS):
            out += [w[:, lo + h * V_HD:lo + (h + 1) * V_HD], z(V_AUG - V_HD)]
        return out

    na = [w[:, 0:512]] + value_heads(512)
    mla = [w[:, 768:1152], z(MLA_NOPE), w[:, 1152:1184], z(MLA_QK_PAD - MLA_QK)]
    diff = [w[:, 1184:1696]] + value_heads(1696)
    ssd = [w[:, 1952:2984], z(SSD_W - 1032)]
    o_ref[0] = jnp.concatenate(na + mla + diff + ssd, axis=-1).astype(BF16)


def _pack_w_in(w_in):
    n_layers, _, n_in = w_in.shape
    return pl.pallas_call(
        _pack_kernel, grid=(n_layers, D // TM),
        in_specs=[pl.BlockSpec((1, TM, n_in), lambda l, i: (l, i, 0))],
        out_specs=pl.BlockSpec((1, TM, P_W), lambda l, i: (l, i, 0)),
        out_shape=jax.ShapeDtypeStruct((n_layers, D, P_W), BF16),
        compiler_params=_cparams(2), name="pack_w_in",
    )(w_in)


def _layer_params(l, p):
    row = lambda a: a.reshape(1, -1)
    t4 = lambda a: jnp.tile(a.reshape(1, -1), (1, HEADS))
    wqb = _pad_heads(p["mla_w_qb"][l], MLA_QK, MLA_QK_PAD)
    wkvb = p["mla_w_kvb"][l].reshape(MLA_KV_RANK, HEADS, MLA_NOPE + MLA_V)
    wkvb = jnp.concatenate([_pad_heads(wkvb[:, :, :MLA_NOPE].reshape(MLA_KV_RANK, -1), MLA_NOPE, MLA_QK_PAD),
                            _pad_heads(wkvb[:, :, MLA_NOPE:].reshape(MLA_KV_RANK, -1), MLA_V, V_AUG)], axis=-1)
    gpad = lambda g: jnp.tile(jnp.pad(g, (0, MLA_QK_PAD - MLA_QK)).reshape(1, -1), (1, HEADS))
    lane8 = lambda a: jnp.pad(a.reshape(1, -1), ((0, 0), (0, 128 - 2 * HEADS)))
    wr = jnp.concatenate([p["moe_w_group"][l], p["moe_w_expert"][l],
                          jnp.zeros((D, 128 - MOE_GROUPS - MOE_EXPERTS), F32)], axis=-1)
    wr_hi = wr.astype(BF16)
    br = jnp.concatenate([p["moe_b_group"][l], p["moe_b_expert"][l],
                          jnp.zeros((128 - MOE_GROUPS - MOE_EXPERTS,), F32)]).reshape(1, 128)
    return {
        "g_mix": row(p["g_mix"][l]), "g_ffn": row(p["g_ffn"][l]),
        "na_gq": t4(p["na_g_q"][l]), "na_gk": t4(p["na_g_k"][l]),
        "mla_gqa": row(p["mla_g_qa"][l]), "mla_wqb": wqb.astype(BF16),
        "mla_gkva": row(p["mla_g_kva"][l]), "mla_wkvb": wkvb.astype(BF16),
        "mla_gq": gpad(p["mla_g_q"][l]), "mla_gk": gpad(p["mla_g_k"][l]),
        "diff_gq": jnp.tile(p["diff_g_q"][l].reshape(1, -1), (1, 8)),
        "diff_gk": jnp.tile(p["diff_g_k"][l].reshape(1, -1), (1, 8)),
        "diff_lam": p["diff_lambda"][l], "diff_gsub": row(p["diff_g_sub"][l]),
        "ssd_convw": p["ssd_conv_w"][l], "ssd_convb": row(p["ssd_conv_b"][l]),
        "ssd_dtb": lane8(p["ssd_dt_bias"][l]), "ssd_alog": lane8(p["ssd_a_log"][l]),
        "ssd_dskip": p["ssd_d"][l], "ssd_gnorm": row(p["ssd_g_norm"][l]),
        "wr_hi": wr_hi, "wr_lo": (wr - wr_hi.astype(F32)).astype(BF16), "br": br,
    }


def _mixers(h, modsel, w_in_l, lp, consts, tabs, bias, lam_init):
    na_qkv, mla_qk, mla_vt, diff_qk, diff_vt, ssd_raw = _inproj(h, modsel, lp["g_mix"], w_in_l, consts, lp, tabs)
    return (*_attention(na_qkv, bias, mla_qk, mla_vt, diff_qk, diff_vt, lp["diff_lam"], lp["diff_gsub"], lam_init),
            _ssd(ssd_raw, lp, consts))


def _moe(hn, f, route, counts, meta, modsel, w_gate, w_up, w_down, layer, latent_only):
    bsz, t, _ = hn.shape
    n_asg = bsz * t * 2
    assert n_asg % MOE_MB == 0
    n_slots = n_asg + (bsz * t // TM) * MOE_EXPERTS * (RUN_ALIGN - 1)
    n_blocks = -(-n_slots // MOE_MB) + MOE_EXPERTS
    plan = _moe_plan(meta, counts, n_blocks)
    xs = _dispatch(plan, f.reshape(bsz * t, D), route.reshape(bsz * t, 128), n_blocks * MOE_MB)
    y = _experts(plan, xs, w_gate, w_up, w_down, layer)
    return _combine(plan, hn, modsel, route, y, latent_only)


def kernel(x, c, ctx, c_ctx, w_mod, b_mod, g_mix, w_in, w_out, na_g_q, na_g_k, na_rel_bias,
           mla_g_qa, mla_w_qb, mla_g_kva, mla_w_kvb, mla_g_q, mla_g_k,
           diff_g_q, diff_g_k, diff_lambda, diff_g_sub,
           ssd_conv_w, ssd_conv_b, ssd_dt_bias, ssd_a_log, ssd_d, ssd_g_norm,
           g_ffn, moe_w_group, moe_b_group, moe_w_expert, moe_b_expert, moe_w_gate, moe_w_up, moe_w_down):
    p = dict(g_mix=g_mix, g_ffn=g_ffn, na_g_q=na_g_q, na_g_k=na_g_k,
             mla_g_qa=mla_g_qa, mla_w_qb=mla_w_qb, mla_g_kva=mla_g_kva, mla_w_kvb=mla_w_kvb,
             mla_g_q=mla_g_q, mla_g_k=mla_g_k, diff_g_q=diff_g_q, diff_g_k=diff_g_k,
             diff_lambda=diff_lambda, diff_g_sub=diff_g_sub,
             ssd_conv_w=ssd_conv_w, ssd_conv_b=ssd_conv_b, ssd_dt_bias=ssd_dt_bias, ssd_a_log=ssd_a_log,
             ssd_d=ssd_d, ssd_g_norm=ssd_g_norm, moe_w_group=moe_w_group, moe_b_group=moe_b_group,
             moe_w_expert=moe_w_expert, moe_b_expert=moe_b_expert)
    bsz, n_lat, _ = x.shape
    n_ctx = ctx.shape[1]
    assert n_ctx == N_CTX == TM == CK and n_lat % TM == 0 and bsz < 16
    t = n_ctx + n_lat
    n_layers = w_mod.shape[0]
    rows = n_lat // GRID_W
    assert rows % NA_R == 0 and rows >= NA_W and (NA_W * GRID_W) % CK == 0

    consts = _constants()
    tabs = _rope_tables(n_lat, t)
    w_in_p = _pack_w_in(w_in)
    w_out_b = w_out.astype(BF16)

    cvec = jnp.concatenate([c, c_ctx[None, :], jnp.zeros((16 - bsz - 1, D), F32)], axis=0)
    mod = _modulation(cvec, w_mod, b_mod).reshape(n_layers, 16, 6, D)

    h = (ctx, x)
    for l in range(n_layers):
        lp = _layer_params(l, p)
        modsel = jnp.stack([jnp.broadcast_to(mod[l, bsz][None], (bsz, 6, D)), mod[l, :bsz]], axis=1)
        lam_init = 0.8 - 0.6 * math.exp(-0.3 * l)
        bias = _na_bias_table(na_rel_bias[l])
        mix = _mixers(h, modsel, w_in_p[l], lp, consts, tabs, bias, lam_init)
        hn, f, route, counts, meta = _outproj(mix, h, modsel, w_out_b[l], lp["g_ffn"], lp, consts)
        h = _moe(hn, f, route, counts, meta, modsel, moe_w_gate, moe_w_up, moe_w_down, l,
                 latent_only=(l == n_layers - 1))
    return h
```

```python
import functools
import math

import numpy as np
import jax
import jax.numpy as jnp
from jax import lax
from jax.experimental import pallas as pl
from jax.experimental.pallas import tpu as pltpu

F32 = jnp.float32
BF16 = jnp.bfloat16

D = 1024
GRID_W = 64
N_CTX = 256
HEADS = 4
NA_HD = 64
NA_KH = 8
NA_KW = 16
MLA_NOPE = 64
MLA_ROPE = 32
MLA_QK = MLA_NOPE + MLA_ROPE
MLA_QK_PAD = 128
MLA_V = 64
MLA_Q_RANK = 256
MLA_KV_RANK = 128
DIFF_QK = 32
DIFF_V = 64
SSD_INNER = 256
SSD_HD = 64
SSD_STATE = 128
SSD_GROUPS = 2
SSD_CONV = 5
SSD_XBC = 768
MOE_GROUPS = 4
MOE_EPG = 4
MOE_EXPERTS = 16
MOE_FF = 512
EPS = 1e-6

TM = 256
MOE_MB = 512
NA_R = 4
NA_W = 12
V_HD = 64
V_AUG = 128
VT_ROWS = 80
CK = 256
LOG2E = math.log2(math.e)
RUN_ALIGN = 16
RUN_BITS = (4, 9)
SORT_ROWS = 2 * TM + 256

P_NA = 0
P_MLA = 1024
P_DIFF = 1536
P_SSD = 2560
P_W = 3712
SSD_W = 1152

VMEM_LIMIT = 56 * 1024 * 1024


def _cparams(n_axes):
    return pltpu.CompilerParams(dimension_semantics=("arbitrary",) * n_axes,
                                vmem_limit_bytes=VMEM_LIMIT)


def _dot(a, b):
    return jnp.dot(a, b, preferred_element_type=F32)


def _dot_nt(a, b):
    return lax.dot_general(a, b, (((1,), (1,)), ((), ())), preferred_element_type=F32)


def _dot_tn(a, b):
    return lax.dot_general(a, b, (((0,), (0,)), ((), ())), preferred_element_type=F32)


def _split3(x):
    hi = x.astype(BF16)
    r1 = x - hi.astype(F32)
    mid = r1.astype(BF16)
    lo = (r1 - mid.astype(F32)).astype(BF16)
    return hi, mid, lo


def _split_dot(x, m):
    hi, mid, lo = _split3(x)
    return _dot(hi, m) + _dot(mid, m) + _dot(lo, m)


def _split_dot_left(m, x):
    hi, mid, lo = _split3(x)
    return _dot(m, hi) + _dot(m, mid) + _dot(m, lo)


def _rms(x, g):
    ms = jnp.mean(x * x, axis=-1, keepdims=True)
    return x * lax.rsqrt(ms + EPS) * g


def _seg_rms(x, bd, inv_n, g):
    x2 = x * x
    hi = x2.astype(BF16)
    lo = (x2 - hi.astype(F32)).astype(BF16)
    ms = (_dot(hi, bd) + _dot(lo, bd)) * inv_n
    return x * lax.rsqrt(ms + EPS) * g


def _silu(x):
    return x * jax.nn.sigmoid(x)


def _rope(x, c, s1, s2, width):
    rot = 16
    return x * c + pltpu.roll(x, width - rot, 1) * s1 + pltpu.roll(x, rot, 1) * s2


def _with_ones(v):
    lane = lax.broadcasted_iota(jnp.int32, v.shape, 1)
    return jnp.where(lane % V_AUG == V_HD, 1.0, v)


def _values_t(v):
    vt = _with_ones(v).T
    return jnp.concatenate([vt[h * V_AUG:h * V_AUG + VT_ROWS] for h in range(HEADS)], axis=0).astype(BF16)


def _mod_kernel(c_ref, w_ref, b_ref, o_ref):
    s = _silu(c_ref[...])
    o_ref[0] = _dot(s.astype(BF16), w_ref[0].astype(BF16)) + b_ref[0]


def _modulation(cvec, w_mod, b_mod):
    n_layers = w_mod.shape[0]
    tn = 1536
    return pl.pallas_call(
        _mod_kernel,
        grid=(n_layers, 6 * D // tn),
        in_specs=[pl.BlockSpec((16, D), lambda l, j: (0, 0)),
                  pl.BlockSpec((1, D, tn), lambda l, j: (l, 0, j)),
                  pl.BlockSpec((1, 1, tn), lambda l, j: (l, 0, j))],
        out_specs=pl.BlockSpec((1, 16, tn), lambda l, j: (l, 0, j)),
        out_shape=jax.ShapeDtypeStruct((n_layers, 16, 6 * D), F32),
        compiler_params=_cparams(2),
        name="modulation",
    )(cvec, w_mod, b_mod.reshape(n_layers, 1, 6 * D))


def _inproj_kernel(h_ref, mod_ref, gmix_ref, w_ref, bd64_ref, bd128_ref, bd32_ref,
                   nagq_ref, nagk_ref, gqa_ref, wqb_ref, gkva_ref, wkvb_ref, mgq_ref, mgk_ref,
                   dgq_ref, dgk_ref, mc_ref, ms1_ref, ms2_ref, dc_ref, ds1_ref, ds2_ref,
                   na_ref, mla_ref, mlavt_ref, diff_ref, diffvt_ref, ssd_ref):
    x = h_ref[0]
    shift = mod_ref[0, 0, 0:1, :]
    scale = mod_ref[0, 0, 1:2, :]
    a = _rms(x, gmix_ref[...]) * (1.0 + scale) + shift
    p = _dot(a.astype(BF16), w_ref[...])


    bd64 = bd64_ref[...]
    q = p[:, P_NA:P_NA + 256]
    k = p[:, P_NA + 256:P_NA + 512]
    na_ref[0, :, 0:256] = (_seg_rms(q, bd64, 1.0 / NA_HD, nagq_ref[...]) * (NA_HD ** -0.5 * LOG2E)).astype(BF16)
    na_ref[0, :, 256:512] = _seg_rms(k, bd64, 1.0 / NA_HD, nagk_ref[...]).astype(BF16)
    na_ref[0, :, 512:1024] = _with_ones(p[:, P_NA + 512:P_NA + 1024]).astype(BF16)

    bd128 = bd128_ref[...]
    cq = p[:, P_MLA:P_MLA + 256]
    ckv = p[:, P_MLA + 256:P_MLA + 384]
    kr = p[:, P_MLA + 384:P_MLA + 512]
    q2 = _dot(_rms(cq, gqa_ref[...]).astype(BF16), wqb_ref[...])
    kv = _dot(_rms(ckv, gkva_ref[...]).astype(BF16), wkvb_ref[...])
    k2 = kv[:, 0:512] + jnp.concatenate([kr] * HEADS, axis=-1)
    mc, ms1, ms2 = mc_ref[...], ms1_ref[...], ms2_ref[...]
    qn = _rope(_seg_rms(q2, bd128, 1.0 / MLA_QK, mgq_ref[...]), mc, ms1, ms2, 512)
    kn = _rope(_seg_rms(k2, bd128, 1.0 / MLA_QK, mgk_ref[...]), mc, ms1, ms2, 512)
    mla_ref[0, :, 0:512] = (qn * (MLA_QK ** -0.5 * LOG2E)).astype(BF16)
    mla_ref[0, :, 512:1024] = kn.astype(BF16)
    mlavt_ref[0] = _values_t(kv[:, 512:1024])

    bd32 = bd32_ref[...]
    dc, ds1, ds2 = dc_ref[...], ds1_ref[...], ds2_ref[...]
    dq = p[:, P_DIFF:P_DIFF + 256]
    dk = p[:, P_DIFF + 256:P_DIFF + 512]
    dqn = _rope(_seg_rms(dq, bd32, 1.0 / DIFF_QK, dgq_ref[...]), dc, ds1, ds2, 256)
    dkn = _rope(_seg_rms(dk, bd32, 1.0 / DIFF_QK, dgk_ref[...]), dc, ds1, ds2, 256)
    diff_ref[0, :, 0:256] = (dqn * (DIFF_QK ** -0.5 * LOG2E)).astype(BF16)
    diff_ref[0, :, 256:512] = dkn.astype(BF16)
    diffvt_ref[0] = _values_t(p[:, P_DIFF + 512:P_DIFF + 1024])

    ssd_ref[0] = p[:, P_SSD:P_SSD + SSD_W]


def _const_spec(shape):
    nd = len(shape)
    return pl.BlockSpec(shape, lambda b, i: (0,) * nd)


def _stream_operand(h):
    if not isinstance(h, tuple):
        bsz, t, _ = h.shape
        return bsz, t, [h], [pl.BlockSpec((1, TM, D), lambda b, i, *_: (b, i, 0))], []
    ctx, x = h
    bsz, t = x.shape[0], ctx.shape[1] + x.shape[1]
    specs = [pl.BlockSpec((1, TM, D), lambda b, i, *_: (b, 0, 0)),
             pl.BlockSpec((1, TM, D), lambda b, i, *_: (b, jnp.maximum(i - 1, 0), 0))]
    return bsz, t, [ctx, x], specs, [pltpu.VMEM((1, TM, D), F32)]


def _merge_stream(kernel_fn, pos):
    def wrapped(*refs):
        refs = list(refs)
        hbuf = refs.pop()
        hbuf[0] = jnp.where(pl.program_id(1) == 0, refs[pos][0], refs[pos + 1][0])
        kernel_fn(*refs[:pos], hbuf, *refs[pos + 2:])
    return wrapped


def _inproj(h, modsel, gmix, w, consts, lp, tabs):
    bsz, t, h_args, h_specs, h_scratch = _stream_operand(h)
    nt = t // TM
    row = lambda w_: pl.BlockSpec((1, TM, w_), lambda b, i: (b, i, 0))
    tab = lambda w_: pl.BlockSpec((TM, w_), lambda b, i: (i, 0))
    in_specs = h_specs + [
        pl.BlockSpec((1, 1, 6, D), lambda b, i: (b, jnp.minimum(i, 1), 0, 0)),
        _const_spec((1, D)), _const_spec((D, P_W)),
        _const_spec((256, 256)), _const_spec((512, 512)), _const_spec((256, 256)),
        _const_spec((1, 256)), _const_spec((1, 256)),
        _const_spec((1, 256)), _const_spec((256, 512)), _const_spec((1, 128)), _const_spec((128, 1024)),
        _const_spec((1, 512)), _const_spec((1, 512)),
        _const_spec((1, 256)), _const_spec((1, 256)),
        tab(512), tab(512), tab(512), tab(256), tab(256), tab(256),
    ]
    col = lambda w_: pl.BlockSpec((1, w_, TM), lambda b, i: (b, 0, i))
    out_shape = [jax.ShapeDtypeStruct((bsz, t, 1024), BF16),
                 jax.ShapeDtypeStruct((bsz, t, 1024), BF16),
                 jax.ShapeDtypeStruct((bsz, HEADS * VT_ROWS, t), BF16),
                 jax.ShapeDtypeStruct((bsz, t, 512), BF16),
                 jax.ShapeDtypeStruct((bsz, HEADS * VT_ROWS, t), BF16),
                 jax.ShapeDtypeStruct((bsz, t, SSD_W), F32)]
    out_specs = [row(1024), row(1024), col(HEADS * VT_ROWS), row(512), col(HEADS * VT_ROWS), row(SSD_W)]
    kern = _merge_stream(_inproj_kernel, 0) if h_scratch else _inproj_kernel
    return pl.pallas_call(
        kern, grid=(bsz, nt), in_specs=in_specs, out_specs=out_specs, out_shape=out_shape,
        scratch_shapes=h_scratch, compiler_params=_cparams(2), name="inproj",
    )(*h_args, modsel, gmix, w, consts["bd64"], consts["bd128"], consts["bd32"],
      lp["na_gq"], lp["na_gk"], lp["mla_gqa"], lp["mla_wqb"], lp["mla_gkva"], lp["mla_wkvb"],
      lp["mla_gq"], lp["mla_gk"], lp["diff_gq"], lp["diff_gk"],
      tabs["mc"], tabs["ms1"], tabs["ms2"], tabs["dc"], tabs["ds1"], tabs["ds2"])


def _attend(jobs, s_ref):
    n = len(jobs[0][1])
    total = len(jobs) * n
    ahead = total if n == 1 else n + min(6, n - 1)
    assert n > 1 or total * CK <= s_ref.shape[2]
    m_run = [None] * len(jobs)

    def cols(j, c):
        return (0, j * CK) if n == 1 else (j % 2, c * CK)

    def score(t):
        j, c = divmod(t, n)
        qh, chunks = jobs[j]
        s = _dot_nt(qh, chunks[c][0]())
        if chunks[c][2] is not None:
            s = s + chunks[c][2]()
        slot, c0 = cols(j, c)
        s_ref[slot, :, c0:c0 + CK] = s
        for b in range(CK // 128):
            blk = s[:, b * 128:(b + 1) * 128]
            m_run[j] = blk if m_run[j] is None else jnp.maximum(m_run[j], blk)

    for t in range(min(ahead, total)):
        score(t)
    outs = []
    for j in range(len(jobs)):
        m = jnp.broadcast_to(jnp.max(m_run[j], axis=-1, keepdims=True), (TM, 128))
        acc = None
        for c in range(n):
            slot, c0 = cols(j, c)
            e = jnp.concatenate(
                [jnp.exp2(s_ref[slot, :, c0 + b * 128:c0 + (b + 1) * 128] - m) for b in range(CK // 128)], axis=-1)
            pv = _dot(e.astype(BF16), jobs[j][1][c][1]())
            acc = pv if acc is None else acc + pv
            if j * n + c + ahead < total:
                score(j * n + c + ahead)
        outs.append(acc[:, 0:V_HD] / acc[:, V_HD:V_HD + 1])
    return outs


def _attend_t(jobs, s_ref):
    n = len(jobs[0][1])
    total = len(jobs) * n
    ahead = total if n == 1 else n + min(6, n - 1)
    assert n > 1 or total * CK <= s_ref.shape[1]
    m_run = [None] * len(jobs)

    def rows(j, c):
        return (0, slice(j * CK, (j + 1) * CK)) if n == 1 else (j % 2, slice(c * CK, (c + 1) * CK))

    def score(t):
        j, c = divmod(t, n)
        qh, chunks = jobs[j]
        st = _dot_nt(chunks[c][0](), qh)
        slot, r = rows(j, c)
        s_ref[slot, r, :] = st
        m8 = jnp.max(st.reshape(CK // 8, 8, TM), axis=0)
        m_run[j] = m8 if m_run[j] is None else jnp.maximum(m_run[j], m8)

    for t in range(min(ahead, total)):
        score(t)
    outs = []
    for j in range(len(jobs)):
        m = jnp.max(m_run[j], axis=0, keepdims=True)
        acc = None
        for c in range(n):
            slot, r = rows(j, c)
            et = jnp.exp2(s_ref[slot, r, :] - m).astype(BF16)
            pv = _dot(jobs[j][1][c][1](), et)
            acc = pv if acc is None else acc + pv
            if j * n + c + ahead < total:
                score(j * n + c + ahead)
        outs.append(acc[0:V_HD, :] / acc[V_HD:V_HD + 1, :])
    return outs


def _kvt_chunks(k_ref, vt_ref, k_sl, h, n_chunks):
    return [(lambda c=c: k_ref[0, c * CK:(c + 1) * CK, k_sl],
             lambda c=c: vt_ref[0, h * VT_ROWS:(h + 1) * VT_ROWS, c * CK:(c + 1) * CK]) for c in range(n_chunks)]


def _kv_chunks(k_ref, v_ref, k_sl, v_sl, n_chunks, first=0, start=None, bias_fn=None):
    out = []
    for c in range(n_chunks):
        if start is None:
            rows = slice((first + c) * CK, (first + c + 1) * CK)
        else:
            rows = pl.ds(start + c * CK, CK)
        out.append((lambda rows=rows: k_ref[0, rows, k_sl],
                    lambda rows=rows: v_ref[0, rows, v_sl],
                    None if bias_fn is None else functools.partial(bias_fn, c)))
    return out


def _na_kernel(q_ref, k_ref, v_ref, tab_ref, o_ref, s_ref, *, rows):
    i = pl.program_id(1)
    r0 = (i - 1) * NA_R
    s0 = jnp.clip(r0 - NA_KH // 2, 0, rows - NA_W)

    def bias_chunk(h, c):
        lane = lax.broadcasted_iota(jnp.int32, (1, 2 * GRID_W), 1)
        row_blocks = []
        for qr in range(NA_R):
            q_row = r0 + qr
            lo = jnp.clip(q_row - NA_KH // 2, 0, rows - NA_KH)
            pieces = []
            for u in range(CK // (2 * GRID_W)):
                k_row = s0 + c * (CK // GRID_W) + 2 * u
                pen = [jnp.where((k_row + d >= lo) & (k_row + d < lo + NA_KH), 0.0, -1e30) for d in range(2)]
                idx = jnp.clip(k_row - q_row + (NA_KH - 1), -1, 2 * NA_KH - 1) + 1
                pieces.append(tab_ref[h, idx] + jnp.where(lane < GRID_W, pen[0], pen[1]))
            row_blocks.append(jnp.concatenate(pieces, axis=-1))
        return jnp.concatenate(row_blocks, axis=0)

    def run(window_start):
        jobs = []
        for h in range(HEADS):
            k_sl = slice(h * NA_HD, (h + 1) * NA_HD)
            v_sl = slice(h * V_AUG, (h + 1) * V_AUG)
            chunks = _kv_chunks(k_ref, v_ref, k_sl, v_sl, 1)
            if window_start is not None:
                chunks += _kv_chunks(k_ref, v_ref, k_sl, v_sl, NA_W * GRID_W // CK, start=window_start,
                                     bias_fn=functools.partial(bias_chunk, h))
            jobs.append((q_ref[0, :, k_sl], chunks))
        o_ref[0] = jnp.concatenate(_attend(jobs, s_ref), axis=-1).astype(BF16)

    @pl.when(i == 0)
    def _():
        run(None)

    @pl.when(i > 0)
    def _():
        run(pl.multiple_of(N_CTX + s0 * GRID_W, GRID_W))


def _na_bias_table(rel_bias):
    cq = np.arange(GRID_W)
    col_lo = np.clip(cq - NA_KW // 2, 0, GRID_W - NA_KW)
    col_ok = (cq[None, :] >= col_lo[:, None]) & (cq[None, :] < col_lo[:, None] + NA_KW)
    col_off = np.clip(cq[None, :] - cq[:, None], 1 - NA_KW, NA_KW - 1) + (NA_KW - 1)
    col_sel = jnp.asarray(col_off[..., None] == np.arange(2 * NA_KW - 1), F32)
    t1 = jnp.einsum("hab,qkb->haqk", rel_bias * LOG2E, col_sel, precision=lax.Precision.HIGHEST)
    t1 = jnp.where(col_ok[None, None], t1, -1e30)
    fill = jnp.full((HEADS, 1, GRID_W, GRID_W), -1e30, F32)
    ext = jnp.concatenate([fill, t1, fill, fill], axis=1)
    return jnp.concatenate([ext[:, :-1], ext[:, 1:]], axis=-1)


def _mla_kernel(q_ref, k_ref, v_ref, o_ref, s_ref):
    i = pl.program_id(1)

    def run(n_chunks):
        jobs = []
        for h in range(HEADS):
            sl = slice(h * MLA_QK_PAD, (h + 1) * MLA_QK_PAD)
            jobs.append((q_ref[0, :, sl], _kvt_chunks(k_ref, v_ref, sl, h, n_chunks)))
        o_ref[0] = jnp.concatenate(_attend_t(jobs, s_ref), axis=0).T.astype(BF16)

    @pl.when(i == 0)
    def _():
        run(1)

    @pl.when(i > 0)
    def _():
        run(k_ref.shape[1] // CK)


def _diff_kernel(q_ref, k_ref, v_ref, lam_ref, gsub_ref, o_ref, s_ref, *, lam_init):
    i = pl.program_id(1)
    lv = lam_ref[...]
    lam = (jnp.exp(jnp.sum(lv[0:1] * lv[1:2], axis=-1, keepdims=True))
           - jnp.exp(jnp.sum(lv[2:3] * lv[3:4], axis=-1, keepdims=True)) + lam_init)

    def run(n_chunks):
        first = lax.broadcasted_iota(jnp.int32, (TM, 2 * DIFF_QK), 1) < DIFF_QK
        jobs = []
        for h in range(HEADS):
            sl = slice(h * 2 * DIFF_QK, (h + 1) * 2 * DIFF_QK)
            chunks = _kvt_chunks(k_ref, v_ref, sl, h, n_chunks)
            qh = q_ref[0, :, sl]
            zero = jnp.zeros_like(qh)
            jobs += [(jnp.where(first, qh, zero), chunks), (jnp.where(first, zero, qh), chunks)]
        ot = _attend_t(jobs, s_ref)
        outs = []
        for h in range(HEADS):
            d = ot[2 * h] - lam * ot[2 * h + 1]
            ms = jnp.mean(d * d, axis=0, keepdims=True)
            outs.append(d * lax.rsqrt(ms + EPS) * gsub_ref[...] * (1.0 - lam_init))
        o_ref[0] = jnp.concatenate(outs, axis=0).T.astype(BF16)

    @pl.when(i == 0)
    def _():
        run(1)

    @pl.when(i > 0)
    def _():
        run(k_ref.shape[1] // CK)


def _attention_kernel(na_q, na_k, na_v, na_tab, mla_q, mla_k, mla_vt, diff_q, diff_k, diff_vt, lam, gsub,
                      na_o, mla_o, diff_o, na_s, mla_s, diff_s, *, rows, lam_init):
    _na_kernel(na_q, na_k, na_v, na_tab, na_o, na_s, rows=rows)
    _mla_kernel(mla_q, mla_k, mla_vt, mla_o, mla_s)
    _diff_kernel(diff_q, diff_k, diff_vt, lam, gsub, diff_o, diff_s, lam_init=lam_init)


def _attention(na_qkv, na_table, mla_qk, mla_vt, diff_qk, diff_vt, lam_vecs, g_sub, lam_init):
    bsz, t, _ = na_qkv.shape
    rows = (t - N_CTX) // GRID_W
    g_sub_t = jnp.broadcast_to(g_sub.reshape(DIFF_V, 1), (DIFF_V, TM))
    q_tile = lambda w_: pl.BlockSpec((1, TM, w_), lambda b, i: (b, i, 0))
    keys = lambda w_, j: pl.BlockSpec((1, t, w_), lambda b, i: (b, 0, j))
    vt_all = pl.BlockSpec((1, HEADS * VT_ROWS, t), lambda b, i: (b, 0, 0))
    out = jax.ShapeDtypeStruct((bsz, t, 256), BF16)
    return pl.pallas_call(
        functools.partial(_attention_kernel, rows=rows, lam_init=lam_init), grid=(bsz, t // TM),
        in_specs=[q_tile(256), keys(256, 1), keys(512, 1), _const_spec(na_table.shape),
                  q_tile(512), keys(512, 1), vt_all,
                  q_tile(256), keys(256, 1), vt_all, _const_spec((4, DIFF_QK)), _const_spec((DIFF_V, TM))],
        out_specs=[q_tile(256), q_tile(256), q_tile(256)],
        out_shape=[out, out, out],
        scratch_shapes=[pltpu.VMEM((2, TM, CK + NA_W * GRID_W), F32), pltpu.VMEM((2, t, TM), F32),
                        pltpu.VMEM((2, t, TM), F32)],
        compiler_params=_cparams(2), name="attention",
    )(na_qkv, na_qkv, na_qkv, na_table, mla_qk, mla_qk, mla_vt, diff_qk, diff_qk, diff_vt, lam_vecs, g_sub_t)


def _softplus(x):
    return jnp.maximum(x, 0.0) + jnp.log1p(jnp.exp(-jnp.abs(x)))


def _ssd_kernel(raw_ref, convw_ref, convb_ref, dtb_ref, alog_ref, dskip_ref, gnorm_ref,
                tril_ref, triu_ref, mlow_ref, mupp_ref, o_ref, xact_ref, yacc_ref, state_ref):
    t = raw_ref.shape[1]
    nt = t // TM
    xbc0 = SSD_INNER
    dt0 = SSD_INNER + SSD_XBC

    cw = convw_ref[...]
    cb = convb_ref[...]
    for j in range(nt):
        lo = j * TM
        cur = raw_ref[0, lo:lo + TM, xbc0:xbc0 + SSD_XBC]
        zeros8 = jnp.zeros((8, SSD_XBC), F32)
        prev = raw_ref[0, lo - 8:lo, xbc0:xbc0 + SSD_XBC] if j >= 2 else zeros8
        nxt = raw_ref[0, lo + TM:lo + TM + 8, xbc0:xbc0 + SSD_XBC] if 1 <= j < nt - 1 else zeros8
        u = jnp.concatenate([prev, cur, nxt], axis=0)
        acc = cb
        for kk in range(SSD_CONV):
            off = 8 - SSD_CONV // 2 + kk
            acc = acc + cw[kk:kk + 1, :] * u[off:off + TM, :]
        xact_ref[lo:lo + TM, :] = _silu(acc)

    a_pad = -jnp.exp(alog_ref[...])
    for d in range(2):
        tri_ref = tril_ref if d == 0 else triu_ref
        off_ref = mlow_ref if d == 0 else mupp_ref
        state_ref[...] = jnp.zeros_like(state_ref)

        def chunk(c, carry, d=d, tri_ref=tri_ref, off_ref=off_ref):
            if d == 0:
                blk = c
            else:
                blk = jnp.where(c == 0, 0, nt - c)
            off = pl.multiple_of(blk * TM, TM)
            rows = pl.ds(off, TM)
            dt = _softplus(raw_ref[0, rows, dt0:dt0 + 128] + dtb_ref[...])
            la = dt * a_pad
            cum = _split_dot_left(tri_ref[...], la)
            cum_t = cum.T
            dt_t = dt.T
            total = cum[TM - 1:TM, :] if d == 0 else cum[0:1, :]
            e_in = jnp.exp(cum)
            w_t = (jnp.exp(total - cum) * dt).T
            e_tot = jnp.exp(total)
            xs = xact_ref[rows, 0:SSD_INNER]
            xs_b = xs.astype(BF16)
            ys = []
            for g in range(SSD_GROUPS):
                bm = xact_ref[rows, SSD_INNER + g * SSD_STATE:SSD_INNER + (g + 1) * SSD_STATE]
                cm_b = xact_ref[rows, SSD_INNER + (SSD_GROUPS + g) * SSD_STATE:
                                SSD_INNER + (SSD_GROUPS + g + 1) * SSD_STATE].astype(BF16)
                gm = _dot_nt(cm_b, bm.astype(BF16))
                bm_t = bm.T
                for hh in range(HEADS // SSD_GROUPS):
                    h = g * (HEADS // SSD_GROUPS) + hh
                    j = d * HEADS + h
                    x_b = xs_b[:, h * SSD_HD:(h + 1) * SSD_HD]
                    dec = jnp.exp(cum[:, j:j + 1] - cum_t[j:j + 1, :] + off_ref[...])
                    y_d = _dot((gm * dec * dt_t[j:j + 1, :]).astype(BF16), x_b)
                    st = state_ref[h]
                    y_o = e_in[:, j:j + 1] * _dot(cm_b, st.astype(BF16))
                    new = _dot((bm_t * w_t[j:j + 1, :]).astype(BF16), x_b)
                    state_ref[h] = st * e_tot[:, j:j + 1] + new
                    ys.append(y_d + y_o + dskip_ref[d, h] * xs[:, h * SSD_HD:(h + 1) * SSD_HD])
            y = jnp.concatenate(ys, axis=-1)
            if d == 0:
                yacc_ref[rows, :] = y
            else:
                yacc_ref[rows, :] += y
            return carry

        lax.fori_loop(0, nt, chunk, 0)

    for j in range(nt):
        lo = j * TM
        y = yacc_ref[lo:lo + TM, :] * _silu(raw_ref[0, lo:lo + TM, 0:SSD_INNER])
        o_ref[0, lo:lo + TM, :] = _rms(y, gnorm_ref[...]).astype(BF16)


def _ssd(ssd_raw, lp, consts):
    bsz, t, _ = ssd_raw.shape
    c1 = lambda shape: pl.BlockSpec(shape, lambda b: (0,) * len(shape))
    return pl.pallas_call(
        _ssd_kernel, grid=(bsz,),
        in_specs=[pl.BlockSpec((1, t, SSD_W), lambda b: (b, 0, 0)),
                  c1((SSD_CONV, SSD_XBC)), c1((1, SSD_XBC)), c1((1, 128)), c1((1, 128)),
                  pl.BlockSpec(memory_space=pltpu.SMEM),
                  c1((1, SSD_INNER)), c1((TM, TM)), c1((TM, TM)), c1((TM, TM)), c1((TM, TM))],
        out_specs=pl.BlockSpec((1, t, SSD_INNER), lambda b: (b, 0, 0)),
        out_shape=jax.ShapeDtypeStruct((bsz, t, SSD_INNER), BF16),
        scratch_shapes=[pltpu.VMEM((t, SSD_XBC), F32), pltpu.VMEM((t, SSD_INNER), F32),
                        pltpu.VMEM((HEADS, SSD_STATE, SSD_HD), F32)],
        compiler_params=_cparams(1), name="ssd",
    )(ssd_raw, lp["ssd_convw"], lp["ssd_convb"], lp["ssd_dtb"], lp["ssd_alog"], lp["ssd_dskip"],
      lp["ssd_gnorm"], consts["tril"], consts["triu"], consts["mlow"], consts["mupp"])


def _outproj_kernel(na_ref, mla_ref, diff_ref, ssd_ref, h_ref, mod_ref, wout_ref, gffn_ref,
                    wrh_ref, wrl_ref, br_ref, lstrict_ref, ustrict_ref,
                    hout_ref, f_ref, route_ref, cnt_ref, meta_ref, run_ref):
    first = (pl.program_id(0) == 0) & (pl.program_id(1) == 0)

    @pl.when(first)
    def _():
        run_ref[...] = jnp.zeros_like(run_ref)

    o = (_dot(na_ref[0], wout_ref[0:256, :]) + _dot(mla_ref[0], wout_ref[256:512, :])
         + _dot(diff_ref[0], wout_ref[512:768, :]) + _dot(ssd_ref[0], wout_ref[768:1024, :]))
    gate = mod_ref[0, 0, 2:3, :]
    hn = h_ref[0] + gate * o
    hout_ref[0] = hn
    f = _rms(hn, gffn_ref[...]) * (1.0 + mod_ref[0, 0, 4:5, :]) + mod_ref[0, 0, 3:4, :]
    f_ref[0] = f.astype(BF16)

    f_hi = f.astype(BF16)
    f_lo = (f - f_hi.astype(F32)).astype(BF16)
    logits = _dot(f_hi, wrh_ref[...]) + _dot(f_lo, wrh_ref[...]) + _dot(f_hi, wrl_ref[...]) + br_ref[...]
    lane = lax.broadcasted_iota(jnp.int32, logits.shape, 1)
    lane_f = lane.astype(F32)
    neg = jnp.float32(-jnp.inf)
    big = jnp.float32(1e9)
    gl = jnp.where(lane < MOE_GROUPS, logits, neg)
    gmax = jnp.max(gl, axis=-1, keepdims=True)
    g_top_p = 1.0 / jnp.sum(jnp.exp(gl - gmax), axis=-1, keepdims=True)
    g_top = jnp.min(jnp.where(gl == gmax, lane_f, big), axis=-1, keepdims=True).astype(jnp.int32)
    in_group = (lane >= MOE_GROUPS) & (lane < MOE_GROUPS + MOE_EXPERTS) & (((lane - MOE_GROUPS) // MOE_EPG) == g_top)
    el = jnp.where(in_group, logits, neg)
    m1 = jnp.max(el, axis=-1, keepdims=True)
    i1 = jnp.min(jnp.where(el == m1, lane_f, big), axis=-1, keepdims=True)
    el2 = jnp.where(lane_f == i1, neg, el)
    m2 = jnp.max(el2, axis=-1, keepdims=True)
    i2 = jnp.min(jnp.where(el2 == m2, lane_f, big), axis=-1, keepdims=True)
    x2 = jnp.exp(m2 - m1)
    w1 = g_top_p / (1.0 + x2)
    w2 = g_top_p * x2 / (1.0 + x2)
    e1 = i1 - MOE_GROUPS
    e2 = i2 - MOE_GROUPS

    onehot = ((lane_f == e1) | (lane_f == e2)).astype(F32)
    tile_cnt = jnp.floor((jnp.sum(onehot, axis=0, keepdims=True) + (RUN_ALIGN - 1.0)) * (1.0 / RUN_ALIGN)) * RUN_ALIGN
    tile_start = _dot(jnp.broadcast_to(tile_cnt, (8, 128)).astype(BF16), ustrict_ref[...])[0:1]
    pos = _dot(lstrict_ref[...], onehot.astype(BF16)) + tile_start
    p1 = jnp.sum(jnp.where(lane_f == e1, pos, 0.0), axis=-1, keepdims=True)
    p2 = jnp.sum(jnp.where(lane_f == e2, pos, 0.0), axis=-1, keepdims=True)
    run_old = run_ref[...]
    run_ref[...] = run_old + tile_cnt
    cnt_ref[...] = run_old + tile_cnt
    route = jnp.zeros(logits.shape, F32)
    for idx, val in enumerate((p1, p2, w1, w2)):
        route = jnp.where(lane == idx, val, route)
    route_ref[0] = route
    sub = lax.broadcasted_iota(jnp.int32, (8, 128), 0)
    meta_ref[0] = jnp.where(sub == 0, tile_cnt, jnp.where(sub == 1, tile_start, jnp.where(sub == 2, run_old, 0.0)))


def _outproj(mix, h, modsel, wout, gffn, lp, consts):
    bsz, t, h_args, h_specs, h_scratch = _stream_operand(h)
    row = lambda w_: pl.BlockSpec((1, TM, w_), lambda b, i: (b, i, 0))
    kern = _merge_stream(_outproj_kernel, 4) if h_scratch else _outproj_kernel
    return pl.pallas_call(
        kern, grid=(bsz, t // TM),
        in_specs=[row(256), row(256), row(256), row(256)] + h_specs + [
                  pl.BlockSpec((1, 1, 6, D), lambda b, i: (b, jnp.minimum(i, 1), 0, 0)),
                  _const_spec((D, D)), _const_spec((1, D)), _const_spec((D, 128)), _const_spec((D, 128)),
                  _const_spec((1, 128)), _const_spec((TM, TM)), _const_spec((128, 128))],
        out_specs=[row(D), row(D), row(128), _const_spec((1, 128)),
                   pl.BlockSpec((1, 8, 128), lambda b, i: (b * (t // TM) + i, 0, 0))],
        out_shape=[jax.ShapeDtypeStruct((bsz, t, D), F32), jax.ShapeDtypeStruct((bsz, t, D), BF16),
                   jax.ShapeDtypeStruct((bsz, t, 128), F32), jax.ShapeDtypeStruct((1, 128), F32),
                   jax.ShapeDtypeStruct((bsz * (t // TM), 8, 128), F32)],
        scratch_shapes=[pltpu.VMEM((1, 128), F32)] + h_scratch,
        compiler_params=_cparams(2), name="outproj_router",
    )(*mix, *h_args, modsel, wout, gffn, lp["wr_hi"], lp["wr_lo"], lp["br"], consts["lstrict"], consts["ustrict"])


def _run_dmas(tile, n_ref, ls_ref, gs_ref, make_copy):
    def walk(wait):
        for e in range(MOE_EXPERTS):
            n = n_ref[tile * MOE_EXPERTS + e]
            l0 = ls_ref[tile * MOE_EXPERTS + e]
            g0 = gs_ref[tile * MOE_EXPERTS + e]
            for b in range(RUN_BITS[1] - 1, RUN_BITS[0] - 1, -1):
                off = (n >> (b + 1)) << (b + 1)

                @pl.when(((n >> b) & 1) == 1)
                def _(b=b, off=off, l0=l0, g0=g0):
                    c = make_copy(pl.multiple_of(l0 + off, RUN_ALIGN), pl.multiple_of(g0 + off, RUN_ALIGN), 1 << b)
                    if wait:
                        c.wait()
                    else:
                        c.start(priority=b % 2)
    return walk


def _dispatch_kernel(n_ref, ls_ref, gs_ref, pe_ref, nu_ref, f_ref, route_ref, xs_ref, sbuf_ref, zbuf_ref,
                     sem, zsem, *, first_tail, n_blocks):
    tile = pl.program_id(0)

    def pad_copies(action):
        for e in range(MOE_EXPERTS):
            prev = pe_ref[e - 1] if e > 0 else 0

            @pl.when(pe_ref[e] > prev)
            def _(e=e):
                start = pl.multiple_of(pe_ref[e] - MOE_MB, MOE_MB)
                action(pltpu.make_async_copy(zbuf_ref, xs_ref.at[pl.ds(start, MOE_MB)], zsem))
        for blk in range(first_tail, n_blocks):
            @pl.when(blk >= nu_ref[0])
            def _(blk=blk):
                action(pltpu.make_async_copy(zbuf_ref, xs_ref.at[pl.ds(blk * MOE_MB, MOE_MB)], zsem))

    @pl.when(tile == 0)
    def _():
        zbuf_ref[...] = jnp.zeros_like(zbuf_ref)
        pad_copies(lambda c: c.start())
        pad_copies(lambda c: c.wait())

    buf = tile % 2
    pos = route_ref[...].T
    slot = lax.broadcasted_iota(jnp.int32, (SORT_ROWS, TM), 0).astype(F32)
    perm = ((slot == pos[0:1, :]) | (slot == pos[1:2, :])).astype(BF16)
    sbuf_ref[buf] = _dot(perm, f_ref[...]).astype(BF16)

    def runs(t, s):
        return _run_dmas(t, n_ref, ls_ref, gs_ref, lambda l, g, size: pltpu.make_async_copy(
            sbuf_ref.at[s, pl.ds(l, size)], xs_ref.at[pl.ds(g, size)], sem.at[s]))

    runs(tile, buf)(False)

    @pl.when(tile >= 1)
    def _():
        runs(tile - 1, 1 - buf)(True)

    @pl.when(tile == pl.num_programs(0) - 1)
    def _():
        runs(tile, buf)(True)


def _dispatch(plan, f2d, route2d, cap):
    n_tok = f2d.shape[0]
    n_blocks = cap // MOE_MB
    first_tail = 2 * n_tok // MOE_MB
    assert 2 * TM + MOE_EXPERTS * (RUN_ALIGN - 1) <= SORT_ROWS
    return pl.pallas_call(
        functools.partial(_dispatch_kernel, first_tail=first_tail, n_blocks=n_blocks),
        grid_spec=pltpu.PrefetchScalarGridSpec(
            num_scalar_prefetch=5, grid=(n_tok // TM,),
            in_specs=[pl.BlockSpec((TM, D), lambda i, *_: (i, 0)),
                      pl.BlockSpec((TM, 128), lambda i, *_: (i, 0))],
            out_specs=pl.BlockSpec(memory_space=pl.ANY),
            scratch_shapes=[pltpu.VMEM((2, SORT_ROWS, D), BF16), pltpu.VMEM((MOE_MB, D), BF16),
                            pltpu.SemaphoreType.DMA((2,)), pltpu.SemaphoreType.DMA(())]),
        out_shape=jax.ShapeDtypeStruct((cap, D), BF16),
        compiler_params=_cparams(1), name="moe_dispatch",
    )(plan["n"], plan["ls"], plan["gs"], plan["pad_end"], plan["n_used"], f2d, route2d)


def _experts_kernel(be_ref, nb_ref, x_ref, wg_ref, wu_ref, wd_ref, y_ref, wgu_s, wd_s):
    i = pl.program_id(0)

    @pl.when((i == 0) | (be_ref[i] != be_ref[jnp.maximum(i - 1, 0)]))
    def _():
        wgu_s[:, 0:MOE_FF] = wg_ref[0, 0].astype(BF16)
        wgu_s[:, MOE_FF:2 * MOE_FF] = wu_ref[0, 0].astype(BF16)
        wd_s[...] = wd_ref[0, 0].astype(BF16)

    @pl.when(i < nb_ref[0])
    def _():
        gu = _dot(x_ref[...], wgu_s[...])
        a = _silu(gu[:, 0:MOE_FF]) * gu[:, MOE_FF:2 * MOE_FF]
        y_ref[...] = _dot(a.astype(BF16), wd_s[...]).astype(BF16)

    @pl.when(i >= nb_ref[0])
    def _():
        y_ref[...] = jnp.zeros_like(y_ref)


def _experts(plan, xs, w_gate, w_up, w_down, layer):
    cap = xs.shape[0]
    wspec = lambda a, b: pl.BlockSpec((1, 1, a, b), lambda i, be, nb: (layer, be[i], 0, 0))
    return pl.pallas_call(
        _experts_kernel,
        grid_spec=pltpu.PrefetchScalarGridSpec(
            num_scalar_prefetch=2, grid=(cap // MOE_MB,),
            in_specs=[pl.BlockSpec((MOE_MB, D), lambda i, be, nb: (i, 0)),
                      wspec(D, MOE_FF), wspec(D, MOE_FF), wspec(MOE_FF, D)],
            out_specs=pl.BlockSpec((MOE_MB, D), lambda i, be, nb: (i, 0)),
            scratch_shapes=[pltpu.VMEM((D, 2 * MOE_FF), BF16), pltpu.VMEM((MOE_FF, D), BF16)]),
        out_shape=jax.ShapeDtypeStruct((cap, D), BF16),
        compiler_params=_cparams(1), name="moe_experts",
    )(plan["block_e"], plan["n_used"], xs, w_gate, w_up, w_down)


def _combine_kernel(n_ref, ls_ref, gs_ref, h_ref, mod_ref, route_ref, y_ref, o_ref, ybuf_ref, sem, *, skip, nt):
    per_sample = pl.num_programs(1)
    step = pl.program_id(0) * per_sample + pl.program_id(1)
    buf = step % 2

    def runs(s, slot):
        tile = (s // per_sample) * nt + s % per_sample + skip
        return _run_dmas(tile, n_ref, ls_ref, gs_ref, lambda l, g, size: pltpu.make_async_copy(
            y_ref.at[pl.ds(g, size)], ybuf_ref.at[slot, pl.ds(l, size)], sem.at[slot]))

    @pl.when(step == 0)
    def _():
        ybuf_ref[...] = jnp.zeros_like(ybuf_ref)
        runs(step, buf)(False)

    @pl.when(step + 1 < pl.num_programs(0) * per_sample)
    def _():
        runs(step + 1, 1 - buf)(False)

    runs(step, buf)(True)

    slot = lax.broadcasted_iota(jnp.int32, (TM, SORT_ROWS), 1).astype(F32)
    r = route_ref[0]
    wm = (jnp.where(slot == r[:, 0:1], r[:, 2:3], 0.0) + jnp.where(slot == r[:, 1:2], r[:, 3:4], 0.0))
    w_hi = wm.astype(BF16)
    w_lo = (wm - w_hi.astype(F32)).astype(BF16)
    yv = ybuf_ref[buf]
    out = _dot(w_hi, yv) + _dot(w_lo, yv)
    o_ref[0] = h_ref[0] + mod_ref[0, 0, 5:6, :] * out


def _combine(plan, h, modsel, route, y, latent_only):
    bsz, t, _ = h.shape
    nt = t // TM
    skip = 1 if latent_only else 0
    row = lambda w_: pl.BlockSpec((1, TM, w_), lambda b, i, *_: (b, i + skip, 0))
    return pl.pallas_call(
        functools.partial(_combine_kernel, skip=skip, nt=nt),
        grid_spec=pltpu.PrefetchScalarGridSpec(
            num_scalar_prefetch=3, grid=(bsz, nt - skip),
            in_specs=[row(D),
                      pl.BlockSpec((1, 1, 6, D), lambda b, i, *_: (b, jnp.minimum(i + skip, 1), 0, 0)),
                      row(128),
                      pl.BlockSpec(memory_space=pl.ANY)],
            out_specs=pl.BlockSpec((1, TM, D), lambda b, i, *_: (b, i, 0)),
            scratch_shapes=[pltpu.VMEM((2, SORT_ROWS, D), BF16), pltpu.SemaphoreType.DMA((2,))]),
        out_shape=jax.ShapeDtypeStruct((bsz, t - skip * TM, D), F32),
        compiler_params=_cparams(2), name="moe_combine",
    )(plan["n"], plan["ls"], plan["gs"], h, modsel, route, y)


def _moe_plan(meta, counts, n_blocks):
    cnt = counts[0, :MOE_EXPERTS].astype(jnp.int32)
    padded = (cnt + MOE_MB - 1) // MOE_MB * MOE_MB
    pad_end = jnp.cumsum(padded)
    pad_start = pad_end - padded
    m = meta[:, 0:3, 0:MOE_EXPERTS].astype(jnp.int32)
    blk0 = jnp.arange(n_blocks, dtype=jnp.int32) * MOE_MB
    block_e = jnp.minimum(jnp.sum(blk0[:, None] >= pad_end[None, :], axis=-1), MOE_EXPERTS - 1)
    return {"n": m[:, 0].reshape(-1), "ls": m[:, 1].reshape(-1), "gs": (pad_start[None, :] + m[:, 2]).reshape(-1),
            "pad_end": pad_end.astype(jnp.int32), "n_used": (pad_end[-1:] // MOE_MB).astype(jnp.int32),
            "block_e": block_e.astype(jnp.int32)}


def _block_diag(n, seg):
    idx = np.arange(n) // seg
    return jnp.asarray(idx[:, None] == idx[None, :], BF16)


def _constants():
    lower = np.tril(np.ones((TM, TM), np.float32))
    upper = np.triu(np.ones((TM, TM), np.float32))
    return {"bd64": _block_diag(256, 64), "bd128": _block_diag(512, 128), "bd32": _block_diag(256, 32),
            "tril": jnp.asarray(lower, BF16), "triu": jnp.asarray(upper, BF16),
            "mlow": jnp.asarray((lower - 1.0) * 1e30, F32), "mupp": jnp.asarray((upper - 1.0) * 1e30, F32),
            "lstrict": jnp.asarray(np.tril(np.ones((TM, TM)), -1), BF16),
            "ustrict": jnp.asarray(np.triu(np.ones((128, 128)), 1), BF16)}


def _rope_tables(n_lat, t):
    n_freq = 8
    inv = jnp.power(10000.0, -jnp.arange(n_freq, dtype=F32) / n_freq)
    tok = jnp.arange(n_lat, dtype=jnp.int32)
    row = (tok // GRID_W).astype(F32)
    col = (tok % GRID_W).astype(F32)
    ang = jnp.concatenate([row[:, None] * inv, col[:, None] * inv], axis=-1)
    n_c = t - n_lat
    cos = jnp.concatenate([jnp.ones((n_c, 16), F32), jnp.cos(ang)], axis=0)
    sin = jnp.concatenate([jnp.zeros((n_c, 16), F32), jnp.sin(ang)], axis=0)
    z16 = jnp.zeros((t, 16), F32)
    one = lambda w_: jnp.ones((t, w_), F32)
    zero = lambda w_: jnp.zeros((t, w_), F32)
    mc = jnp.concatenate([one(64), cos, cos, one(32)], axis=-1)
    ms1 = jnp.concatenate([zero(64), -sin, z16, zero(32)], axis=-1)
    ms2 = jnp.concatenate([zero(64), z16, sin, zero(32)], axis=-1)
    dc = jnp.concatenate([cos, cos], axis=-1)
    ds1 = jnp.concatenate([-sin, z16], axis=-1)
    ds2 = jnp.concatenate([z16, sin], axis=-1)
    tile = lambda a, n: jnp.tile(a, (1, n))
    return {"mc": tile(mc, 4), "ms1": tile(ms1, 4), "ms2": tile(ms2, 4),
            "dc": tile(dc, 8), "ds1": tile(ds1, 8), "ds2": tile(ds2, 8)}


def _pad_heads(w, width, padded):
    lead = w.shape[:-1]
    w = w.reshape(lead + (HEADS, width))
    w = jnp.pad(w, [(0, 0)] * len(lead) + [(0, 0), (0, padded - width)])
    return w.reshape(lead + (HEADS * padded,))


def _pack_kernel(w_ref, o_ref):
    w = w_ref[0]
    z = lambda n: jnp.zeros((w.shape[0], n), F32)

    def value_heads(lo):
        out = []
        for h in range(HEADS):
            out += [w[:, lo + h * V_HD:lo + (h + 1) * V_HD], z(V_AUG - V_HD)]
        return out

    na = [w[:, 0:512]] + value_heads(512)
    mla = [w[:, 768:1152], z(MLA_NOPE), w[:, 1152:1184], z(MLA_QK_PAD - MLA_QK)]
    diff = [w[:, 1184:1696]] + value_heads(1696)
    ssd = [w[:, 1952:2984], z(SSD_W - 1032)]
    o_ref[0] = jnp.concatenate(na + mla + diff + ssd, axis=-1).astype(BF16)


def _pack_w_in(w_in):
    n_layers, _, n_in = w_in.shape
    return pl.pallas_call(
        _pack_kernel, grid=(n_layers, D // TM),
        in_specs=[pl.BlockSpec((1, TM, n_in), lambda l, i: (l, i, 0))],
        out_specs=pl.BlockSpec((1, TM, P_W), lambda l, i: (l, i, 0)),
        out_shape=jax.ShapeDtypeStruct((n_layers, D, P_W), BF16),
        compiler_params=_cparams(2), name="pack_w_in",
    )(w_in)


def _layer_params(l, p):
    row = lambda a: a.reshape(1, -1)
    t4 = lambda a: jnp.tile(a.reshape(1, -1), (1, HEADS))
    wqb = _pad_heads(p["mla_w_qb"][l], MLA_QK, MLA_QK_PAD)
    wkvb = p["mla_w_kvb"][l].reshape(MLA_KV_RANK, HEADS, MLA_NOPE + MLA_V)
    wkvb = jnp.concatenate([_pad_heads(wkvb[:, :, :MLA_NOPE].reshape(MLA_KV_RANK, -1), MLA_NOPE, MLA_QK_PAD),
                            _pad_heads(wkvb[:, :, MLA_NOPE:].reshape(MLA_KV_RANK, -1), MLA_V, V_AUG)], axis=-1)
    gpad = lambda g: jnp.tile(jnp.pad(g, (0, MLA_QK_PAD - MLA_QK)).reshape(1, -1), (1, HEADS))
    lane8 = lambda a: jnp.pad(a.reshape(1, -1), ((0, 0), (0, 128 - 2 * HEADS)))
    wr = jnp.concatenate([p["moe_w_group"][l], p["moe_w_expert"][l],
                          jnp.zeros((D, 128 - MOE_GROUPS - MOE_EXPERTS), F32)], axis=-1)
    wr_hi = wr.astype(BF16)
    br = jnp.concatenate([p["moe_b_group"][l], p["moe_b_expert"][l],
                          jnp.zeros((128 - MOE_GROUPS - MOE_EXPERTS,), F32)]).reshape(1, 128)
    return {
        "g_mix": row(p["g_mix"][l]), "g_ffn": row(p["g_ffn"][l]),
        "na_gq": t4(p["na_g_q"][l]), "na_gk": t4(p["na_g_k"][l]),
        "mla_gqa": row(p["mla_g_qa"][l]), "mla_wqb": wqb.astype(BF16),
        "mla_gkva": row(p["mla_g_kva"][l]), "mla_wkvb": wkvb.astype(BF16),
        "mla_gq": gpad(p["mla_g_q"][l]), "mla_gk": gpad(p["mla_g_k"][l]),
        "diff_gq": jnp.tile(p["diff_g_q"][l].reshape(1, -1), (1, 8)),
        "diff_gk": jnp.tile(p["diff_g_k"][l].reshape(1, -1), (1, 8)),
        "diff_lam": p["diff_lambda"][l], "diff_gsub": row(p["diff_g_sub"][l]),
        "ssd_convw": p["ssd_conv_w"][l], "ssd_convb": row(p["ssd_conv_b"][l]),
        "ssd_dtb": lane8(p["ssd_dt_bias"][l]), "ssd_alog": lane8(p["ssd_a_log"][l]),
        "ssd_dskip": p["ssd_d"][l], "ssd_gnorm": row(p["ssd_g_norm"][l]),
        "wr_hi": wr_hi, "wr_lo": (wr - wr_hi.astype(F32)).astype(BF16), "br": br,
    }


def _mixers(h, modsel, w_in_l, lp, consts, tabs, bias, lam_init):
    na_qkv, mla_qk, mla_vt, diff_qk, diff_vt, ssd_raw = _inproj(h, modsel, lp["g_mix"], w_in_l, consts, lp, tabs)
    return (*_attention(na_qkv, bias, mla_qk, mla_vt, diff_qk, diff_vt, lp["diff_lam"], lp["diff_gsub"], lam_init),
            _ssd(ssd_raw, lp, consts))


def _moe(hn, f, route, counts, meta, modsel, w_gate, w_up, w_down, layer, latent_only):
    bsz, t, _ = hn.shape
    n_asg = bsz * t * 2
    assert n_asg % MOE_MB == 0
    n_slots = n_asg + (bsz * t // TM) * MOE_EXPERTS * (RUN_ALIGN - 1)
    n_blocks = -(-n_slots // MOE_MB) + MOE_EXPERTS
    plan = _moe_plan(meta, counts, n_blocks)
    xs = _dispatch(plan, f.reshape(bsz * t, D), route.reshape(bsz * t, 128), n_blocks * MOE_MB)
    y = _experts(plan, xs, w_gate, w_up, w_down, layer)
    return _combine(plan, hn, modsel, route, y, latent_only)


def kernel(x, c, ctx, c_ctx, w_mod, b_mod, g_mix, w_in, w_out, na_g_q, na_g_k, na_rel_bias,
           mla_g_qa, mla_w_qb, mla_g_kva, mla_w_kvb, mla_g_q, mla_g_k,
           diff_g_q, diff_g_k, diff_lambda, diff_g_sub,
           ssd_conv_w, ssd_conv_b, ssd_dt_bias, ssd_a_log, ssd_d, ssd_g_norm,
           g_ffn, moe_w_group, moe_b_group, moe_w_expert, moe_b_expert, moe_w_gate, moe_w_up, moe_w_down):
    p = dict(g_mix=g_mix, g_ffn=g_ffn, na_g_q=na_g_q, na_g_k=na_g_k,
             mla_g_qa=mla_g_qa, mla_w_qb=mla_w_qb, mla_g_kva=mla_g_kva, mla_w_kvb=mla_w_kvb,
             mla_g_q=mla_g_q, mla_g_k=mla_g_k, diff_g_q=diff_g_q, diff_g_k=diff_g_k,
             diff_lambda=diff_lambda, diff_g_sub=diff_g_sub,
             ssd_conv_w=ssd_conv_w, ssd_conv_b=ssd_conv_b, ssd_dt_bias=ssd_dt_bias, ssd_a_log=ssd_a_log,
             ssd_d=ssd_d, ssd_g_norm=ssd_g_norm, moe_w_group=moe_w_group, moe_b_group=moe_b_group,
             moe_w_expert=moe_w_expert, moe_b_expert=moe_b_expert)
    bsz, n_lat, _ = x.shape
    n_ctx = ctx.shape[1]
    assert n_ctx == N_CTX == TM == CK and n_lat % TM == 0 and bsz < 16
    t = n_ctx + n_lat
    n_layers = w_mod.shape[0]
    rows = n_lat // GRID_W
    assert rows % NA_R == 0 and rows >= NA_W and (NA_W * GRID_W) % CK == 0

    consts = _constants()
    tabs = _rope_tables(n_lat, t)
    w_in_p = _pack_w_in(w_in)
    w_out_b = w_out.astype(BF16)

    cvec = jnp.concatenate([c, c_ctx[None, :], jnp.zeros((16 - bsz - 1, D), F32)], axis=0)
    mod = _modulation(cvec, w_mod, b_mod).reshape(n_layers, 16, 6, D)

    h = (ctx, x)
    for l in range(n_layers):
        lp = _layer_params(l, p)
        modsel = jnp.stack([jnp.broadcast_to(mod[l, bsz][None], (bsz, 6, D)), mod[l, :bsz]], axis=1)
        lam_init = 0.8 - 0.6 * math.exp(-0.3 * l)
        bias = _na_bias_table(na_rel_bias[l])
        mix = _mixers(h, modsel, w_in_p[l], lp, consts, tabs, bias, lam_init)
        hn, f, route, counts, meta = _outproj(mix, h, modsel, w_out_b[l], lp["g_ffn"], lp, consts)
        h = _moe(hn, f, route, counts, meta, modsel, moe_w_gate, moe_w_up, moe_w_down, l,
                 latent_only=(l == n_layers - 1))
    return h
```
